```python
import math
import jax
import jax.numpy as jnp
from jax import lax
import numpy as np

D_MODEL = 1024
BATCH = 4
SEQ = 8192
DEPTH = 1
DEC_BATCH = 1
DEC_SEQ = 16384
PAST_LEN = 128

D_MIX = D_MODEL
D_A = D_MIX // 2
A_GROUPS = 4
A_DG = D_A // A_GROUPS
A_CHUNK = 128
D_B = D_MIX - D_A
B_HEADS = 4
B_DK = 128
B_DV = D_B // B_HEADS
D_QK = B_HEADS * B_DK
DN_CHUNK = 64
CONV_W = 5
N_DIR = 2
N_EXPERT_GROUPS = 4
EXPERTS_PER_GROUP = 8
N_EXPERTS = N_EXPERT_GROUPS * EXPERTS_PER_GROUP
TOP_K = 2
D_EXPERT = 512
MOE_BLOCK = 128
EPS = 1e-6
D_IN = 2 * D_A + 2 * D_QK + 2 * D_B + 2 * N_DIR * B_HEADS

kernel_name = "hymba_gmlp_gdn_hmoe_encoder"


def rms_norm(x, w):
    xf = x.astype(jnp.float32)
    y = xf * lax.rsqrt(jnp.mean(xf * xf, axis=-1, keepdims=True) + EPS)
    return y * w.astype(jnp.float32)


def l2_normalize(x):
    return x * lax.rsqrt(jnp.sum(x * x, axis=-1, keepdims=True) + EPS)


def spatial_gating_mixer(u_in, v_in, ln_w, ln_b, sp_w, sp_b, out_w):
    bsz, s, _ = u_in.shape
    nc = s // A_CHUNK
    u = jax.nn.gelu(u_in.astype(jnp.float32))
    v = jax.nn.gelu(v_in.astype(jnp.float32)).reshape(bsz, s, A_GROUPS, A_DG)
    mu = jnp.mean(v, axis=-1, keepdims=True)
    var = jnp.mean(jnp.square(v - mu), axis=-1, keepdims=True)
    v = (v - mu) * lax.rsqrt(var + EPS) * ln_w.astype(jnp.float32).reshape(A_GROUPS, A_DG) \
        + ln_b.astype(jnp.float32).reshape(A_GROUPS, A_DG)
    v = v.reshape(bsz, nc, A_CHUNK, A_GROUPS, A_DG)
    mixed = jnp.einsum('gpq,bnqgc->bnpgc', sp_w.astype(jnp.float32), v) \
        + sp_b.astype(jnp.float32).T[None, None, :, :, None]
    gated = u.reshape(bsz, nc, A_CHUNK, A_GROUPS, A_DG) * mixed
    gated = gated.reshape(bsz, s, A_GROUPS, A_DG)
    out = rms_norm(gated, out_w.reshape(A_GROUPS, A_DG))
    return out.reshape(bsz, s, D_A)


def short_conv(x, w):
    c = x.shape[-1]
    return lax.conv_general_dilated(
        x, w[:, None, :].astype(x.dtype), window_strides=(1,),
        padding=[(CONV_W // 2, CONV_W // 2)],
        dimension_numbers=('NWC', 'WIO', 'NWC'), feature_group_count=c)


def gated_delta_chunked(q, k, v, g, beta):
    bsz, s, h, dk = q.shape
    dv = v.shape[-1]
    n, c = s // DN_CHUNK, DN_CHUNK

    def to_chunks(t):
        t = t.reshape(bsz, n, c, h, *t.shape[3:])
        return jnp.moveaxis(t, 3, 2)

    q, k, v, g, beta = (to_chunks(t) for t in (q, k, v, g, beta))
    g = jnp.cumsum(g, axis=-1)
    pos = jnp.arange(c)
    incl = pos[:, None] >= pos[None, :]
    strict = pos[:, None] > pos[None, :]
    diff = jnp.where(incl, g[..., :, None] - g[..., None, :], 0.0)
    decay = jnp.where(incl, jnp.exp(diff), 0.0)
    kb = k * beta[..., None]
    lmat = jnp.eye(c, dtype=jnp.float32) + jnp.where(
        strict, jnp.einsum('bnhik,bnhjk->bnhij', kb, k) * decay, 0.0)
    u = lax.linalg.triangular_solve(lmat, v * beta[..., None], left_side=True,
                                    lower=True, unit_diagonal=True)
    w = lax.linalg.triangular_solve(lmat, kb * jnp.exp(g)[..., None], left_side=True,
                                    lower=True, unit_diagonal=True)
    qk = jnp.einsum('bnhik,bnhjk->bnhij', q, k) * decay
    qg = q * jnp.exp(g)[..., None]
    kg = k * jnp.exp(g[..., -1:] - g)[..., None]
    g_last = jnp.exp(g[..., -1])

    def step(state, xs):
        qg_c, kg_c, u_c, w_c, qk_c, gl_c = xs
        v_new = u_c - jnp.einsum('bhck,bhkv->bhcv', w_c, state)
        o_c = jnp.einsum('bhck,bhkv->bhcv', qg_c, state) + jnp.einsum('bhij,bhjv->bhiv', qk_c, v_new)
        state = state * gl_c[..., None, None] + jnp.einsum('bhck,bhcv->bhkv', kg_c, v_new)
        return state, o_c

    xs = tuple(jnp.moveaxis(t, 1, 0) for t in (qg, kg, u, w, qk, g_last))
    state0 = jnp.zeros((bsz, h, dk, dv), jnp.float32)
    _, o = lax.scan(step, state0, xs)
    o = jnp.moveaxis(o, 0, 1)
    return jnp.moveaxis(o, 2, 3).reshape(bsz, s, h, dv)


def gdn_mixer(q_in, k_in, v_in, z_in, a_in, b_in, conv_w, a_log, dt_bias, norm_w):
    bsz, s, _ = q_in.shape
    qkv = jax.nn.silu(short_conv(jnp.concatenate([q_in, k_in, v_in], axis=-1), conv_w))
    qkv = qkv.astype(jnp.float32)
    q, k, v = jnp.split(qkv, [D_QK, 2 * D_QK], axis=-1)
    q = l2_normalize(q.reshape(bsz, s, B_HEADS, B_DK)) * (B_DK ** -0.5)
    k = l2_normalize(k.reshape(bsz, s, B_HEADS, B_DK))
    v = v.reshape(bsz, s, B_HEADS, B_DV)
    a = a_in.astype(jnp.float32).reshape(bsz, s, N_DIR, B_HEADS)
    g = -jnp.exp(a_log.astype(jnp.float32)) * jax.nn.softplus(a + dt_bias.astype(jnp.float32))
    beta = jax.nn.sigmoid(b_in.astype(jnp.float32).reshape(bsz, s, N_DIR, B_HEADS))
    o_fwd = gated_delta_chunked(q, k, v, g[:, :, 0], beta[:, :, 0])
    o_bwd = gated_delta_chunked(jnp.flip(q, axis=1), jnp.flip(k, axis=1), jnp.flip(v, axis=1),
                                jnp.flip(g[:, :, 1], axis=1), jnp.flip(beta[:, :, 1], axis=1))
    o = o_fwd + jnp.flip(o_bwd, axis=1)
    z = z_in.astype(jnp.float32).reshape(bsz, s, B_HEADS, B_DV)
    o = rms_norm(o, norm_w) * jax.nn.silu(z)
    return o.reshape(bsz, s, D_B)


def hierarchical_moe(x, w_router_group, w_router_expert, w_gate, w_up, w_down):
    t, d = x.shape
    xf = x.astype(jnp.float32)
    group_p = jax.nn.softmax(xf @ w_router_group.astype(jnp.float32), axis=-1)
    g_w, g_idx = lax.top_k(group_p, 1)
    expert_logits = (xf @ w_router_expert.astype(jnp.float32)).reshape(t, N_EXPERT_GROUPS, EXPERTS_PER_GROUP)
    in_group = expert_logits[jnp.arange(t), g_idx[:, 0]]
    top_l, top_i = lax.top_k(in_group, TOP_K)
    weights = g_w * jax.nn.softmax(top_l, axis=-1)
    expert_id = g_idx * EXPERTS_PER_GROUP + top_i

    n_assign = t * TOP_K
    flat_e = expert_id.reshape(n_assign)
    flat_w = weights.reshape(n_assign)
    flat_tok = jnp.arange(n_assign, dtype=jnp.int32) // TOP_K
    order = jnp.argsort(flat_e)
    sorted_e = flat_e[order]
    sorted_tok = flat_tok[order]
    sorted_w = flat_w[order]
    counts = jnp.bincount(flat_e, length=N_EXPERTS)
    seg_start = jnp.cumsum(counts) - counts
    padded = (counts + MOE_BLOCK - 1) // MOE_BLOCK * MOE_BLOCK
    pad_end = jnp.cumsum(padded)
    pad_start = pad_end - padded
    dest = pad_start[sorted_e] + jnp.arange(n_assign) - seg_start[sorted_e]
    n_blocks = -(-n_assign // MOE_BLOCK) + N_EXPERTS
    slot_tok = jnp.full((n_blocks * MOE_BLOCK,), t, jnp.int32).at[dest].set(sorted_tok)
    block_expert = jnp.minimum(
        jnp.searchsorted(pad_end, jnp.arange(n_blocks) * MOE_BLOCK, side='right'), N_EXPERTS - 1)
    x_pad = jnp.concatenate([x, jnp.zeros((1, d), x.dtype)], axis=0)

    def run_block(args):
        tok, e = args
        xb = x_pad[tok]
        hb = jax.nn.silu(xb @ w_gate[e]) * (xb @ w_up[e])
        return hb @ w_down[e]

    y_slots = lax.map(run_block, (slot_tok.reshape(n_blocks, MOE_BLOCK), block_expert))
    y_sorted = y_slots.reshape(n_blocks * MOE_BLOCK, d)[dest]
    out = jnp.zeros((t, d), jnp.float32).at[sorted_tok].add(
        y_sorted.astype(jnp.float32) * sorted_w[:, None])
    return out.astype(x.dtype)


def encoder_layer(x, norm_mix_w, w_in, a_ln_w, a_ln_b, a_spatial_w, a_spatial_b, a_out_norm_w,
                  conv_w, a_log, dt_bias, gdn_norm_w, w_out, norm_ffn_w, w_router_group,
                  w_router_expert, w_gate, w_up, w_down):
    bsz, s, d = x.shape
    xn = rms_norm(x, norm_mix_w).astype(x.dtype)
    proj = xn @ w_in
    cuts = [int(c) for c in np.cumsum([D_A, D_A, D_QK, D_QK, D_B, D_B, N_DIR * B_HEADS])]
    u_a, v_a, q, k, v_b, z, a_dec, b_beta = jnp.split(proj, cuts, axis=-1)
    y_a = spatial_gating_mixer(u_a, v_a, a_ln_w, a_ln_b, a_spatial_w, a_spatial_b, a_out_norm_w)
    y_b = gdn_mixer(q, k, v_b, z, a_dec, b_beta, conv_w, a_log, dt_bias, gdn_norm_w)
    h = x + jnp.concatenate([y_a, y_b], axis=-1).astype(x.dtype) @ w_out
    hn = rms_norm(h, norm_ffn_w).astype(h.dtype).reshape(bsz * s, d)
    h = h + hierarchical_moe(hn, w_router_group, w_router_expert, w_gate, w_up, w_down).reshape(bsz, s, d)
    return h


def run_trunk(x, layer_params, norm_final_w):
    for layer in range(DEPTH):
        x = encoder_layer(x, *[p[layer] for p in layer_params])
    return rms_norm(x, norm_final_w).astype(x.dtype)


def setup_inputs(seed: int = 0) -> dict:
    key = jax.random.key(seed)
    ks = jax.random.split(key, 24)
    f32 = jnp.float32
    L = DEPTH

    def nrm(k, shape, scale):
        return scale * jax.random.normal(k, shape, f32)

    def gain(k, shape):
        return 1.0 + 0.02 * jax.random.normal(k, shape, f32)

    dt = jnp.exp(jax.random.uniform(ks[10], (L, N_DIR, B_HEADS), f32, math.log(1e-3), math.log(1e-1)))
    return {
        'x_prompt': nrm(ks[0], (BATCH, SEQ, D_MODEL), 1.0),
        'x_sample': nrm(ks[1], (DEC_BATCH, DEC_SEQ, D_MODEL), 1.0),
        'norm_mix_w': gain(ks[2], (L, D_MODEL)),
        'w_in': nrm(ks[3], (L, D_MODEL, D_IN), D_MODEL ** -0.5),
        'a_ln_w': gain(ks[4], (L, D_A)),
        'a_ln_b': nrm(ks[5], (L, D_A), 0.02),
        'a_spatial_w': nrm(ks[6], (L, A_GROUPS, A_CHUNK, A_CHUNK), A_CHUNK ** -0.5),
        'a_spatial_b': gain(ks[7], (L, A_GROUPS, A_CHUNK)),
        'a_out_norm_w': gain(ks[8], (L, D_A)),
        'conv_w': nrm(ks[9], (L, CONV_W, 2 * D_QK + D_B), CONV_W ** -0.5),
        'a_log': jnp.log(jax.random.uniform(ks[11], (L, N_DIR, B_HEADS), f32, 1.0, 16.0)),
        'dt_bias': dt + jnp.log(-jnp.expm1(-dt)),
        'gdn_norm_w': gain(ks[12], (L, B_DV)),
        'w_out': nrm(ks[13], (L, D_MIX, D_MODEL), D_MIX ** -0.5),
        'norm_ffn_w': gain(ks[14], (L, D_MODEL)),
        'w_router_group': nrm(ks[15], (L, D_MODEL, N_EXPERT_GROUPS), D_MODEL ** -0.5),
        'w_router_expert': nrm(ks[16], (L, D_MODEL, N_EXPERTS), D_MODEL ** -0.5),
        'w_gate': nrm(ks[17], (L, N_EXPERTS, D_MODEL, D_EXPERT), D_MODEL ** -0.5),
        'w_up': nrm(ks[18], (L, N_EXPERTS, D_MODEL, D_EXPERT), D_MODEL ** -0.5),
        'w_down': nrm(ks[19], (L, N_EXPERTS, D_EXPERT, D_MODEL), D_EXPERT ** -0.5),
        'norm_final_w': gain(ks[20], (D_MODEL,)),
    }


def reference(x_prompt, x_sample, norm_mix_w, w_in, a_ln_w, a_ln_b, a_spatial_w, a_spatial_b,
              a_out_norm_w, conv_w, a_log, dt_bias, gdn_norm_w, w_out, norm_ffn_w,
              w_router_group, w_router_expert, w_gate, w_up, w_down, norm_final_w):
    layer_params = (norm_mix_w, w_in, a_ln_w, a_ln_b, a_spatial_w, a_spatial_b, a_out_norm_w,
                    conv_w, a_log, dt_bias, gdn_norm_w, w_out, norm_ffn_w, w_router_group,
                    w_router_expert, w_gate, w_up, w_down)
    y_prompt = run_trunk(x_prompt, layer_params, norm_final_w)
    y_sample = run_trunk(x_sample, layer_params, norm_final_w)
    return (y_prompt, y_sample)
```

```python
import functools
import math

import jax
import jax.numpy as jnp
from jax import lax
from jax.experimental import pallas as pl
from jax.experimental.pallas import tpu as pltpu

D_MODEL = 1024
D_A = 512
A_GROUPS = 4
A_DG = 128
A_CHUNK = 128
D_B = 512
B_HEADS = 4
B_DK = 128
B_DV = 128
D_QK = 512
DN_CHUNK = 64
CONV_W = 5
N_DIR = 2
N_EXPERT_GROUPS = 4
EXPERTS_PER_GROUP = 8
N_EXPERTS = 32
TOP_K = 2
D_EXPERT = 512
EPS = 1e-6
D_CONV = 2 * D_QK + D_B
N_GB = 2 * N_DIR * B_HEADS

LANES = 128
VMEM_LIMIT = 48 * 1024 * 1024

BF16 = jnp.bfloat16
F32 = jnp.float32


def _dot(a, b):
    return jnp.dot(a, b, preferred_element_type=F32)


def _dot_nt(a, b):
    return lax.dot_general(a, b, (((1,), (1,)), ((), ())), preferred_element_type=F32)


def _gelu_tanh(x):
    c = math.sqrt(2.0 / math.pi)
    return x * (0.5 * (1.0 + jnp.tanh(c * (x + 0.044715 * (x * x * x)))))


def _sigmoid(x):
    return 1.0 / (1.0 + jnp.exp(-x))


def _silu(x):
    return x * _sigmoid(x)


def _softplus(x):
    return jnp.maximum(x, 0.0) + jnp.log(1.0 + jnp.exp(-jnp.abs(x)))


def _stage1_kernel(x_ref, nw_ref, wuv_ref, wqkv_ref, wz_ref, wgb_ref, wgbt_ref,
                   lnw_ref, lnb_ref, spw_ref, spbt_ref, onw_ref, alog_ref, dtb_ref, alogt_ref, dtbt_ref,
                   ya_ref, qkv_ref, z_ref, gb_ref, gbt_ref):
    tm = x_ref.shape[0]
    x = x_ref[...]
    xn = x * lax.rsqrt(jnp.mean(x * x, axis=-1, keepdims=True) + EPS) * nw_ref[...]
    xb = xn.astype(BF16)

    qkv_ref[...] = _dot(xb, wqkv_ref[...])
    z_ref[...] = _dot(xb, wz_ref[...])

    ab = _dot(xb, wgb_ref[...])[:, :N_GB]
    abt = _dot_nt(wgbt_ref[...], xb)
    na = N_DIR * B_HEADS
    g = -jnp.exp(alog_ref[...]) * _softplus(ab[:, :na] + dtb_ref[...])
    gb_ref[:, :na] = g
    gb_ref[:, na:] = _sigmoid(ab[:, na:])
    gt = -jnp.exp(alogt_ref[...]) * _softplus(abt[:na, :] + dtbt_ref[...])
    gbt_ref[:na, :] = gt
    gbt_ref[na:, :] = _sigmoid(abt[na:, :])

    for grp in range(A_GROUPS):
        cols = slice(grp * A_DG, (grp + 1) * A_DG)
        u_all = _dot(xb, wuv_ref[:, grp * A_DG:(grp + 1) * A_DG])
        v_all = _dot(xb, wuv_ref[:, D_A + grp * A_DG:D_A + (grp + 1) * A_DG])
        spw = spw_ref[grp]
        for c in range(tm // A_CHUNK):
            rows = slice(c * A_CHUNK, (c + 1) * A_CHUNK)
            u = _gelu_tanh(u_all[rows])
            v = _gelu_tanh(v_all[rows])
            mu = jnp.mean(v, axis=-1, keepdims=True)
            vc = v - mu
            var = jnp.mean(vc * vc, axis=-1, keepdims=True)
            vn = vc * lax.rsqrt(var + EPS) * lnw_ref[:, cols] + lnb_ref[:, cols]
            mixed = _dot(spw, vn.astype(BF16)) + spbt_ref[:, grp:grp + 1]
            gated = u * mixed
            out = gated * lax.rsqrt(jnp.mean(gated * gated, axis=-1, keepdims=True) + EPS)
            ya_ref[rows, cols] = (out * onw_ref[:, cols]).astype(BF16)


def _stage1(x, nw, wuv, wqkv, wz, wgb, wgbt, lnw, lnb, spw, spbt, onw, alog, dtb, alogt, dtbt, tm):
    t = x.shape[0]
    full = lambda shape: pl.BlockSpec(shape, lambda i: (0,) * len(shape))
    return pl.pallas_call(
        _stage1_kernel,
        grid=(t // tm,),
        in_specs=[
            pl.BlockSpec((tm, D_MODEL), lambda i: (i, 0)),
            full(nw.shape), full(wuv.shape), full(wqkv.shape), full(wz.shape),
            full(wgb.shape), full(wgbt.shape), full(lnw.shape), full(lnb.shape),
            full(spw.shape), full(spbt.shape), full(onw.shape), full(alog.shape), full(dtb.shape),
            full(alogt.shape), full(dtbt.shape),
        ],
        out_specs=[
            pl.BlockSpec((tm, D_A), lambda i: (i, 0)),
            pl.BlockSpec((tm, D_CONV), lambda i: (i, 0)),
            pl.BlockSpec((tm, D_B), lambda i: (i, 0)),
            pl.BlockSpec((tm, N_GB), lambda i: (i, 0)),
            pl.BlockSpec((N_GB, tm), lambda i: (0, i)),
        ],
        out_shape=[
            jax.ShapeDtypeStruct((t, D_A), BF16),
            jax.ShapeDtypeStruct((t, D_CONV), F32),
            jax.ShapeDtypeStruct((t, D_B), F32),
            jax.ShapeDtypeStruct((t, N_GB), F32),
            jax.ShapeDtypeStruct((N_GB, t), F32),
        ],
        compiler_params=pltpu.CompilerParams(
            dimension_semantics=("arbitrary",), vmem_limit_bytes=VMEM_LIMIT),
        name="stage1_inproj_gmlp",
    )(x, nw, wuv, wqkv, wz, wgb, wgbt, lnw, lnb, spw, spbt, onw, alog, dtb, alogt, dtbt)


def _prep_stage1_weights(norm_mix_w, w_in, a_ln_w, a_ln_b, a_spatial_w, a_spatial_b,
                         a_out_norm_w, a_log, dt_bias):
    c0 = 2 * D_A
    c1 = c0 + D_CONV
    c2 = c1 + D_B
    wuv = w_in[:, :c0].astype(BF16)
    wqkv = w_in[:, c0:c1].astype(BF16)
    wz = w_in[:, c1:c2].astype(BF16)
    wgb_raw = w_in[:, c2:]
    wgb = jnp.pad(wgb_raw, ((0, 0), (0, LANES - N_GB))).astype(BF16)
    wgbt = wgb_raw.T.astype(BF16)
    return (norm_mix_w.reshape(1, D_MODEL), wuv, wqkv, wz, wgb, wgbt,
            a_ln_w.reshape(1, D_A), a_ln_b.reshape(1, D_A), a_spatial_w.astype(BF16),
            a_spatial_b.T, a_out_norm_w.reshape(1, D_A),
            a_log.reshape(1, N_DIR * B_HEADS), dt_bias.reshape(1, N_DIR * B_HEADS),
            a_log.reshape(N_DIR * B_HEADS, 1), dt_bias.reshape(N_DIR * B_HEADS, 1))


PAIR = 2 * DN_CHUNK
HALO = 8


def _split3(x):
    hi = x.astype(BF16)
    r1 = x - hi.astype(F32)
    mid = r1.astype(BF16)
    lo = (r1 - mid.astype(F32)).astype(BF16)
    return hi, mid, lo


def _dot_exact_rhs01(x, m01):
    hi, mid, lo = _split3(x)
    return _dot(hi, m01) + _dot(mid, m01) + _dot(lo, m01)


def _dot_exact_lhs01(m01, x):
    hi, mid, lo = _split3(x)
    return _dot(m01, hi) + _dot(m01, mid) + _dot(m01, lo)


def _unit_tri_inverse(a_neg):
    n = a_neg.shape[0]
    row = lax.broadcasted_iota(jnp.int32, (n, n), 0)
    col = lax.broadcasted_iota(jnp.int32, (n, n), 1)
    r = jnp.where(row == col, 1.0, 0.0) + a_neg
    p16 = a_neg.astype(BF16)
    p = _dot(p16, p16)
    levels = int(math.log2(DN_CHUNK)) - 1
    for lvl in range(levels):
        p16 = p.astype(BF16)
        if lvl < levels - 1:
            both = _dot(jnp.concatenate([r.astype(BF16), p16], axis=0), p16)
            r = r + both[:n]
            p = both[n:]
        else:
            r = r + _dot(r.astype(BF16), p16)
    return r


def _stage2_kernel(n_i, x_ref, xp_ref, xn_ref, cw_ref, gb_ref, gbt_ref,
                   w_ref, qg_ref, u_ref, qk_ref, kgt_ref, gl_ref, xpad_ref, act_ref):
    i = pl.program_id(1)
    tc = x_ref.shape[1]
    top = 6

    xpad_ref[0:HALO, :] = jnp.where(i > 0, xp_ref[0], 0.0)
    xpad_ref[HALO:HALO + tc, :] = x_ref[0]
    xpad_ref[HALO + tc:, :] = jnp.where(i < n_i - 1, xn_ref[0], 0.0)

    for cb in range(D_CONV // LANES):
        cols = slice(cb * LANES, (cb + 1) * LANES)
        for r0 in range(0, tc, PAIR):
            acc = cw_ref[0:1, cols] * xpad_ref[top + r0:top + r0 + PAIR, cols]
            for j in range(1, CONV_W):
                acc = acc + cw_ref[j:j + 1, cols] * xpad_ref[top + j + r0:top + j + r0 + PAIR, cols]
            act_ref[r0:r0 + PAIR, cols] = _silu(acc)

    row = lax.broadcasted_iota(jnp.int32, (PAIR, PAIR), 0)
    col = lax.broadcasted_iota(jnp.int32, (PAIR, PAIR), 1)
    same = (row >= DN_CHUNK) == (col >= DN_CHUNK)
    incl = (same & (row >= col), same & (row <= col))
    strict = (same & (row > col), same & (row < col))
    as01 = lambda m: jnp.where(m, 1.0, 0.0).astype(BF16)
    m_incl = tuple(as01(m) for m in incl)
    m_same = as01(same)
    e_chunk = (as01(row < DN_CHUNK), as01(row >= DN_CHUNK))
    na = N_DIR * B_HEADS

    for p in range(tc // PAIR):
        rows = slice(p * PAIR, (p + 1) * PAIR)
        gbp = gb_ref[0, rows, :]
        gbtp = gbt_ref[:, rows]
        gcol = tuple(_dot_exact_lhs01(m_incl[d], gbp) for d in range(N_DIR))
        grow = tuple(_dot_exact_rhs01(gbtp, m_incl[1 - d]) for d in range(N_DIR))
        tot_row = _dot_exact_rhs01(gbtp, m_same)
        for c in range(2):
            gl_ref[0, 2 * p + c] = jnp.exp(_dot_exact_rhs01(gbtp, e_chunk[c]))

        for h in range(B_HEADS):
            lanes = slice(h * LANES, (h + 1) * LANES)
            q = act_ref[rows, h * B_DK:(h + 1) * B_DK]
            k = act_ref[rows, D_QK + h * B_DK:D_QK + (h + 1) * B_DK]
            v = act_ref[rows, 2 * D_QK + h * B_DV:2 * D_QK + (h + 1) * B_DV]
            qn = q * lax.rsqrt(jnp.sum(q * q, axis=-1, keepdims=True) + EPS) * (B_DK ** -0.5)
            kn = k * lax.rsqrt(jnp.sum(k * k, axis=-1, keepdims=True) + EPS)
            kt = kn.T
            kt16 = kt.astype(BF16)
            kk = _dot(kn.astype(BF16), kt16)
            qk = _dot(qn.astype(BF16), kt16)
            for d in range(N_DIR):
                ci = d * B_HEADS + h
                gc = gcol[d][:, ci:ci + 1]
                gr = grow[d][ci:ci + 1, :]
                beta = gbp[:, na + ci:na + ci + 1]
                decay = jnp.where(incl[d], jnp.exp(gc - gr), 0.0)
                a_neg = jnp.where(strict[d], -(kk * beta * decay), 0.0)
                tinv = _unit_tri_inverse(a_neg)
                eg = jnp.exp(gc)
                rhs = jnp.concatenate([v * beta, kn * (beta * eg)], axis=1).astype(BF16)
                uw = _dot(tinv.astype(BF16), rhs)
                u_ref[d, 0, rows, lanes] = uw[:, :B_DV]
                w_ref[d, 0, rows, lanes] = uw[:, B_DV:].astype(BF16)
                qk_ref[d, 0, rows, lanes] = (qk * decay).astype(BF16)
                qg_ref[d, 0, rows, lanes] = (qn * eg).astype(BF16)
                kgt_ref[d, 0, rows, lanes] = (kt * jnp.exp(tot_row[ci:ci + 1, :] - gr)).astype(BF16)


def _stage2(qkv, conv_w, gb, gbt, tc):
    b, s, _ = qkv.shape
    n_i = s // tc
    hb = tc // HALO
    dirs = lambda shape, dtype: jax.ShapeDtypeStruct((N_DIR, b, s) + shape, dtype)
    out_block = pl.BlockSpec((N_DIR, 1, tc, D_B), lambda bi, i: (0, bi, i, 0))
    return pl.pallas_call(
        functools.partial(_stage2_kernel, n_i),
        grid=(b, n_i),
        in_specs=[
            pl.BlockSpec((1, tc, D_CONV), lambda bi, i: (bi, i, 0)),
            pl.BlockSpec((1, HALO, D_CONV), lambda bi, i: (bi, jnp.maximum(i * hb - 1, 0), 0)),
            pl.BlockSpec((1, HALO, D_CONV),
                         lambda bi, i: (bi, jnp.minimum((i + 1) * hb, s // HALO - 1), 0)),
            pl.BlockSpec((CONV_W, D_CONV), lambda bi, i: (0, 0)),
            pl.BlockSpec((1, tc, N_GB), lambda bi, i: (bi, i, 0)),
            pl.BlockSpec((N_GB, tc), lambda bi, i: (0, bi * n_i + i)),
        ],
        out_specs=[out_block, out_block, out_block, out_block, out_block,
                   pl.BlockSpec((1, tc // DN_CHUNK, N_GB, LANES), lambda bi, i: (bi, i, 0, 0))],
        out_shape=[dirs((D_B,), BF16), dirs((D_B,), BF16), dirs((D_B,), F32),
                   dirs((D_B,), BF16), dirs((D_B,), BF16),
                   jax.ShapeDtypeStruct((b, s // DN_CHUNK, N_GB, LANES), F32)],
        scratch_shapes=[pltpu.VMEM((tc + 2 * HALO, D_CONV), F32), pltpu.VMEM((tc, D_CONV), F32)],
        compiler_params=pltpu.CompilerParams(
            dimension_semantics=("arbitrary", "arbitrary"), vmem_limit_bytes=VMEM_LIMIT),
        name="stage2_gdn_chunk_prep",
    )(qkv, qkv, qkv, conv_w, gb, gbt)


def _stage3_kernel(wf, qgf, uf, qkf, kgf, wb, qgb, ub, qkb, kgb, glf, glb, of_ref, ob_ref, s_ref):
    i = pl.program_id(1)

    @pl.when(i == 0)
    def _():
        s_ref[...] = jnp.zeros_like(s_ref)

    rows_blk = wf.shape[2]
    npairs = rows_blk // PAIR
    zpad = jnp.zeros((DN_CHUNK, B_DV), BF16)
    per_dir = ((wf, qgf, uf, qkf, kgf, glf, of_ref), (wb, qgb, ub, qkb, kgb, glb, ob_ref))
    for step in range(2 * npairs):
        for d in range(N_DIR):
            w_r, qg_r, u_r, qk_r, kg_r, gl_r, o_r = per_dir[d]
            chunk = step if d == 0 else 2 * npairs - 1 - step
            pair, half = chunk // 2, chunk % 2
            r0 = chunk * DN_CHUNK
            rows = slice(r0, r0 + DN_CHUNK)
            prow = slice(pair * PAIR, (pair + 1) * PAIR)
            for h in range(B_HEADS):
                lanes = slice(h * LANES, (h + 1) * LANES)
                s = s_ref[d, h]
                lhs1 = jnp.concatenate([w_r[0, 0, rows, lanes], qg_r[0, 0, rows, lanes]], axis=0)
                m1 = _dot(lhs1, s.astype(BF16))
                v_new = (u_r[0, 0, rows, lanes] - m1[:DN_CHUNK]).astype(BF16)
                v_pad = jnp.concatenate([v_new, zpad] if half == 0 else [zpad, v_new], axis=0)
                lhs2 = jnp.concatenate([qk_r[0, 0, rows, lanes], kg_r[0, 0, prow, lanes]], axis=0)
                m2 = _dot(lhs2, v_pad)
                o_r[0, rows, lanes] = m1[DN_CHUNK:] + m2[:DN_CHUNK]
                ci = d * B_HEADS + h
                s_ref[d, h] = s * gl_r[0, chunk, ci:ci + 1, :] + m2[DN_CHUNK:]


def _stage3(w, qg, u, qk, kgt, gl, rows_blk):
    _, b, s, _ = w.shape
    n_i = s // rows_blk
    cpb = rows_blk // DN_CHUNK
    fwd = pl.BlockSpec((1, 1, rows_blk, D_B), lambda bi, i: (0, bi, i, 0))
    bwd = pl.BlockSpec((1, 1, rows_blk, D_B), lambda bi, i: (1, bi, n_i - 1 - i, 0))
    return pl.pallas_call(
        _stage3_kernel,
        grid=(b, n_i),
        in_specs=[fwd] * 5 + [bwd] * 5 + [
            pl.BlockSpec((1, cpb, N_GB, LANES), lambda bi, i: (bi, i, 0, 0)),
            pl.BlockSpec((1, cpb, N_GB, LANES), lambda bi, i: (bi, n_i - 1 - i, 0, 0)),
        ],
        out_specs=[pl.BlockSpec((1, rows_blk, D_B), lambda bi, i: (bi, i, 0)),
                   pl.BlockSpec((1, rows_blk, D_B), lambda bi, i: (bi, n_i - 1 - i, 0))],
        out_shape=[jax.ShapeDtypeStruct((b, s, D_B), F32), jax.ShapeDtypeStruct((b, s, D_B), F32)],
        scratch_shapes=[pltpu.VMEM((N_DIR, B_HEADS, B_DK, B_DV), F32)],
        compiler_params=pltpu.CompilerParams(
            dimension_semantics=("arbitrary", "arbitrary"), vmem_limit_bytes=VMEM_LIMIT),
        name="stage3_gdn_scan",
    )(w, qg, u, qk, kgt, w, qg, u, qk, kgt, gl, gl)


N_ROUTE = N_EXPERT_GROUPS + N_EXPERTS


def _stage4_kernel(x_ref, ya_ref, of_ref, ob_ref, z_ref, gnw_ref, woa_ref, wob_ref, fnw_ref,
                   wrh_ref, wrl_ref, h_ref, hn_ref, ids_ref, wts_ref):
    tm = x_ref.shape[0]
    o = of_ref[...] + ob_ref[...]
    z = z_ref[...]
    parts = []
    for hd in range(B_HEADS):
        lanes = slice(hd * B_DV, (hd + 1) * B_DV)
        oh = o[:, lanes]
        yh = oh * lax.rsqrt(jnp.mean(oh * oh, axis=-1, keepdims=True) + EPS) * gnw_ref[...]
        parts.append((yh * _silu(z[:, lanes])).astype(BF16))
    yb = jnp.concatenate(parts, axis=1)
    h = x_ref[...] + (_dot(ya_ref[...], woa_ref[...]) + _dot(yb, wob_ref[...]))
    h_ref[...] = h
    hn = h * lax.rsqrt(jnp.mean(h * h, axis=-1, keepdims=True) + EPS) * fnw_ref[...]
    hn_ref[...] = hn

    hi = hn.astype(BF16)
    lo = (hn - hi.astype(F32)).astype(BF16)
    logits = _dot(hi, wrh_ref[...]) + (_dot(lo, wrh_ref[...]) + _dot(hi, wrl_ref[...]))

    lane = lax.broadcasted_iota(jnp.int32, (tm, LANES), 1)
    neg = -jnp.inf
    is_g = lane < N_EXPERT_GROUPS
    gl = jnp.where(is_g, logits, neg)
    gmax = jnp.max(gl, axis=-1, keepdims=True)
    gidx = jnp.min(jnp.where(gl == gmax, lane, LANES), axis=-1, keepdims=True)
    g_w = 1.0 / jnp.sum(jnp.where(is_g, jnp.exp(gl - gmax), 0.0), axis=-1, keepdims=True)

    e0 = N_EXPERT_GROUPS + gidx * EXPERTS_PER_GROUP
    in_grp = (lane >= e0) & (lane < e0 + EXPERTS_PER_GROUP)
    el = jnp.where(in_grp, logits, neg)
    m1 = jnp.max(el, axis=-1, keepdims=True)
    i1 = jnp.min(jnp.where(el == m1, lane, LANES), axis=-1, keepdims=True)
    el2 = jnp.where(lane == i1, neg, el)
    m2 = jnp.max(el2, axis=-1, keepdims=True)
    i2 = jnp.min(jnp.where(el2 == m2, lane, LANES), axis=-1, keepdims=True)
    e2 = jnp.exp(m2 - m1)
    inv = 1.0 / (1.0 + e2)
    ids_ref[:, 0:1] = i1 - N_EXPERT_GROUPS
    ids_ref[:, 1:2] = i2 - N_EXPERT_GROUPS
    wts_ref[:, 0:1] = g_w * inv
    wts_ref[:, 1:2] = g_w * (e2 * inv)


def _stage4(x, ya, o_f, o_b, z, gnw, woa, wob, fnw, wrh, wrl, tm):
    t = x.shape[0]
    full = lambda a: pl.BlockSpec(a.shape, lambda i: (0,) * a.ndim)
    tile = lambda n: pl.BlockSpec((tm, n), lambda i: (i, 0))
    return pl.pallas_call(
        _stage4_kernel,
        grid=(t // tm,),
        in_specs=[tile(D_MODEL), tile(D_A), tile(D_B), tile(D_B), tile(D_B),
                  full(gnw), full(woa), full(wob), full(fnw), full(wrh), full(wrl)],
        out_specs=[tile(D_MODEL), tile(D_MODEL), tile(TOP_K), tile(TOP_K)],
        out_shape=[jax.ShapeDtypeStruct((t, D_MODEL), F32), jax.ShapeDtypeStruct((t, D_MODEL), F32),
                   jax.ShapeDtypeStruct((t, TOP_K), jnp.int32),
                   jax.ShapeDtypeStruct((t, TOP_K), F32)],
        compiler_params=pltpu.CompilerParams(
            dimension_semantics=("arbitrary",), vmem_limit_bytes=VMEM_LIMIT),
        name="stage4_outproj_router",
    )(x, ya, o_f, o_b, z, gnw, woa, wob, fnw, wrh, wrl)


def _rank_kernel(ids_ref, rank_ref, cnt_ref, carry_ref):
    tm = ids_ref.shape[0]

    @pl.when(pl.program_id(0) == 0)
    def _():
        carry_ref[...] = jnp.zeros_like(carry_ref)

    ids = ids_ref[...]
    lane = lax.broadcasted_iota(jnp.int32, (tm, LANES), 1)
    oh0 = jnp.where(lane == ids[:, 0:1], 1.0, 0.0)
    oh1 = jnp.where(lane == ids[:, 1:2], 1.0, 0.0)
    oh = oh0 + oh1
    row = lax.broadcasted_iota(jnp.int32, (tm, tm), 0)
    col = lax.broadcasted_iota(jnp.int32, (tm, tm), 1)
    earlier = jnp.where(row > col, 1.0, 0.0).astype(BF16)
    before = _dot(earlier, oh.astype(BF16)) + carry_ref[...]
    rank_ref[:, 0:1] = jnp.sum(before * oh0, axis=-1, keepdims=True).astype(jnp.int32)
    rank_ref[:, 1:2] = jnp.sum(before * oh1, axis=-1, keepdims=True).astype(jnp.int32)
    carry_ref[...] = carry_ref[...] + jnp.sum(oh, axis=0, keepdims=True)
    cnt_ref[...] = carry_ref[...]


def _expert_ranks(ids, tm):
    t = ids.shape[0]
    return pl.pallas_call(
        _rank_kernel,
        grid=(t // tm,),
        in_specs=[pl.BlockSpec((tm, TOP_K), lambda i: (i, 0))],
        out_specs=[pl.BlockSpec((tm, TOP_K), lambda i: (i, 0)),
                   pl.BlockSpec((1, LANES), lambda i: (0, 0))],
        out_shape=[jax.ShapeDtypeStruct((t, TOP_K), jnp.int32),
                   jax.ShapeDtypeStruct((1, LANES), F32)],
        scratch_shapes=[pltpu.VMEM((1, LANES), F32)],
        compiler_params=pltpu.CompilerParams(dimension_semantics=("arbitrary",)),
        name="stage5a_expert_ranks",
    )(ids)


MOE_BM = 256


def _expert_kernel(be_ref, nu_ref, tokc_ref, tokn_ref, hn_hbm, wg_ref, wu_ref, wd_ref,
                   y_ref, xbuf, sem):
    b = pl.program_id(0)
    n_used = nu_ref[0]
    slot = lax.rem(b, 2)
    bm = xbuf.shape[1]

    def row_copy(tok_ref, r, sl):
        return pltpu.make_async_copy(hn_hbm.at[pl.ds(tok_ref[0, 0, r], 1)],
                                     xbuf.at[sl, pl.ds(r, 1)], sem.at[sl])

    def start_rows(tok_ref, sl):
        def body(r, c):
            row_copy(tok_ref, r, sl).start()
            return c
        lax.fori_loop(0, bm, body, 0, unroll=8)

    def wait_rows(tok_ref, sl):
        def body(r, c):
            row_copy(tok_ref, r, sl).wait()
            return c
        lax.fori_loop(0, bm, body, 0, unroll=8)

    @pl.when(b == 0)
    def _():
        start_rows(tokc_ref, 0)

    @pl.when(b + 1 < n_used)
    def _():
        start_rows(tokn_ref, 1 - slot)

    @pl.when(b < n_used)
    def _():
        wait_rows(tokc_ref, slot)
        x = xbuf[slot].astype(BF16)
        g = _dot(x, wg_ref[0])
        u = _dot(x, wu_ref[0])
        y_ref[...] = _dot((_silu(g) * u).astype(BF16), wd_ref[0])

    @pl.when(b >= n_used)
    def _():
        y_ref[...] = jnp.zeros_like(y_ref)


def _experts(block_expert, n_used, slot_tok, hn, wg, wu, wd):
    n_blocks = block_expert.shape[0]
    bm = MOE_BM
    tok3 = slot_tok.reshape(n_blocks, 1, bm)
    grid_spec = pltpu.PrefetchScalarGridSpec(
        num_scalar_prefetch=2,
        grid=(n_blocks,),
        in_specs=[
            pl.BlockSpec((1, 1, bm), lambda b, be, nu: (b, 0, 0), memory_space=pltpu.SMEM),
            pl.BlockSpec((1, 1, bm), lambda b, be, nu: (jnp.minimum(b + 1, n_blocks - 1), 0, 0),
                         memory_space=pltpu.SMEM),
            pl.BlockSpec(memory_space=pl.ANY),
            pl.BlockSpec((1, D_MODEL, D_EXPERT), lambda b, be, nu: (be[b], 0, 0)),
            pl.BlockSpec((1, D_MODEL, D_EXPERT), lambda b, be, nu: (be[b], 0, 0)),
            pl.BlockSpec((1, D_EXPERT, D_MODEL), lambda b, be, nu: (be[b], 0, 0)),
        ],
        out_specs=pl.BlockSpec((bm, D_MODEL), lambda b, be, nu: (b, 0)),
        scratch_shapes=[pltpu.VMEM((2, bm, D_MODEL), F32), pltpu.SemaphoreType.DMA((2,))],
    )
    return pl.pallas_call(
        _expert_kernel,
        grid_spec=grid_spec,
        out_shape=jax.ShapeDtypeStruct((n_blocks * bm, D_MODEL), F32),
        compiler_params=pltpu.CompilerParams(
            dimension_semantics=("arbitrary",), vmem_limit_bytes=VMEM_LIMIT),
        name="stage5b_experts",
    )(block_expert, n_used, tok3, tok3, hn, wg, wu, wd)


def _combine_kernel(n, dc_ref, dn_ref, h_ref, wts_ref, fw_ref, y_hbm, out_ref, ybuf, sem):
    i = pl.program_id(0)
    slot = lax.rem(i, 2)
    tm = h_ref.shape[0]
    nrows = TOP_K * tm

    def row_copy(d_ref, r, sl):
        return pltpu.make_async_copy(y_hbm.at[pl.ds(d_ref[0, 0, r], 1)],
                                     ybuf.at[sl, pl.ds(r, 1)], sem.at[sl])

    def start_rows(d_ref, sl):
        def body(r, c):
            row_copy(d_ref, r, sl).start()
            return c
        lax.fori_loop(0, nrows, body, 0, unroll=8)

    def wait_rows(d_ref, sl):
        def body(r, c):
            row_copy(d_ref, r, sl).wait()
            return c
        lax.fori_loop(0, nrows, body, 0, unroll=8)

    @pl.when(i == 0)
    def _():
        start_rows(dc_ref, 0)

    @pl.when(i + 1 < n)
    def _():
        start_rows(dn_ref, 1 - slot)

    wait_rows(dc_ref, slot)
    w = wts_ref[...]
    moe = w[:, 0:1] * ybuf[slot, 0:tm, :] + w[:, 1:2] * ybuf[slot, tm:nrows, :]
    h = h_ref[...] + moe
    out_ref[...] = h * lax.rsqrt(jnp.mean(h * h, axis=-1, keepdims=True) + EPS) * fw_ref[...]


def _combine(dest_tiles, h, wts, fw, y_slots, tm):
    t = h.shape[0]
    n = t // tm
    return pl.pallas_call(
        functools.partial(_combine_kernel, n),
        grid=(n,),
        in_specs=[
            pl.BlockSpec((1, 1, TOP_K * tm), lambda i: (i, 0, 0), memory_space=pltpu.SMEM),
            pl.BlockSpec((1, 1, TOP_K * tm), lambda i: (jnp.minimum(i + 1, n - 1), 0, 0),
                         memory_space=pltpu.SMEM),
            pl.BlockSpec((tm, D_MODEL), lambda i: (i, 0)),
            pl.BlockSpec((tm, TOP_K), lambda i: (i, 0)),
            pl.BlockSpec((1, D_MODEL), lambda i: (0, 0)),
            pl.BlockSpec(memory_space=pl.ANY),
        ],
        out_specs=pl.BlockSpec((tm, D_MODEL), lambda i: (i, 0)),
        out_shape=jax.ShapeDtypeStruct((t, D_MODEL), F32),
        scratch_shapes=[pltpu.VMEM((2, TOP_K * tm, D_MODEL), F32), pltpu.SemaphoreType.DMA((2,))],
        compiler_params=pltpu.CompilerParams(
            dimension_semantics=("arbitrary",), vmem_limit_bytes=VMEM_LIMIT),
        name="stage6_combine_norm",
    )(dest_tiles, dest_tiles, h, wts, fw, y_slots)


def _moe_plan(ids, rank, counts):
    bm = MOE_BM
    n_assign = ids.shape[0] * TOP_K
    n_blocks = -(-n_assign // bm) + N_EXPERTS
    cnt = counts[0, :N_EXPERTS].astype(jnp.int32)
    padded = (cnt + bm - 1) // bm * bm
    pad_end = jnp.cumsum(padded)
    pad_start = pad_end - padded
    dest = pad_start[ids] + rank
    n_used = (pad_end[-1] // bm).astype(jnp.int32)
    blk = jnp.arange(n_blocks, dtype=jnp.int32)
    block_expert = jnp.searchsorted(pad_end, jnp.minimum(blk, n_used - 1) * bm, side='right')
    block_expert = jnp.minimum(block_expert, N_EXPERTS - 1).astype(jnp.int32)
    flat_tok = jnp.arange(n_assign, dtype=jnp.int32) // TOP_K
    slot_tok = jnp.zeros((n_blocks * bm,), jnp.int32).at[dest.reshape(-1)].set(flat_tok)
    return dest, slot_tok, block_expert, n_used.reshape(1)


def kernel(x_prompt, x_sample, norm_mix_w, w_in, a_ln_w, a_ln_b, a_spatial_w, a_spatial_b, a_out_norm_w, conv_w, a_log, dt_bias, gdn_norm_w, w_out, norm_ffn_w, w_router_group, w_router_expert, w_gate, w_up, w_down, norm_final_w):
    s1w = _prep_stage1_weights(norm_mix_w[0], w_in[0], a_ln_w[0], a_ln_b[0], a_spatial_w[0],
                               a_spatial_b[0], a_out_norm_w[0], a_log[0], dt_bias[0])
    woa = w_out[0, :D_A].astype(BF16)
    wob = w_out[0, D_A:].astype(BF16)
    w_r = jnp.concatenate([w_router_group[0], w_router_expert[0]], axis=1)
    w_r = jnp.pad(w_r, ((0, 0), (0, LANES - N_ROUTE)))
    wrh = w_r.astype(BF16)
    wrl = (w_r - wrh.astype(F32)).astype(BF16)
    gnw = gdn_norm_w[0].reshape(1, B_DV)
    fnw = norm_ffn_w[0].reshape(1, D_MODEL)

    tm = 256
    per_run = []
    for x in (x_prompt, x_sample):
        b, s, d = x.shape
        x2 = x.reshape(b * s, d)
        ya, qkv, z, gb, gbt = _stage1(x2, *s1w, tm=tm)
        w, qg, u, qk, kgt, gl = _stage2(qkv.reshape(b, s, D_CONV), conv_w[0],
                                        gb.reshape(b, s, N_GB), gbt, tc=256)
        o_f, o_b = _stage3(w, qg, u, qk, kgt, gl, rows_blk=512)
        h, hn, ids, wts = _stage4(x2, ya, o_f.reshape(b * s, D_B), o_b.reshape(b * s, D_B), z,
                                  gnw, woa, wob, fnw, wrh, wrl, tm=tm)
        per_run.append((x.shape, h, hn, ids, wts))

    hn_all = jnp.concatenate([r[2] for r in per_run], axis=0)
    ids_all = jnp.concatenate([r[3] for r in per_run], axis=0)
    rank, counts = _expert_ranks(ids_all, tm=512)
    dest, slot_tok, block_expert, n_used = _moe_plan(ids_all, rank, counts)
    y_slots = _experts(block_expert, n_used, slot_tok, hn_all,
                       w_gate[0].astype(BF16), w_up[0].astype(BF16), w_down[0].astype(BF16))

    outs = []
    t0 = 0
    fw = norm_final_w.reshape(1, D_MODEL)
    for shape, h, _, _, wts in per_run:
        t = h.shape[0]
        d_run = dest[t0:t0 + t].reshape(t // tm, tm, TOP_K).transpose(0, 2, 1)
        d_run = d_run.reshape(t // tm, 1, TOP_K * tm)
        outs.append(_combine(d_run, h, wts, fw, y_slots, tm=tm).reshape(shape))
        t0 += t
    return tuple(outs)
```

```python
import functools
import math

import jax
import jax.numpy as jnp
from jax import lax
from jax.experimental import pallas as pl
from jax.experimental.pallas import tpu as pltpu

D_MODEL = 1024
D_A = 512
A_GROUPS = 4
A_DG = 128
A_CHUNK = 128
D_B = 512
B_HEADS = 4
B_DK = 128
B_DV = 128
D_QK = 512
DN_CHUNK = 64
CONV_W = 5
N_DIR = 2
N_EXPERT_GROUPS = 4
EXPERTS_PER_GROUP = 8
N_EXPERTS = 32
TOP_K = 2
D_EXPERT = 512
EPS = 1e-6
D_CONV = 2 * D_QK + D_B
N_GB = 2 * N_DIR * B_HEADS

LANES = 128
VMEM_LIMIT = 48 * 1024 * 1024

BF16 = jnp.bfloat16
F32 = jnp.float32


def _dot(a, b):
    return jnp.dot(a, b, preferred_element_type=F32)


def _dot_nt(a, b):
    return lax.dot_general(a, b, (((1,), (1,)), ((), ())), preferred_element_type=F32)


def _gelu_tanh(x):
    c = math.sqrt(2.0 / math.pi)
    return x * (0.5 * (1.0 + jnp.tanh(c * (x + 0.044715 * (x * x * x)))))


def _sigmoid(x):
    return 1.0 / (1.0 + jnp.exp(-x))


def _silu(x):
    return x * _sigmoid(x)


def _softplus(x):
    return jnp.maximum(x, 0.0) + jnp.log(1.0 + jnp.exp(-jnp.abs(x)))


def _stage1_kernel(x_ref, nw_ref, wuv_ref, wqkv_ref, wz_ref, wgb_ref, wgbt_ref,
                   lnw_ref, lnb_ref, spw_ref, spbt_ref, onw_ref, alog_ref, dtb_ref, alogt_ref, dtbt_ref,
                   ya_ref, qkv_ref, z_ref, gb_ref, gbt_ref):
    tm = x_ref.shape[0]
    x = x_ref[...]
    xn = x * lax.rsqrt(jnp.mean(x * x, axis=-1, keepdims=True) + EPS) * nw_ref[...]
    xb = xn.astype(BF16)

    qkv_ref[...] = _dot(xb, wqkv_ref[...])
    z_ref[...] = _dot(xb, wz_ref[...])

    ab = _dot(xb, wgb_ref[...])[:, :N_GB]
    abt = _dot_nt(wgbt_ref[...], xb)
    na = N_DIR * B_HEADS
    g = -jnp.exp(alog_ref[...]) * _softplus(ab[:, :na] + dtb_ref[...])
    gb_ref[:, :na] = g
    gb_ref[:, na:] = _sigmoid(ab[:, na:])
    gt = -jnp.exp(alogt_ref[...]) * _softplus(abt[:na, :] + dtbt_ref[...])
    gbt_ref[:na, :] = gt
    gbt_ref[na:, :] = _sigmoid(abt[na:, :])

    for grp in range(A_GROUPS):
        cols = slice(grp * A_DG, (grp + 1) * A_DG)
        uv_all = _dot(xb, wuv_ref[:, 2 * grp * A_DG:2 * (grp + 1) * A_DG])
        u_all = uv_all[:, :A_DG]
        v_all = uv_all[:, A_DG:]
        spw = spw_ref[grp]
        for c in range(tm // A_CHUNK):
            rows = slice(c * A_CHUNK, (c + 1) * A_CHUNK)
            u = _gelu_tanh(u_all[rows])
            v = _gelu_tanh(v_all[rows])
            mu = jnp.mean(v, axis=-1, keepdims=True)
            vc = v - mu
            var = jnp.mean(vc * vc, axis=-1, keepdims=True)
            vn = vc * lax.rsqrt(var + EPS) * lnw_ref[:, cols] + lnb_ref[:, cols]
            mixed = _dot(spw, vn.astype(BF16)) + spbt_ref[:, grp:grp + 1]
            gated = u * mixed
            out = gated * lax.rsqrt(jnp.mean(gated * gated, axis=-1, keepdims=True) + EPS)
            ya_ref[rows, cols] = (out * onw_ref[:, cols]).astype(BF16)


def _stage1(x, nw, wuv, wqkv, wz, wgb, wgbt, lnw, lnb, spw, spbt, onw, alog, dtb, alogt, dtbt, tm):
    t = x.shape[0]
    full = lambda shape: pl.BlockSpec(shape, lambda i: (0,) * len(shape))
    return pl.pallas_call(
        _stage1_kernel,
        grid=(t // tm,),
        in_specs=[
            pl.BlockSpec((tm, D_MODEL), lambda i: (i, 0)),
            full(nw.shape), full(wuv.shape), full(wqkv.shape), full(wz.shape),
            full(wgb.shape), full(wgbt.shape), full(lnw.shape), full(lnb.shape),
            full(spw.shape), full(spbt.shape), full(onw.shape), full(alog.shape), full(dtb.shape),
            full(alogt.shape), full(dtbt.shape),
        ],
        out_specs=[
            pl.BlockSpec((tm, D_A), lambda i: (i, 0)),
            pl.BlockSpec((tm, D_CONV), lambda i: (i, 0)),
            pl.BlockSpec((tm, D_B), lambda i: (i, 0)),
            pl.BlockSpec((tm, N_GB), lambda i: (i, 0)),
            pl.BlockSpec((N_GB, tm), lambda i: (0, i)),
        ],
        out_shape=[
            jax.ShapeDtypeStruct((t, D_A), BF16),
            jax.ShapeDtypeStruct((t, D_CONV), F32),
            jax.ShapeDtypeStruct((t, D_B), F32),
            jax.ShapeDtypeStruct((t, N_GB), F32),
            jax.ShapeDtypeStruct((N_GB, t), F32),
        ],
        compiler_params=pltpu.CompilerParams(
            dimension_semantics=("arbitrary",), vmem_limit_bytes=VMEM_LIMIT),
        name="stage1_inproj_gmlp",
    )(x, nw, wuv, wqkv, wz, wgb, wgbt, lnw, lnb, spw, spbt, onw, alog, dtb, alogt, dtbt)


def _prep_stage1_weights(norm_mix_w, w_in, a_ln_w, a_ln_b, a_spatial_w, a_spatial_b,
                         a_out_norm_w, a_log, dt_bias):
    c0 = 2 * D_A
    c1 = c0 + D_CONV
    c2 = c1 + D_B
    wuv = w_in[:, :c0].reshape(D_MODEL, 2, A_GROUPS, A_DG).transpose(0, 2, 1, 3)
    wuv = wuv.reshape(D_MODEL, c0).astype(BF16)
    wqkv = w_in[:, c0:c1].astype(BF16)
    wz = w_in[:, c1:c2].astype(BF16)
    wgb_raw = w_in[:, c2:]
    wgb = jnp.pad(wgb_raw, ((0, 0), (0, LANES - N_GB))).astype(BF16)
    wgbt = wgb_raw.T.astype(BF16)
    return (norm_mix_w.reshape(1, D_MODEL), wuv, wqkv, wz, wgb, wgbt,
            a_ln_w.reshape(1, D_A), a_ln_b.reshape(1, D_A), a_spatial_w.astype(BF16),
            a_spatial_b.T, a_out_norm_w.reshape(1, D_A),
            a_log.reshape(1, N_DIR * B_HEADS), dt_bias.reshape(1, N_DIR * B_HEADS),
            a_log.reshape(N_DIR * B_HEADS, 1), dt_bias.reshape(N_DIR * B_HEADS, 1))


PAIR = 2 * DN_CHUNK
HALO = 8


def _split3(x):
    hi = x.astype(BF16)
    r1 = x - hi.astype(F32)
    mid = r1.astype(BF16)
    lo = (r1 - mid.astype(F32)).astype(BF16)
    return hi, mid, lo


def _dot_exact_rhs01(x, m01):
    hi, mid, lo = _split3(x)
    return _dot(hi, m01) + _dot(mid, m01) + _dot(lo, m01)


def _dot_exact_lhs01(m01, x):
    hi, mid, lo = _split3(x)
    return _dot(m01, hi) + _dot(m01, mid) + _dot(m01, lo)


def _unit_tri_inverses(a_negs):
    n = a_negs[0].shape[0]
    row = lax.broadcasted_iota(jnp.int32, (n, n), 0)
    col = lax.broadcasted_iota(jnp.int32, (n, n), 1)
    eye = jnp.where(row == col, 1.0, 0.0)
    r = [eye + a for a in a_negs]
    p16 = [a.astype(BF16) for a in a_negs]
    pw = [_dot(x, x) for x in p16]
    levels = int(math.log2(DN_CHUNK)) - 1
    for lvl in range(levels):
        p16 = [x.astype(BF16) for x in pw]
        if lvl < levels - 1:
            both = [_dot(jnp.concatenate([ri.astype(BF16), pi], axis=0), pi)
                    for ri, pi in zip(r, p16)]
            r = [ri + bi[:n] for ri, bi in zip(r, both)]
            pw = [bi[n:] for bi in both]
        else:
            r = [ri + _dot(ri.astype(BF16), pi) for ri, pi in zip(r, p16)]
    return r


def _stage2_kernel(n_i, x_ref, xp_ref, xn_ref, cw_ref, gb_ref, gbt_ref,
                   w_ref, qg_ref, u_ref, qk_ref, kgt_ref, gl_ref, xpad_ref, act_ref):
    i = pl.program_id(1)
    tc = x_ref.shape[1]
    top = 6

    xpad_ref[0:HALO, :] = jnp.where(i > 0, xp_ref[0], 0.0)
    xpad_ref[HALO:HALO + tc, :] = x_ref[0]
    xpad_ref[HALO + tc:, :] = jnp.where(i < n_i - 1, xn_ref[0], 0.0)

    for cb in range(D_CONV // LANES):
        cols = slice(cb * LANES, (cb + 1) * LANES)
        for r0 in range(0, tc, PAIR):
            acc = cw_ref[0:1, cols] * xpad_ref[top + r0:top + r0 + PAIR, cols]
            for j in range(1, CONV_W):
                acc = acc + cw_ref[j:j + 1, cols] * xpad_ref[top + j + r0:top + j + r0 + PAIR, cols]
            act_ref[r0:r0 + PAIR, cols] = _silu(acc)

    row = lax.broadcasted_iota(jnp.int32, (PAIR, PAIR), 0)
    col = lax.broadcasted_iota(jnp.int32, (PAIR, PAIR), 1)
    same = (row >= DN_CHUNK) == (col >= DN_CHUNK)
    incl = (same & (row >= col), same & (row <= col))
    strict = (same & (row > col), same & (row < col))
    as01 = lambda m: jnp.where(m, 1.0, 0.0).astype(BF16)
    m_incl = tuple(as01(m) for m in incl)
    m_same = as01(same)
    e_chunk = (as01(row < DN_CHUNK), as01(row >= DN_CHUNK))
    na = N_DIR * B_HEADS

    for p in range(tc // PAIR):
        rows = slice(p * PAIR, (p + 1) * PAIR)
        gbp = gb_ref[0, rows, :]
        gbtp = gbt_ref[:, rows]
        gcol = tuple(_dot_exact_lhs01(m_incl[d], gbp) for d in range(N_DIR))
        grow = tuple(_dot_exact_rhs01(gbtp, m_incl[1 - d]) for d in range(N_DIR))
        tot_row = _dot_exact_rhs01(gbtp, m_same)
        for c in range(2):
            gl_ref[0, 2 * p + c] = jnp.exp(_dot_exact_rhs01(gbtp, e_chunk[c]))

        heads = []
        for h in range(B_HEADS):
            q = act_ref[rows, h * B_DK:(h + 1) * B_DK]
            k = act_ref[rows, D_QK + h * B_DK:D_QK + (h + 1) * B_DK]
            qn = q * lax.rsqrt(jnp.sum(q * q, axis=-1, keepdims=True) + EPS) * (B_DK ** -0.5)
            kn = k * lax.rsqrt(jnp.sum(k * k, axis=-1, keepdims=True) + EPS)
            kt = kn.T
            heads.append((qn, kn, kt, kt.astype(BF16)))
        kks = [_dot(kn.astype(BF16), kt16) for _, kn, _, kt16 in heads]
        qks = [_dot(qn.astype(BF16), kt16) for qn, _, _, kt16 in heads]

        chains = [(d, h) for d in range(N_DIR) for h in range(B_HEADS)]
        a_negs, rhss = [], []
        for d, h in chains:
            ci = d * B_HEADS + h
            lanes = slice(h * LANES, (h + 1) * LANES)
            qn, kn, kt, _ = heads[h]
            v = act_ref[rows, 2 * D_QK + h * B_DV:2 * D_QK + (h + 1) * B_DV]
            gc = gcol[d][:, ci:ci + 1]
            gr = grow[d][ci:ci + 1, :]
            beta = gbp[:, na + ci:na + ci + 1]
            decay = jnp.where(incl[d], jnp.exp(gc - gr), 0.0)
            a_negs.append(jnp.where(strict[d], -(kks[h] * beta * decay), 0.0))
            eg = jnp.exp(gc)
            rhss.append(jnp.concatenate([v * beta, kn * (beta * eg)], axis=1).astype(BF16))
            qk_ref[d, 0, rows, lanes] = (qks[h] * decay).astype(BF16)
            qg_ref[d, 0, rows, lanes] = (qn * eg).astype(BF16)
            kgt_ref[d, 0, rows, lanes] = (kt * jnp.exp(tot_row[ci:ci + 1, :] - gr)).astype(BF16)

        tinvs = _unit_tri_inverses(a_negs)
        uws = [_dot(t.astype(BF16), rhs) for t, rhs in zip(tinvs, rhss)]
        for (d, h), uw in zip(chains, uws):
            lanes = slice(h * LANES, (h + 1) * LANES)
            u_ref[d, 0, rows, lanes] = uw[:, :B_DV]
            w_ref[d, 0, rows, lanes] = uw[:, B_DV:].astype(BF16)


def _stage2(qkv, conv_w, gb, gbt, tc):
    b, s, _ = qkv.shape
    n_i = s // tc
    hb = tc // HALO
    dirs = lambda shape, dtype: jax.ShapeDtypeStruct((N_DIR, b, s) + shape, dtype)
    out_block = pl.BlockSpec((N_DIR, 1, tc, D_B), lambda bi, i: (0, bi, i, 0))
    return pl.pallas_call(
        functools.partial(_stage2_kernel, n_i),
        grid=(b, n_i),
        in_specs=[
            pl.BlockSpec((1, tc, D_CONV), lambda bi, i: (bi, i, 0)),
            pl.BlockSpec((1, HALO, D_CONV), lambda bi, i: (bi, jnp.maximum(i * hb - 1, 0), 0)),
            pl.BlockSpec((1, HALO, D_CONV),
                         lambda bi, i: (bi, jnp.minimum((i + 1) * hb, s // HALO - 1), 0)),
            pl.BlockSpec((CONV_W, D_CONV), lambda bi, i: (0, 0)),
            pl.BlockSpec((1, tc, N_GB), lambda bi, i: (bi, i, 0)),
            pl.BlockSpec((N_GB, tc), lambda bi, i: (0, bi * n_i + i)),
        ],
        out_specs=[out_block, out_block, out_block, out_block, out_block,
                   pl.BlockSpec((1, tc // DN_CHUNK, N_GB, LANES), lambda bi, i: (bi, i, 0, 0))],
        out_shape=[dirs((D_B,), BF16), dirs((D_B,), BF16), dirs((D_B,), F32),
                   dirs((D_B,), BF16), dirs((D_B,), BF16),
                   jax.ShapeDtypeStruct((b, s // DN_CHUNK, N_GB, LANES), F32)],
        scratch_shapes=[pltpu.VMEM((tc + 2 * HALO, D_CONV), F32), pltpu.VMEM((tc, D_CONV), F32)],
        compiler_params=pltpu.CompilerParams(
            dimension_semantics=("arbitrary", "arbitrary"), vmem_limit_bytes=VMEM_LIMIT),
        name="stage2_gdn_chunk_prep",
    )(qkv, qkv, qkv, conv_w, gb, gbt)


def _stage3_kernel(wf, qgf, uf, qkf, kgf, wb, qgb, ub, qkb, kgb, glf, glb, of_ref, ob_ref, s_ref):
    i = pl.program_id(1)

    @pl.when(i == 0)
    def _():
        s_ref[...] = jnp.zeros_like(s_ref)

    rows_blk = wf.shape[2]
    npairs = rows_blk // PAIR
    zpad = jnp.zeros((DN_CHUNK, B_DV), BF16)
    per_dir = ((wf, qgf, uf, qkf, kgf, glf, of_ref), (wb, qgb, ub, qkb, kgb, glb, ob_ref))
    chains = [(d, h) for d in range(N_DIR) for h in range(B_HEADS)]
    for step in range(2 * npairs):
        def where(d):
            chunk = step if d == 0 else 2 * npairs - 1 - step
            pair = chunk // 2
            return (chunk, slice(chunk * DN_CHUNK, (chunk + 1) * DN_CHUNK),
                    slice(pair * PAIR, (pair + 1) * PAIR))

        states, m1s, m2s = [], [], []
        for d, h in chains:
            w_r, qg_r = per_dir[d][0], per_dir[d][1]
            _, rows, _ = where(d)
            lanes = slice(h * LANES, (h + 1) * LANES)
            s = s_ref[d, h]
            states.append(s)
            lhs1 = jnp.concatenate([w_r[0, 0, rows, lanes], qg_r[0, 0, rows, lanes]], axis=0)
            m1s.append(_dot(lhs1, s.astype(BF16)))
        for (d, h), m1 in zip(chains, m1s):
            u_r, qk_r, kg_r = per_dir[d][2], per_dir[d][3], per_dir[d][4]
            chunk, rows, prow = where(d)
            lanes = slice(h * LANES, (h + 1) * LANES)
            v_new = (u_r[0, 0, rows, lanes] - m1[:DN_CHUNK]).astype(BF16)
            v_pad = jnp.concatenate([v_new, zpad] if chunk % 2 == 0 else [zpad, v_new], axis=0)
            lhs2 = jnp.concatenate([qk_r[0, 0, rows, lanes], kg_r[0, 0, prow, lanes]], axis=0)
            m2s.append(_dot(lhs2, v_pad))
        for (d, h), s, m1, m2 in zip(chains, states, m1s, m2s):
            gl_r, o_r = per_dir[d][5], per_dir[d][6]
            chunk, rows, _ = where(d)
            lanes = slice(h * LANES, (h + 1) * LANES)
            o_r[0, rows, lanes] = m1[DN_CHUNK:] + m2[:DN_CHUNK]
            ci = d * B_HEADS + h
            s_ref[d, h] = s * gl_r[0, chunk, ci:ci + 1, :] + m2[DN_CHUNK:]


def _stage3(w, qg, u, qk, kgt, gl, rows_blk):
    _, b, s, _ = w.shape
    n_i = s // rows_blk
    cpb = rows_blk // DN_CHUNK
    fwd = pl.BlockSpec((1, 1, rows_blk, D_B), lambda bi, i: (0, bi, i, 0))
    bwd = pl.BlockSpec((1, 1, rows_blk, D_B), lambda bi, i: (1, bi, n_i - 1 - i, 0))
    return pl.pallas_call(
        _stage3_kernel,
        grid=(b, n_i),
        in_specs=[fwd] * 5 + [bwd] * 5 + [
            pl.BlockSpec((1, cpb, N_GB, LANES), lambda bi, i: (bi, i, 0, 0)),
            pl.BlockSpec((1, cpb, N_GB, LANES), lambda bi, i: (bi, n_i - 1 - i, 0, 0)),
        ],
        out_specs=[pl.BlockSpec((1, rows_blk, D_B), lambda bi, i: (bi, i, 0)),
                   pl.BlockSpec((1, rows_blk, D_B), lambda bi, i: (bi, n_i - 1 - i, 0))],
        out_shape=[jax.ShapeDtypeStruct((b, s, D_B), F32), jax.ShapeDtypeStruct((b, s, D_B), F32)],
        scratch_shapes=[pltpu.VMEM((N_DIR, B_HEADS, B_DK, B_DV), F32)],
        compiler_params=pltpu.CompilerParams(
            dimension_semantics=("arbitrary", "arbitrary"), vmem_limit_bytes=VMEM_LIMIT),
        name="stage3_gdn_scan",
    )(w, qg, u, qk, kgt, w, qg, u, qk, kgt, gl, gl)


N_ROUTE = N_EXPERT_GROUPS + N_EXPERTS


def _stage4_kernel(x_ref, ya_ref, of_ref, ob_ref, z_ref, gnw_ref, woa_ref, wob_ref, fnw_ref,
                   wrh_ref, wrl_ref, h_ref, hn_ref, ids_ref, wts_ref):
    tm = x_ref.shape[0]
    o = of_ref[...] + ob_ref[...]
    z = z_ref[...]
    parts = []
    for hd in range(B_HEADS):
        lanes = slice(hd * B_DV, (hd + 1) * B_DV)
        oh = o[:, lanes]
        yh = oh * lax.rsqrt(jnp.mean(oh * oh, axis=-1, keepdims=True) + EPS) * gnw_ref[...]
        parts.append((yh * _silu(z[:, lanes])).astype(BF16))
    yb = jnp.concatenate(parts, axis=1)
    h = x_ref[...] + (_dot(ya_ref[...], woa_ref[...]) + _dot(yb, wob_ref[...]))
    h_ref[...] = h
    hn = h * lax.rsqrt(jnp.mean(h * h, axis=-1, keepdims=True) + EPS) * fnw_ref[...]
    hn_ref[...] = hn

    hi = hn.astype(BF16)
    lo = (hn - hi.astype(F32)).astype(BF16)
    logits = _dot(hi, wrh_ref[...]) + (_dot(lo, wrh_ref[...]) + _dot(hi, wrl_ref[...]))

    lane = lax.broadcasted_iota(jnp.int32, (tm, LANES), 1)
    neg = -jnp.inf
    is_g = lane < N_EXPERT_GROUPS
    gl = jnp.where(is_g, logits, neg)
    gmax = jnp.max(gl, axis=-1, keepdims=True)
    gidx = jnp.min(jnp.where(gl == gmax, lane, LANES), axis=-1, keepdims=True)
    g_w = 1.0 / jnp.sum(jnp.where(is_g, jnp.exp(gl - gmax), 0.0), axis=-1, keepdims=True)

    e0 = N_EXPERT_GROUPS + gidx * EXPERTS_PER_GROUP
    in_grp = (lane >= e0) & (lane < e0 + EXPERTS_PER_GROUP)
    el = jnp.where(in_grp, logits, neg)
    m1 = jnp.max(el, axis=-1, keepdims=True)
    i1 = jnp.min(jnp.where(el == m1, lane, LANES), axis=-1, keepdims=True)
    el2 = jnp.where(lane == i1, neg, el)
    m2 = jnp.max(el2, axis=-1, keepdims=True)
    i2 = jnp.min(jnp.where(el2 == m2, lane, LANES), axis=-1, keepdims=True)
    e2 = jnp.exp(m2 - m1)
    inv = 1.0 / (1.0 + e2)
    ids_ref[:, 0:1] = i1 - N_EXPERT_GROUPS
    ids_ref[:, 1:2] = i2 - N_EXPERT_GROUPS
    wts_ref[:, 0:1] = g_w * inv
    wts_ref[:, 1:2] = g_w * (e2 * inv)


def _stage4(x, ya, o_f, o_b, z, gnw, woa, wob, fnw, wrh, wrl, tm):
    t = x.shape[0]
    full = lambda a: pl.BlockSpec(a.shape, lambda i: (0,) * a.ndim)
    tile = lambda n: pl.BlockSpec((tm, n), lambda i: (i, 0))
    return pl.pallas_call(
        _stage4_kernel,
        grid=(t // tm,),
        in_specs=[tile(D_MODEL), tile(D_A), tile(D_B), tile(D_B), tile(D_B),
                  full(gnw), full(woa), full(wob), full(fnw), full(wrh), full(wrl)],
        out_specs=[tile(D_MODEL), tile(D_MODEL), tile(TOP_K), tile(TOP_K)],
        out_shape=[jax.ShapeDtypeStruct((t, D_MODEL), F32), jax.ShapeDtypeStruct((t, D_MODEL), F32),
                   jax.ShapeDtypeStruct((t, TOP_K), jnp.int32),
                   jax.ShapeDtypeStruct((t, TOP_K), F32)],
        compiler_params=pltpu.CompilerParams(
            dimension_semantics=("arbitrary",), vmem_limit_bytes=VMEM_LIMIT),
        name="stage4_outproj_router",
    )(x, ya, o_f, o_b, z, gnw, woa, wob, fnw, wrh, wrl)


def _rank_kernel(ids_ref, rank_ref, cnt_ref, carry_ref):
    tm = ids_ref.shape[0]

    @pl.when(pl.program_id(0) == 0)
    def _():
        carry_ref[...] = jnp.zeros_like(carry_ref)

    ids = ids_ref[...]
    lane = lax.broadcasted_iota(jnp.int32, (tm, LANES), 1)
    oh0 = jnp.where(lane == ids[:, 0:1], 1.0, 0.0)
    oh1 = jnp.where(lane == ids[:, 1:2], 1.0, 0.0)
    oh = oh0 + oh1
    row = lax.broadcasted_iota(jnp.int32, (tm, tm), 0)
    col = lax.broadcasted_iota(jnp.int32, (tm, tm), 1)
    earlier = jnp.where(row > col, 1.0, 0.0).astype(BF16)
    before = _dot(earlier, oh.astype(BF16)) + carry_ref[...]
    rank_ref[:, 0:1] = jnp.sum(before * oh0, axis=-1, keepdims=True).astype(jnp.int32)
    rank_ref[:, 1:2] = jnp.sum(before * oh1, axis=-1, keepdims=True).astype(jnp.int32)
    carry_ref[...] = carry_ref[...] + jnp.sum(oh, axis=0, keepdims=True)
    cnt_ref[...] = carry_ref[...]


def _expert_ranks(ids, tm):
    t = ids.shape[0]
    return pl.pallas_call(
        _rank_kernel,
        grid=(t // tm,),
        in_specs=[pl.BlockSpec((tm, TOP_K), lambda i: (i, 0))],
        out_specs=[pl.BlockSpec((tm, TOP_K), lambda i: (i, 0)),
                   pl.BlockSpec((1, LANES), lambda i: (0, 0))],
        out_shape=[jax.ShapeDtypeStruct((t, TOP_K), jnp.int32),
                   jax.ShapeDtypeStruct((1, LANES), F32)],
        scratch_shapes=[pltpu.VMEM((1, LANES), F32)],
        compiler_params=pltpu.CompilerParams(dimension_semantics=("arbitrary",)),
        name="stage5a_expert_ranks",
    )(ids)


MOE_BM = 256


def _expert_kernel(be_ref, nu_ref, tokc_ref, tokn_ref, hn_hbm, wg_ref, wu_ref, wd_ref,
                   y_ref, xbuf, sem):
    b = pl.program_id(0)
    n_used = nu_ref[0]
    slot = lax.rem(b, 2)
    bm = xbuf.shape[1]

    def row_copy(tok_ref, r, sl):
        return pltpu.make_async_copy(hn_hbm.at[pl.ds(tok_ref[0, 0, r], 1)],
                                     xbuf.at[sl, pl.ds(r, 1)], sem.at[sl])

    def start_rows(tok_ref, sl):
        def body(r, c):
            row_copy(tok_ref, r, sl).start()
            return c
        lax.fori_loop(0, bm, body, 0, unroll=8)

    def wait_rows(sl):
        pltpu.make_async_copy(hn_hbm.at[pl.ds(0, bm)], xbuf.at[sl], sem.at[sl]).wait()

    @pl.when(b == 0)
    def _():
        start_rows(tokc_ref, 0)

    @pl.when(b + 1 < n_used)
    def _():
        start_rows(tokn_ref, 1 - slot)

    @pl.when(b < n_used)
    def _():
        wait_rows(slot)
        x = xbuf[slot].astype(BF16)
        g = _dot(x, wg_ref[0])
        u = _dot(x, wu_ref[0])
        y_ref[...] = _dot((_silu(g) * u).astype(BF16), wd_ref[0])

    @pl.when(b >= n_used)
    def _():
        y_ref[...] = jnp.zeros_like(y_ref)


def _experts(block_expert, n_used, slot_tok, hn, wg, wu, wd):
    n_blocks = block_expert.shape[0]
    bm = MOE_BM
    tok3 = slot_tok.reshape(n_blocks, 1, bm)
    grid_spec = pltpu.PrefetchScalarGridSpec(
        num_scalar_prefetch=2,
        grid=(n_blocks,),
        in_specs=[
            pl.BlockSpec((1, 1, bm), lambda b, be, nu: (b, 0, 0), memory_space=pltpu.SMEM),
            pl.BlockSpec((1, 1, bm), lambda b, be, nu: (jnp.minimum(b + 1, n_blocks - 1), 0, 0),
                         memory_space=pltpu.SMEM),
            pl.BlockSpec(memory_space=pl.ANY),
            pl.BlockSpec((1, D_MODEL, D_EXPERT), lambda b, be, nu: (be[b], 0, 0)),
            pl.BlockSpec((1, D_MODEL, D_EXPERT), lambda b, be, nu: (be[b], 0, 0)),
            pl.BlockSpec((1, D_EXPERT, D_MODEL), lambda b, be, nu: (be[b], 0, 0)),
        ],
        out_specs=pl.BlockSpec((bm, D_MODEL), lambda b, be, nu: (b, 0)),
        scratch_shapes=[pltpu.VMEM((2, bm, D_MODEL), F32), pltpu.SemaphoreType.DMA((2,))],
    )
    return pl.pallas_call(
        _expert_kernel,
        grid_spec=grid_spec,
        out_shape=jax.ShapeDtypeStruct((n_blocks * bm, D_MODEL), F32),
        compiler_params=pltpu.CompilerParams(
            dimension_semantics=("arbitrary",), vmem_limit_bytes=VMEM_LIMIT),
        name="stage5b_experts",
    )(block_expert, n_used, tok3, tok3, hn, wg, wu, wd)


def _combine_kernel(n, dc_ref, dn_ref, h_ref, wts_ref, fw_ref, y_hbm, out_ref, ybuf, sem):
    i = pl.program_id(0)
    slot = lax.rem(i, 2)
    tm = h_ref.shape[0]
    nrows = TOP_K * tm

    def row_copy(d_ref, r, sl):
        return pltpu.make_async_copy(y_hbm.at[pl.ds(d_ref[0, 0, r], 1)],
                                     ybuf.at[sl, pl.ds(r, 1)], sem.at[sl])

    def start_rows(d_ref, sl):
        def body(r, c):
            row_copy(d_ref, r, sl).start()
            return c
        lax.fori_loop(0, nrows, body, 0, unroll=8)

    def wait_rows(sl):
        pltpu.make_async_copy(y_hbm.at[pl.ds(0, nrows)], ybuf.at[sl], sem.at[sl]).wait()

    @pl.when(i == 0)
    def _():
        start_rows(dc_ref, 0)

    @pl.when(i + 1 < n)
    def _():
        start_rows(dn_ref, 1 - slot)

    wait_rows(slot)
    w = wts_ref[...]
    moe = w[:, 0:1] * ybuf[slot, 0:tm, :] + w[:, 1:2] * ybuf[slot, tm:nrows, :]
    h = h_ref[...] + moe
    out_ref[...] = h * lax.rsqrt(jnp.mean(h * h, axis=-1, keepdims=True) + EPS) * fw_ref[...]


def _combine(dest_tiles, h, wts, fw, y_slots, tm):
    t = h.shape[0]
    n = t // tm
    return pl.pallas_call(
        functools.partial(_combine_kernel, n),
        grid=(n,),
        in_specs=[
            pl.BlockSpec((1, 1, TOP_K * tm), lambda i: (i, 0, 0), memory_space=pltpu.SMEM),
            pl.BlockSpec((1, 1, TOP_K * tm), lambda i: (jnp.minimum(i + 1, n - 1), 0, 0),
                         memory_space=pltpu.SMEM),
            pl.BlockSpec((tm, D_MODEL), lambda i: (i, 0)),
            pl.BlockSpec((tm, TOP_K), lambda i: (i, 0)),
            pl.BlockSpec((1, D_MODEL), lambda i: (0, 0)),
            pl.BlockSpec(memory_space=pl.ANY),
        ],
        out_specs=pl.BlockSpec((tm, D_MODEL), lambda i: (i, 0)),
        out_shape=jax.ShapeDtypeStruct((t, D_MODEL), F32),
        scratch_shapes=[pltpu.VMEM((2, TOP_K * tm, D_MODEL), F32), pltpu.SemaphoreType.DMA((2,))],
        compiler_params=pltpu.CompilerParams(
            dimension_semantics=("arbitrary",), vmem_limit_bytes=VMEM_LIMIT),
        name="stage6_combine_norm",
    )(dest_tiles, dest_tiles, h, wts, fw, y_slots)


def _moe_plan(ids, rank, counts):
    bm = MOE_BM
    n_assign = ids.shape[0] * TOP_K
    n_blocks = -(-n_assign // bm) + N_EXPERTS
    cnt = counts[0, :N_EXPERTS].astype(jnp.int32)
    padded = (cnt + bm - 1) // bm * bm
    pad_end = jnp.cumsum(padded)
    pad_start = pad_end - padded
    dest = pad_start[ids] + rank
    n_used = (pad_end[-1] // bm).astype(jnp.int32)
    blk = jnp.arange(n_blocks, dtype=jnp.int32)
    first_row = jnp.minimum(blk, n_used - 1) * bm
    block_expert = jnp.sum((pad_end[None, :] <= first_row[:, None]).astype(jnp.int32), axis=1)
    block_expert = jnp.minimum(block_expert, N_EXPERTS - 1)
    flat_tok = jnp.arange(n_assign, dtype=jnp.int32) // TOP_K
    slot_tok = jnp.zeros((n_blocks * bm,), jnp.int32).at[dest.reshape(-1)].set(flat_tok)
    return dest, slot_tok, block_expert, n_used.reshape(1)


def kernel(x_prompt, x_sample, norm_mix_w, w_in, a_ln_w, a_ln_b, a_spatial_w, a_spatial_b, a_out_norm_w, conv_w, a_log, dt_bias, gdn_norm_w, w_out, norm_ffn_w, w_router_group, w_router_expert, w_gate, w_up, w_down, norm_final_w):
    s1w = _prep_stage1_weights(norm_mix_w[0], w_in[0], a_ln_w[0], a_ln_b[0], a_spatial_w[0],
                               a_spatial_b[0], a_out_norm_w[0], a_log[0], dt_bias[0])
    woa = w_out[0, :D_A].astype(BF16)
    wob = w_out[0, D_A:].astype(BF16)
    w_r = jnp.concatenate([w_router_group[0], w_router_expert[0]], axis=1)
    w_r = jnp.pad(w_r, ((0, 0), (0, LANES - N_ROUTE)))
    wrh = w_r.astype(BF16)
    wrl = (w_r - wrh.astype(F32)).astype(BF16)
    gnw = gdn_norm_w[0].reshape(1, B_DV)
    fnw = norm_ffn_w[0].reshape(1, D_MODEL)

    tm = 256
    per_run = []
    for x in (x_prompt, x_sample):
        b, s, d = x.shape
        x2 = x.reshape(b * s, d)
        ya, qkv, z, gb, gbt = _stage1(x2, *s1w, tm=tm)
        w, qg, u, qk, kgt, gl = _stage2(qkv.reshape(b, s, D_CONV), conv_w[0],
                                        gb.reshape(b, s, N_GB), gbt, tc=256)
        o_f, o_b = _stage3(w, qg, u, qk, kgt, gl, rows_blk=512)
        h, hn, ids, wts = _stage4(x2, ya, o_f.reshape(b * s, D_B), o_b.reshape(b * s, D_B), z,
                                  gnw, woa, wob, fnw, wrh, wrl, tm=tm)
        per_run.append((x.shape, h, hn, ids, wts))

    hn_all = jnp.concatenate([r[2] for r in per_run], axis=0)
    ids_all = jnp.concatenate([r[3] for r in per_run], axis=0)
    rank, counts = _expert_ranks(ids_all, tm=512)
    dest, slot_tok, block_expert, n_used = _moe_plan(ids_all, rank, counts)
    y_slots = _experts(block_expert, n_used, slot_tok, hn_all,
                       w_gate[0].astype(BF16), w_up[0].astype(BF16), w_down[0].astype(BF16))

    outs = []
    t0 = 0
    fw = norm_final_w.reshape(1, D_MODEL)
    for shape, h, _, _, wts in per_run:
        t = h.shape[0]
        d_run = dest[t0:t0 + t].reshape(t // tm, tm, TOP_K).transpose(0, 2, 1)
        d_run = d_run.reshape(t // tm, 1, TOP_K * tm)
        outs.append(_combine(d_run, h, wts, fw, y_slots, tm=tm).reshape(shape))
        t0 += t
    return tuple(outs)
```

```python
import functools
import math

import jax
import jax.numpy as jnp
from jax import lax
from jax.experimental import pallas as pl
from jax.experimental.pallas import tpu as pltpu

D_MODEL = 1024
D_A = 512
A_GROUPS = 4
A_DG = 128
A_CHUNK = 128
D_B = 512
B_HEADS = 4
B_DK = 128
B_DV = 128
D_QK = 512
DN_CHUNK = 64
CONV_W = 5
N_DIR = 2
N_EXPERT_GROUPS = 4
EXPERTS_PER_GROUP = 8
N_EXPERTS = 32
TOP_K = 2
D_EXPERT = 512
EPS = 1e-6
D_CONV = 2 * D_QK + D_B
N_GB = 2 * N_DIR * B_HEADS

LANES = 128
VMEM_LIMIT = 48 * 1024 * 1024

BF16 = jnp.bfloat16
F32 = jnp.float32


def _dot(a, b):
    return jnp.dot(a, b, preferred_element_type=F32)


def _dot_nt(a, b):
    return lax.dot_general(a, b, (((1,), (1,)), ((), ())), preferred_element_type=F32)


def _gelu_tanh(x):
    c = math.sqrt(2.0 / math.pi)
    return x * (0.5 * (1.0 + jnp.tanh(c * (x + 0.044715 * (x * x * x)))))


def _sigmoid(x):
    return 1.0 / (1.0 + jnp.exp(-x))


def _silu(x):
    return x * _sigmoid(x)


def _softplus(x):
    return jnp.maximum(x, 0.0) + jnp.log(1.0 + jnp.exp(-jnp.abs(x)))


def _stage1_kernel(x_ref, nw_ref, wuv_ref, wqkv_ref, wz_ref, wgb_ref, wgbt_ref,
                   lnw_ref, lnb_ref, spw_ref, spbt_ref, onw_ref, alog_ref, dtb_ref, alogt_ref, dtbt_ref,
                   ya_ref, qkv_ref, z_ref, gb_ref, gbt_ref):
    tm = x_ref.shape[0]
    x = x_ref[...]
    xn = x * lax.rsqrt(jnp.mean(x * x, axis=-1, keepdims=True) + EPS) * nw_ref[...]
    xb = xn.astype(BF16)

    qkv_ref[...] = _dot(xb, wqkv_ref[...])
    z_ref[...] = _dot(xb, wz_ref[...])

    ab = _dot(xb, wgb_ref[...])[:, :N_GB]
    abt = _dot_nt(wgbt_ref[...], xb)
    na = N_DIR * B_HEADS
    g = -jnp.exp(alog_ref[...]) * _softplus(ab[:, :na] + dtb_ref[...])
    gb_ref[:, :na] = g
    gb_ref[:, na:] = _sigmoid(ab[:, na:])
    gt = -jnp.exp(alogt_ref[...]) * _softplus(abt[:na, :] + dtbt_ref[...])
    gbt_ref[:na, :] = gt
    gbt_ref[na:, :] = _sigmoid(abt[na:, :])

    for grp in range(A_GROUPS):
        cols = slice(grp * A_DG, (grp + 1) * A_DG)
        uv_all = _dot(xb, wuv_ref[:, 2 * grp * A_DG:2 * (grp + 1) * A_DG])
        u_all = uv_all[:, :A_DG]
        v_all = uv_all[:, A_DG:]
        spw = spw_ref[grp]
        for c in range(tm // A_CHUNK):
            rows = slice(c * A_CHUNK, (c + 1) * A_CHUNK)
            u = _gelu_tanh(u_all[rows])
            v = _gelu_tanh(v_all[rows])
            mu = jnp.mean(v, axis=-1, keepdims=True)
            vc = v - mu
            var = jnp.mean(vc * vc, axis=-1, keepdims=True)
            vn = vc * lax.rsqrt(var + EPS) * lnw_ref[:, cols] + lnb_ref[:, cols]
            mixed = _dot(spw, vn.astype(BF16)) + spbt_ref[:, grp:grp + 1]
            gated = u * mixed
            out = gated * lax.rsqrt(jnp.mean(gated * gated, axis=-1, keepdims=True) + EPS)
            ya_ref[rows, cols] = (out * onw_ref[:, cols]).astype(BF16)


def _stage1(x, nw, wuv, wqkv, wz, wgb, wgbt, lnw, lnb, spw, spbt, onw, alog, dtb, alogt, dtbt, tm):
    t = x.shape[0]
    full = lambda shape: pl.BlockSpec(shape, lambda i: (0,) * len(shape))
    return pl.pallas_call(
        _stage1_kernel,
        grid=(t // tm,),
        in_specs=[
            pl.BlockSpec((tm, D_MODEL), lambda i: (i, 0)),
            full(nw.shape), full(wuv.shape), full(wqkv.shape), full(wz.shape),
            full(wgb.shape), full(wgbt.shape), full(lnw.shape), full(lnb.shape),
            full(spw.shape), full(spbt.shape), full(onw.shape), full(alog.shape), full(dtb.shape),
            full(alogt.shape), full(dtbt.shape),
        ],
        out_specs=[
            pl.BlockSpec((tm, D_A), lambda i: (i, 0)),
            pl.BlockSpec((tm, D_CONV), lambda i: (i, 0)),
            pl.BlockSpec((tm, D_B), lambda i: (i, 0)),
            pl.BlockSpec((tm, N_GB), lambda i: (i, 0)),
            pl.BlockSpec((N_GB, tm), lambda i: (0, i)),
        ],
        out_shape=[
            jax.ShapeDtypeStruct((t, D_A), BF16),
            jax.ShapeDtypeStruct((t, D_CONV), F32),
            jax.ShapeDtypeStruct((t, D_B), F32),
            jax.ShapeDtypeStruct((t, N_GB), F32),
            jax.ShapeDtypeStruct((N_GB, t), F32),
        ],
        compiler_params=pltpu.CompilerParams(
            dimension_semantics=("arbitrary",), vmem_limit_bytes=VMEM_LIMIT),
        name="stage1_inproj_gmlp",
    )(x, nw, wuv, wqkv, wz, wgb, wgbt, lnw, lnb, spw, spbt, onw, alog, dtb, alogt, dtbt)


def _prep_stage1_weights(norm_mix_w, w_in, a_ln_w, a_ln_b, a_spatial_w, a_spatial_b,
                         a_out_norm_w, a_log, dt_bias):
    c0 = 2 * D_A
    c1 = c0 + D_CONV
    c2 = c1 + D_B
    wuv = w_in[:, :c0].reshape(D_MODEL, 2, A_GROUPS, A_DG).transpose(0, 2, 1, 3)
    wuv = wuv.reshape(D_MODEL, c0).astype(BF16)
    wqkv = w_in[:, c0:c1].astype(BF16)
    wz = w_in[:, c1:c2].astype(BF16)
    wgb_raw = w_in[:, c2:]
    wgb = jnp.pad(wgb_raw, ((0, 0), (0, LANES - N_GB))).astype(BF16)
    wgbt = wgb_raw.T.astype(BF16)
    return (norm_mix_w.reshape(1, D_MODEL), wuv, wqkv, wz, wgb, wgbt,
            a_ln_w.reshape(1, D_A), a_ln_b.reshape(1, D_A), a_spatial_w.astype(BF16),
            a_spatial_b.T, a_out_norm_w.reshape(1, D_A),
            a_log.reshape(1, N_DIR * B_HEADS), dt_bias.reshape(1, N_DIR * B_HEADS),
            a_log.reshape(N_DIR * B_HEADS, 1), dt_bias.reshape(N_DIR * B_HEADS, 1))


PAIR = 2 * DN_CHUNK
HALO = 8


def _split3(x):
    hi = x.astype(BF16)
    r1 = x - hi.astype(F32)
    mid = r1.astype(BF16)
    lo = (r1 - mid.astype(F32)).astype(BF16)
    return hi, mid, lo


def _dot_exact_rhs01(x, m01):
    hi, mid, lo = _split3(x)
    return _dot(hi, m01) + _dot(mid, m01) + _dot(lo, m01)


def _dot_exact_lhs01(m01, x):
    hi, mid, lo = _split3(x)
    return _dot(m01, hi) + _dot(m01, mid) + _dot(m01, lo)


def _unit_tri_inverses(a_negs):
    n = a_negs[0].shape[0]
    row = lax.broadcasted_iota(jnp.int32, (n, n), 0)
    col = lax.broadcasted_iota(jnp.int32, (n, n), 1)
    eye = jnp.where(row == col, 1.0, 0.0)
    r = [eye + a for a in a_negs]
    p16 = [a.astype(BF16) for a in a_negs]
    pw = [_dot(x, x) for x in p16]
    levels = int(math.log2(DN_CHUNK)) - 1
    for lvl in range(levels):
        p16 = [x.astype(BF16) for x in pw]
        if lvl < levels - 1:
            both = [_dot(jnp.concatenate([ri.astype(BF16), pi], axis=0), pi)
                    for ri, pi in zip(r, p16)]
            r = [ri + bi[:n] for ri, bi in zip(r, both)]
            pw = [bi[n:] for bi in both]
        else:
            r = [ri + _dot(ri.astype(BF16), pi) for ri, pi in zip(r, p16)]
    return r


def _stage2_kernel(n_i, x_ref, xp_ref, xn_ref, cw_ref, gb_ref, gbt_ref,
                   w_ref, qg_ref, u_ref, qk_ref, kgt_ref, gl_ref, xpad_ref, act_ref):
    i = pl.program_id(1)
    tc = x_ref.shape[1]
    top = 6

    xpad_ref[0:HALO, :] = jnp.where(i > 0, xp_ref[0], 0.0)
    xpad_ref[HALO:HALO + tc, :] = x_ref[0]
    xpad_ref[HALO + tc:, :] = jnp.where(i < n_i - 1, xn_ref[0], 0.0)

    for cb in range(D_CONV // LANES):
        cols = slice(cb * LANES, (cb + 1) * LANES)
        for r0 in range(0, tc, PAIR):
            acc = cw_ref[0:1, cols] * xpad_ref[top + r0:top + r0 + PAIR, cols]
            for j in range(1, CONV_W):
                acc = acc + cw_ref[j:j + 1, cols] * xpad_ref[top + j + r0:top + j + r0 + PAIR, cols]
            act_ref[r0:r0 + PAIR, cols] = _silu(acc)

    row = lax.broadcasted_iota(jnp.int32, (PAIR, PAIR), 0)
    col = lax.broadcasted_iota(jnp.int32, (PAIR, PAIR), 1)
    same = (row >= DN_CHUNK) == (col >= DN_CHUNK)
    incl = (same & (row >= col), same & (row <= col))
    strict = (same & (row > col), same & (row < col))
    as01 = lambda m: jnp.where(m, 1.0, 0.0).astype(BF16)
    m_incl = tuple(as01(m) for m in incl)
    m_same = as01(same)
    e_chunk = (as01(row < DN_CHUNK), as01(row >= DN_CHUNK))
    na = N_DIR * B_HEADS

    for p in range(tc // PAIR):
        rows = slice(p * PAIR, (p + 1) * PAIR)
        gbp = gb_ref[0, rows, :]
        gbtp = gbt_ref[:, rows]
        gcol = tuple(_dot_exact_lhs01(m_incl[d], gbp) for d in range(N_DIR))
        grow = tuple(_dot_exact_rhs01(gbtp, m_incl[1 - d]) for d in range(N_DIR))
        tot_row = _dot_exact_rhs01(gbtp, m_same)
        for c in range(2):
            gl_ref[0, 2 * p + c] = jnp.exp(_dot_exact_rhs01(gbtp, e_chunk[c]))

        heads = []
        for h in range(B_HEADS):
            q = act_ref[rows, h * B_DK:(h + 1) * B_DK]
            k = act_ref[rows, D_QK + h * B_DK:D_QK + (h + 1) * B_DK]
            qn = q * lax.rsqrt(jnp.sum(q * q, axis=-1, keepdims=True) + EPS) * (B_DK ** -0.5)
            kn = k * lax.rsqrt(jnp.sum(k * k, axis=-1, keepdims=True) + EPS)
            kt = kn.T
            heads.append((qn, kn, kt, kt.astype(BF16)))
        kks = [_dot(kn.astype(BF16), kt16) for _, kn, _, kt16 in heads]
        qks = [_dot(qn.astype(BF16), kt16) for qn, _, _, kt16 in heads]

        chains = [(d, h) for d in range(N_DIR) for h in range(B_HEADS)]
        a_negs, rhss = [], []
        for d, h in chains:
            ci = d * B_HEADS + h
            lanes = slice(h * LANES, (h + 1) * LANES)
            qn, kn, kt, _ = heads[h]
            v = act_ref[rows, 2 * D_QK + h * B_DV:2 * D_QK + (h + 1) * B_DV]
            gc = gcol[d][:, ci:ci + 1]
            gr = grow[d][ci:ci + 1, :]
            beta = gbp[:, na + ci:na + ci + 1]
            decay = jnp.where(incl[d], jnp.exp(gc - gr), 0.0)
            a_negs.append(jnp.where(strict[d], -(kks[h] * beta * decay), 0.0))
            eg = jnp.exp(gc)
            rhss.append(jnp.concatenate([v * beta, kn * (beta * eg)], axis=1).astype(BF16))
            qk_ref[d, 0, rows, lanes] = (qks[h] * decay).astype(BF16)
            qg_ref[d, 0, rows, lanes] = (qn * eg).astype(BF16)
            kgt_ref[d, 0, rows, lanes] = (kt * jnp.exp(tot_row[ci:ci + 1, :] - gr)).astype(BF16)

        tinvs = _unit_tri_inverses(a_negs)
        uws = [_dot(t.astype(BF16), rhs) for t, rhs in zip(tinvs, rhss)]
        for (d, h), uw in zip(chains, uws):
            lanes = slice(h * LANES, (h + 1) * LANES)
            u_ref[d, 0, rows, lanes] = uw[:, :B_DV]
            w_ref[d, 0, rows, lanes] = uw[:, B_DV:].astype(BF16)


def _stage2(qkv, conv_w, gb, gbt, tc):
    b, s, _ = qkv.shape
    n_i = s // tc
    hb = tc // HALO
    dirs = lambda shape, dtype: jax.ShapeDtypeStruct((N_DIR, b, s) + shape, dtype)
    out_block = pl.BlockSpec((N_DIR, 1, tc, D_B), lambda bi, i: (0, bi, i, 0))
    return pl.pallas_call(
        functools.partial(_stage2_kernel, n_i),
        grid=(b, n_i),
        in_specs=[
            pl.BlockSpec((1, tc, D_CONV), lambda bi, i: (bi, i, 0)),
            pl.BlockSpec((1, HALO, D_CONV), lambda bi, i: (bi, jnp.maximum(i * hb - 1, 0), 0)),
            pl.BlockSpec((1, HALO, D_CONV),
                         lambda bi, i: (bi, jnp.minimum((i + 1) * hb, s // HALO - 1), 0)),
            pl.BlockSpec((CONV_W, D_CONV), lambda bi, i: (0, 0)),
            pl.BlockSpec((1, tc, N_GB), lambda bi, i: (bi, i, 0)),
            pl.BlockSpec((N_GB, tc), lambda bi, i: (0, bi * n_i + i)),
        ],
        out_specs=[out_block, out_block, out_block, out_block, out_block,
                   pl.BlockSpec((1, tc // DN_CHUNK, N_GB, LANES), lambda bi, i: (bi, i, 0, 0))],
        out_shape=[dirs((D_B,), BF16), dirs((D_B,), BF16), dirs((D_B,), F32),
                   dirs((D_B,), BF16), dirs((D_B,), BF16),
                   jax.ShapeDtypeStruct((b, s // DN_CHUNK, N_GB, LANES), F32)],
        scratch_shapes=[pltpu.VMEM((tc + 2 * HALO, D_CONV), F32), pltpu.VMEM((tc, D_CONV), F32)],
        compiler_params=pltpu.CompilerParams(
            dimension_semantics=("arbitrary", "arbitrary"), vmem_limit_bytes=VMEM_LIMIT),
        name="stage2_gdn_chunk_prep",
    )(qkv, qkv, qkv, conv_w, gb, gbt)


def _stage3_kernel(wf, qgf, uf, qkf, kgf, wb, qgb, ub, qkb, kgb, glf, glb, of_ref, ob_ref, s_ref):
    i = pl.program_id(1)

    @pl.when(i == 0)
    def _():
        s_ref[...] = jnp.zeros_like(s_ref)

    rows_blk = wf.shape[2]
    npairs = rows_blk // PAIR
    zpad = jnp.zeros((DN_CHUNK, B_DV), BF16)
    per_dir = ((wf, qgf, uf, qkf, kgf, glf, of_ref), (wb, qgb, ub, qkb, kgb, glb, ob_ref))
    chains = [(d, h) for d in range(N_DIR) for h in range(B_HEADS)]
    for step in range(2 * npairs):
        def where(d):
            chunk = step if d == 0 else 2 * npairs - 1 - step
            pair = chunk // 2
            return (chunk, slice(chunk * DN_CHUNK, (chunk + 1) * DN_CHUNK),
                    slice(pair * PAIR, (pair + 1) * PAIR))

        states, m1s, m2s = [], [], []
        for d, h in chains:
            w_r, qg_r = per_dir[d][0], per_dir[d][1]
            _, rows, _ = where(d)
            lanes = slice(h * LANES, (h + 1) * LANES)
            s = s_ref[d, h]
            states.append(s)
            lhs1 = jnp.concatenate([w_r[0, 0, rows, lanes], qg_r[0, 0, rows, lanes]], axis=0)
            m1s.append(_dot(lhs1, s.astype(BF16)))
        for (d, h), m1 in zip(chains, m1s):
            u_r, qk_r, kg_r = per_dir[d][2], per_dir[d][3], per_dir[d][4]
            chunk, rows, prow = where(d)
            lanes = slice(h * LANES, (h + 1) * LANES)
            v_new = (u_r[0, 0, rows, lanes] - m1[:DN_CHUNK]).astype(BF16)
            v_pad = jnp.concatenate([v_new, zpad] if chunk % 2 == 0 else [zpad, v_new], axis=0)
            lhs2 = jnp.concatenate([qk_r[0, 0, rows, lanes], kg_r[0, 0, prow, lanes]], axis=0)
            m2s.append(_dot(lhs2, v_pad))
        for (d, h), s, m1, m2 in zip(chains, states, m1s, m2s):
            gl_r, o_r = per_dir[d][5], per_dir[d][6]
            chunk, rows, _ = where(d)
            lanes = slice(h * LANES, (h + 1) * LANES)
            o_r[0, rows, lanes] = m1[DN_CHUNK:] + m2[:DN_CHUNK]
            ci = d * B_HEADS + h
            s_ref[d, h] = s * gl_r[0, chunk, ci:ci + 1, :] + m2[DN_CHUNK:]


def _stage3(w, qg, u, qk, kgt, gl, rows_blk):
    _, b, s, _ = w.shape
    n_i = s // rows_blk
    cpb = rows_blk // DN_CHUNK
    fwd = pl.BlockSpec((1, 1, rows_blk, D_B), lambda bi, i: (0, bi, i, 0))
    bwd = pl.BlockSpec((1, 1, rows_blk, D_B), lambda bi, i: (1, bi, n_i - 1 - i, 0))
    return pl.pallas_call(
        _stage3_kernel,
        grid=(b, n_i),
        in_specs=[fwd] * 5 + [bwd] * 5 + [
            pl.BlockSpec((1, cpb, N_GB, LANES), lambda bi, i: (bi, i, 0, 0)),
            pl.BlockSpec((1, cpb, N_GB, LANES), lambda bi, i: (bi, n_i - 1 - i, 0, 0)),
        ],
        out_specs=[pl.BlockSpec((1, rows_blk, D_B), lambda bi, i: (bi, i, 0)),
                   pl.BlockSpec((1, rows_blk, D_B), lambda bi, i: (bi, n_i - 1 - i, 0))],
        out_shape=[jax.ShapeDtypeStruct((b, s, D_B), F32), jax.ShapeDtypeStruct((b, s, D_B), F32)],
        scratch_shapes=[pltpu.VMEM((N_DIR, B_HEADS, B_DK, B_DV), F32)],
        compiler_params=pltpu.CompilerParams(
            dimension_semantics=("arbitrary", "arbitrary"), vmem_limit_bytes=VMEM_LIMIT),
        name="stage3_gdn_scan",
    )(w, qg, u, qk, kgt, w, qg, u, qk, kgt, gl, gl)


N_ROUTE = N_EXPERT_GROUPS + N_EXPERTS


def _stage4_kernel(x_ref, ya_ref, of_ref, ob_ref, z_ref, gnw_ref, woa_ref, wob_ref, fnw_ref,
                   wrh_ref, wrl_ref, h_ref, hn_ref, ids_ref, wts_ref, cnt_ref):
    tm = x_ref.shape[0]
    o = of_ref[...] + ob_ref[...]
    z = z_ref[...]
    parts = []
    for hd in range(B_HEADS):
        lanes = slice(hd * B_DV, (hd + 1) * B_DV)
        oh = o[:, lanes]
        yh = oh * lax.rsqrt(jnp.mean(oh * oh, axis=-1, keepdims=True) + EPS) * gnw_ref[...]
        parts.append((yh * _silu(z[:, lanes])).astype(BF16))
    yb = jnp.concatenate(parts, axis=1)
    h = x_ref[...] + (_dot(ya_ref[...], woa_ref[...]) + _dot(yb, wob_ref[...]))
    h_ref[...] = h
    hn = h * lax.rsqrt(jnp.mean(h * h, axis=-1, keepdims=True) + EPS) * fnw_ref[...]

    hi = hn.astype(BF16)
    hn_ref[...] = hi
    lo = (hn - hi.astype(F32)).astype(BF16)
    logits = _dot(hi, wrh_ref[...]) + (_dot(lo, wrh_ref[...]) + _dot(hi, wrl_ref[...]))

    lane = lax.broadcasted_iota(jnp.int32, (tm, LANES), 1)
    neg = -jnp.inf
    is_g = lane < N_EXPERT_GROUPS
    gl = jnp.where(is_g, logits, neg)
    gmax = jnp.max(gl, axis=-1, keepdims=True)
    gidx = jnp.min(jnp.where(gl == gmax, lane, LANES), axis=-1, keepdims=True)
    g_w = 1.0 / jnp.sum(jnp.where(is_g, jnp.exp(gl - gmax), 0.0), axis=-1, keepdims=True)

    e0 = N_EXPERT_GROUPS + gidx * EXPERTS_PER_GROUP
    in_grp = (lane >= e0) & (lane < e0 + EXPERTS_PER_GROUP)
    el = jnp.where(in_grp, logits, neg)
    m1 = jnp.max(el, axis=-1, keepdims=True)
    i1 = jnp.min(jnp.where(el == m1, lane, LANES), axis=-1, keepdims=True)
    el2 = jnp.where(lane == i1, neg, el)
    m2 = jnp.max(el2, axis=-1, keepdims=True)
    i2 = jnp.min(jnp.where(el2 == m2, lane, LANES), axis=-1, keepdims=True)
    e2 = jnp.exp(m2 - m1)
    inv = 1.0 / (1.0 + e2)
    ids_ref[:, 0:1] = i1 - N_EXPERT_GROUPS
    ids_ref[:, 1:2] = i2 - N_EXPERT_GROUPS
    wts_ref[:, 0:1] = g_w * inv
    wts_ref[:, 1:2] = g_w * (e2 * inv)
    elane = lane + N_EXPERT_GROUPS
    chosen = (elane == i1) | (elane == i2)
    cnt_ref[0] = jnp.sum(jnp.where(chosen, 1.0, 0.0), axis=0, keepdims=True)


def _stage4(x, ya, o_f, o_b, z, gnw, woa, wob, fnw, wrh, wrl, tm):
    t = x.shape[0]
    full = lambda a: pl.BlockSpec(a.shape, lambda i: (0,) * a.ndim)
    tile = lambda n: pl.BlockSpec((tm, n), lambda i: (i, 0))
    return pl.pallas_call(
        _stage4_kernel,
        grid=(t // tm,),
        in_specs=[tile(D_MODEL), tile(D_A), tile(D_B), tile(D_B), tile(D_B),
                  full(gnw), full(woa), full(wob), full(fnw), full(wrh), full(wrl)],
        out_specs=[tile(D_MODEL), tile(D_MODEL), tile(TOP_K), tile(TOP_K),
                   pl.BlockSpec((1, 1, LANES), lambda i: (i, 0, 0))],
        out_shape=[jax.ShapeDtypeStruct((t, D_MODEL), F32), jax.ShapeDtypeStruct((t, D_MODEL), BF16),
                   jax.ShapeDtypeStruct((t, TOP_K), jnp.int32),
                   jax.ShapeDtypeStruct((t, TOP_K), F32),
                   jax.ShapeDtypeStruct((t // tm, 1, LANES), F32)],
        compiler_params=pltpu.CompilerParams(
            dimension_semantics=("arbitrary",), vmem_limit_bytes=VMEM_LIMIT),
        name="stage4_outproj_router",
    )(x, ya, o_f, o_b, z, gnw, woa, wob, fnw, wrh, wrl)


MOE_TM = 512
MOE_R = 16
MOE_BM = 256
MOE_L = TOP_K * MOE_TM + N_EXPERTS * MOE_R


def _moe_plan(cnt):
    n_tiles = cnt.shape[0]
    c = cnt[:, :N_EXPERTS].astype(jnp.int32)
    cpad = (c + MOE_R - 1) // MOE_R * MOE_R
    seg = jnp.sum(cpad, axis=0)
    segpad = (seg + MOE_BM - 1) // MOE_BM * MOE_BM
    pad_end = jnp.cumsum(segpad)
    pad_start = pad_end - segpad
    off = pad_start[None, :] + jnp.cumsum(cpad, axis=0) - cpad
    loc = jnp.cumsum(cpad, axis=1) - cpad
    nch = cpad // MOE_R
    n_blocks = -(-(TOP_K * MOE_TM + N_EXPERTS * (MOE_R - 1)) * n_tiles // MOE_BM) + N_EXPERTS
    n_used = pad_end[-1] // MOE_BM
    blk = jnp.arange(n_blocks, dtype=jnp.int32)
    first_row = jnp.minimum(blk, n_used - 1) * MOE_BM
    block_expert = jnp.sum((pad_end[None, :] <= first_row[:, None]).astype(jnp.int32), axis=1)
    block_expert = jnp.minimum(block_expert, N_EXPERTS - 1)
    loc_lanes = jnp.pad(loc, ((0, 0), (0, LANES - N_EXPERTS))).astype(F32)
    return dict(off=off.reshape(-1), loc=loc.reshape(-1), nch=nch.reshape(-1),
                tot=jnp.sum(nch, axis=1), tail_off=pad_start + seg,
                tail_n=(segpad - seg) // MOE_R, loc_lanes=loc_lanes.reshape(n_tiles, 1, LANES),
                block_expert=block_expert, n_used=n_used.reshape(1), n_blocks=n_blocks)


def _run_chunks(nch_ref, tile, visit):
    def per_expert(e, carry):
        def per_chunk(j, c):
            visit(e, j)
            return c
        lax.fori_loop(0, nch_ref[tile * N_EXPERTS + e], per_chunk, 0)
        return carry
    lax.fori_loop(0, N_EXPERTS, per_expert, 0)


def _chunk_row(base_ref, tile, e, j):
    return pl.multiple_of(base_ref[tile * N_EXPERTS + e] + j * MOE_R, MOE_R)


def _dispatch_kernel(n, n_first, n_blocks, off_ref, loc_ref, nch_ref, tot_ref, toff_ref, tn_ref,
                     nu_ref, ids_ref, hna_ref, hnb_ref, locl_ref, xs_ref, pos_ref,
                     hn_ref, xl, zrows, sem, tsem):
    i = pl.program_id(0)
    g = i
    slot = lax.rem(i, 2)
    tm = ids_ref.shape[0]

    @pl.when(i < n_first)
    def _():
        hn_ref[...] = hna_ref[...]

    @pl.when(i >= n_first)
    def _():
        hn_ref[...] = hnb_ref[...]

    ids = ids_ref[...]
    lane = lax.broadcasted_iota(jnp.int32, (tm, LANES), 1)
    oh0 = jnp.where(lane == ids[:, 0:1], 1.0, 0.0)
    oh1 = jnp.where(lane == ids[:, 1:2], 1.0, 0.0)
    row = lax.broadcasted_iota(jnp.int32, (tm, tm), 0)
    col = lax.broadcasted_iota(jnp.int32, (tm, tm), 1)
    earlier = jnp.where(row > col, 1.0, 0.0).astype(BF16)
    base = _dot(earlier, (oh0 + oh1).astype(BF16)) + locl_ref[0]
    m0 = base * oh0
    m1 = base * oh1
    pos_ref[:, 0:1] = jnp.sum(m0, axis=-1, keepdims=True).astype(jnp.int32)
    pos_ref[:, 1:2] = jnp.sum(m1, axis=-1, keepdims=True).astype(jnp.int32)
    ones = jnp.ones((8, LANES), BF16)
    lane_form = lambda m: sum(_dot_nt(ones, part) for part in _split3(m))[0:1].astype(jnp.int32)
    p0 = lane_form(m0)
    p1 = lane_form(m1)
    srow = lax.broadcasted_iota(jnp.int32, (MOE_L, tm), 0)
    perm = jnp.where((srow == p0) | (srow == p1), 1.0, 0.0).astype(BF16)
    nb = 256
    for cb in range(D_MODEL // nb):
        xl[slot, :, cb * nb:(cb + 1) * nb] = _dot(perm, hn_ref[:, cb * nb:(cb + 1) * nb]).astype(BF16)

    def run_copy(tile, sl, e, j):
        return pltpu.make_async_copy(
            xl.at[sl, pl.ds(_chunk_row(loc_ref, tile, e, j), MOE_R)],
            xs_ref.at[pl.ds(_chunk_row(off_ref, tile, e, j), MOE_R)], sem.at[sl])

    _run_chunks(nch_ref, g, lambda e, j: run_copy(g, slot, e, j).start())

    def wait_tile(tile, sl):
        def body(k, c):
            pltpu.make_async_copy(xl.at[sl, pl.ds(0, MOE_R)], xs_ref.at[pl.ds(0, MOE_R)],
                                  sem.at[sl]).wait()
            return c
        lax.fori_loop(0, tot_ref[tile], body, 0)

    @pl.when(i > 0)
    def _():
        wait_tile(g - 1, 1 - slot)

    @pl.when(i == n - 1)
    def _():
        wait_tile(g, slot)
        zrows[...] = jnp.zeros_like(zrows)

        def tail_copy(e, j):
            row0 = pl.multiple_of(toff_ref[e] + j * MOE_R, MOE_R)
            return pltpu.make_async_copy(zrows.at[pl.ds(0, MOE_R)],
                                         xs_ref.at[pl.ds(row0, MOE_R)], tsem.at[0])

        def block_copy(b):
            row0 = pl.multiple_of(b * MOE_BM, MOE_BM)
            return pltpu.make_async_copy(zrows, xs_ref.at[pl.ds(row0, MOE_BM)], tsem.at[0])

        def fill(act):
            def per_expert(e, carry):
                def per_chunk(j, c):
                    act(tail_copy(e, j))
                    return c
                lax.fori_loop(0, tn_ref[e], per_chunk, 0)
                return carry
            lax.fori_loop(0, N_EXPERTS, per_expert, 0)

            def per_block(b, c):
                act(block_copy(b))
                return c
            lax.fori_loop(nu_ref[0], n_blocks, per_block, 0)

        fill(lambda cp: cp.start())
        fill(lambda cp: cp.wait())


def _dispatch(plan, ids, hn_a, hn_b):
    n_a = hn_a.shape[0] // MOE_TM
    n = ids.shape[0] // MOE_TM
    n_blocks = plan['n_blocks']
    grid_spec = pltpu.PrefetchScalarGridSpec(
        num_scalar_prefetch=7,
        grid=(n,),
        in_specs=[
            pl.BlockSpec((MOE_TM, TOP_K), lambda i, *_: (i, 0)),
            pl.BlockSpec((MOE_TM, D_MODEL), lambda i, *_: (jnp.minimum(i, n_a - 1), 0)),
            pl.BlockSpec((MOE_TM, D_MODEL), lambda i, *_: (jnp.maximum(i - n_a, 0), 0)),
            pl.BlockSpec((1, 1, LANES), lambda i, *_: (i, 0, 0)),
        ],
        out_specs=[pl.BlockSpec(memory_space=pl.ANY),
                   pl.BlockSpec((MOE_TM, TOP_K), lambda i, *_: (i, 0))],
        scratch_shapes=[pltpu.VMEM((MOE_TM, D_MODEL), BF16),
                        pltpu.VMEM((2, MOE_L, D_MODEL), BF16), pltpu.VMEM((MOE_BM, D_MODEL), BF16),
                        pltpu.SemaphoreType.DMA((2,)), pltpu.SemaphoreType.DMA((1,))],
    )
    return pl.pallas_call(
        functools.partial(_dispatch_kernel, n, n_a, n_blocks),
        grid_spec=grid_spec,
        out_shape=[jax.ShapeDtypeStruct((n_blocks * MOE_BM, D_MODEL), BF16),
                   jax.ShapeDtypeStruct((ids.shape[0], TOP_K), jnp.int32)],
        compiler_params=pltpu.CompilerParams(
            dimension_semantics=("arbitrary",), vmem_limit_bytes=VMEM_LIMIT),
        name="stage5a_dispatch",
    )(plan['off'], plan['loc'], plan['nch'], plan['tot'], plan['tail_off'], plan['tail_n'],
      plan['n_used'], ids, hn_a, hn_b, plan['loc_lanes'])


def _expert_kernel(be_ref, nu_ref, xs_ref, wg_ref, wu_ref, wd_ref, y_ref):
    b = pl.program_id(0)

    @pl.when(b < nu_ref[0])
    def _():
        x = xs_ref[...]
        g = _dot(x, wg_ref[0])
        u = _dot(x, wu_ref[0])
        y_ref[...] = _dot((_silu(g) * u).astype(BF16), wd_ref[0]).astype(BF16)

    @pl.when(b >= nu_ref[0])
    def _():
        y_ref[...] = jnp.zeros_like(y_ref)


def _experts(plan, xs, wg, wu, wd):
    n_blocks = plan['n_blocks']
    used = lambda b, nu: jnp.minimum(b, nu[0] - 1)
    grid_spec = pltpu.PrefetchScalarGridSpec(
        num_scalar_prefetch=2,
        grid=(n_blocks,),
        in_specs=[
            pl.BlockSpec((MOE_BM, D_MODEL), lambda b, be, nu: (used(b, nu), 0)),
            pl.BlockSpec((1, D_MODEL, D_EXPERT), lambda b, be, nu: (be[b], 0, 0)),
            pl.BlockSpec((1, D_MODEL, D_EXPERT), lambda b, be, nu: (be[b], 0, 0)),
            pl.BlockSpec((1, D_EXPERT, D_MODEL), lambda b, be, nu: (be[b], 0, 0)),
        ],
        out_specs=pl.BlockSpec((MOE_BM, D_MODEL), lambda b, be, nu: (b, 0)),
    )
    return pl.pallas_call(
        _expert_kernel,
        grid_spec=grid_spec,
        out_shape=jax.ShapeDtypeStruct((n_blocks * MOE_BM, D_MODEL), BF16),
        compiler_params=pltpu.CompilerParams(
            dimension_semantics=("arbitrary",), vmem_limit_bytes=VMEM_LIMIT),
        name="stage5b_experts",
    )(plan['block_expert'], plan['n_used'], xs, wg, wu, wd)


def _combine_kernel(n, tile_base, off_ref, loc_ref, nch_ref, tot_ref, pos_ref, wts_ref, h_ref,
                    fw_ref, y_hbm, out_ref, yl, sem):
    i = pl.program_id(0)
    g = tile_base + i
    slot = lax.rem(i, 2)
    tm = h_ref.shape[0]

    def run_copy(tile, sl, e, j):
        return pltpu.make_async_copy(
            y_hbm.at[pl.ds(_chunk_row(off_ref, tile, e, j), MOE_R)],
            yl.at[sl, pl.ds(_chunk_row(loc_ref, tile, e, j), MOE_R)], sem.at[sl])

    @pl.when(i == 0)
    def _():
        yl[...] = jnp.zeros_like(yl)
        _run_chunks(nch_ref, g, lambda e, j: run_copy(g, 0, e, j).start())

    @pl.when(i + 1 < n)
    def _():
        _run_chunks(nch_ref, g + 1, lambda e, j: run_copy(g + 1, 1 - slot, e, j).start())

    def wait_one(k, c):
        pltpu.make_async_copy(y_hbm.at[pl.ds(0, MOE_R)], yl.at[slot, pl.ds(0, MOE_R)],
                              sem.at[slot]).wait()
        return c
    lax.fori_loop(0, tot_ref[g], wait_one, 0)

    lane = lax.broadcasted_iota(jnp.int32, (tm, MOE_L), 1)
    pos = pos_ref[...]
    w = wts_ref[...]
    sel = (jnp.where(lane == pos[:, 0:1], w[:, 0:1], 0.0)
           + jnp.where(lane == pos[:, 1:2], w[:, 1:2], 0.0)).astype(BF16)
    h = h_ref[...] + _dot(sel, yl[slot])
    out_ref[...] = h * lax.rsqrt(jnp.mean(h * h, axis=-1, keepdims=True) + EPS) * fw_ref[...]


def _combine(plan, tile_base, pos, wts, h, fw, y_rows):
    t = h.shape[0]
    n = t // MOE_TM
    grid_spec = pltpu.PrefetchScalarGridSpec(
        num_scalar_prefetch=4,
        grid=(n,),
        in_specs=[
            pl.BlockSpec((MOE_TM, TOP_K), lambda i, *_: (i, 0)),
            pl.BlockSpec((MOE_TM, TOP_K), lambda i, *_: (i, 0)),
            pl.BlockSpec((MOE_TM, D_MODEL), lambda i, *_: (i, 0)),
            pl.BlockSpec((1, D_MODEL), lambda i, *_: (0, 0)),
            pl.BlockSpec(memory_space=pl.ANY),
        ],
        out_specs=pl.BlockSpec((MOE_TM, D_MODEL), lambda i, *_: (i, 0)),
        scratch_shapes=[pltpu.VMEM((2, MOE_L, D_MODEL), BF16), pltpu.SemaphoreType.DMA((2,))],
    )
    return pl.pallas_call(
        functools.partial(_combine_kernel, n, tile_base),
        grid_spec=grid_spec,
        out_shape=jax.ShapeDtypeStruct((t, D_MODEL), F32),
        compiler_params=pltpu.CompilerParams(
            dimension_semantics=("arbitrary",), vmem_limit_bytes=VMEM_LIMIT),
        name="stage6_combine_norm",
    )(plan['off'], plan['loc'], plan['nch'], plan['tot'], pos, wts, h, fw, y_rows)


def kernel(x_prompt, x_sample, norm_mix_w, w_in, a_ln_w, a_ln_b, a_spatial_w, a_spatial_b, a_out_norm_w, conv_w, a_log, dt_bias, gdn_norm_w, w_out, norm_ffn_w, w_router_group, w_router_expert, w_gate, w_up, w_down, norm_final_w):
    s1w = _prep_stage1_weights(norm_mix_w[0], w_in[0], a_ln_w[0], a_ln_b[0], a_spatial_w[0],
                               a_spatial_b[0], a_out_norm_w[0], a_log[0], dt_bias[0])
    woa = w_out[0, :D_A].astype(BF16)
    wob = w_out[0, D_A:].astype(BF16)
    w_r = jnp.concatenate([w_router_group[0], w_router_expert[0]], axis=1)
    w_r = jnp.pad(w_r, ((0, 0), (0, LANES - N_ROUTE)))
    wrh = w_r.astype(BF16)
    wrl = (w_r - wrh.astype(F32)).astype(BF16)
    gnw = gdn_norm_w[0].reshape(1, B_DV)
    fnw = norm_ffn_w[0].reshape(1, D_MODEL)

    per_run = []
    for x in (x_prompt, x_sample):
        b, s, d = x.shape
        x2 = x.reshape(b * s, d)
        ya, qkv, z, gb, gbt = _stage1(x2, *s1w, tm=256)
        w, qg, u, qk, kgt, gl = _stage2(qkv.reshape(b, s, D_CONV), conv_w[0],
                                        gb.reshape(b, s, N_GB), gbt, tc=256)
        o_f, o_b = _stage3(w, qg, u, qk, kgt, gl, rows_blk=512)
        h, hn, ids, wts, cnt = _stage4(x2, ya, o_f.reshape(b * s, D_B), o_b.reshape(b * s, D_B),
                                       z, gnw, woa, wob, fnw, wrh, wrl, tm=MOE_TM)
        per_run.append(dict(shape=x.shape, h=h, hn=hn, ids=ids, wts=wts, cnt=cnt[:, 0, :]))

    plan = _moe_plan(jnp.concatenate([r['cnt'] for r in per_run], axis=0))
    ids_all = jnp.concatenate([r['ids'] for r in per_run], axis=0)
    xs, pos_all = _dispatch(plan, ids_all, per_run[0]['hn'], per_run[1]['hn'])
    y_rows = _experts(plan, xs, w_gate[0].astype(BF16), w_up[0].astype(BF16),
                      w_down[0].astype(BF16))
    fw = norm_final_w.reshape(1, D_MODEL)
    outs = []
    t0 = 0
    for r in per_run:
        t = r['h'].shape[0]
        outs.append(_combine(plan, t0 // MOE_TM, pos_all[t0:t0 + t], r['wts'], r['h'], fw,
                             y_rows).reshape(r['shape']))
        t0 += t
    return tuple(outs)
```

```python
import functools
import math

import jax
import jax.numpy as jnp
from jax import lax
from jax.experimental import pallas as pl
from jax.experimental.pallas import tpu as pltpu

D_MODEL = 1024
D_A = 512
A_GROUPS = 4
A_DG = 128
A_CHUNK = 128
D_B = 512
B_HEADS = 4
B_DK = 128
B_DV = 128
D_QK = 512
DN_CHUNK = 64
CONV_W = 5
N_DIR = 2
N_EXPERT_GROUPS = 4
EXPERTS_PER_GROUP = 8
N_EXPERTS = 32
TOP_K = 2
D_EXPERT = 512
EPS = 1e-6
D_CONV = 2 * D_QK + D_B
N_GB = 2 * N_DIR * B_HEADS

LANES = 128
VMEM_LIMIT = 48 * 1024 * 1024

BF16 = jnp.bfloat16
F32 = jnp.float32


def _dot(a, b):
    return jnp.dot(a, b, preferred_element_type=F32)


def _dot_nt(a, b):
    return lax.dot_general(a, b, (((1,), (1,)), ((), ())), preferred_element_type=F32)


def _gelu_tanh(x):
    c = math.sqrt(2.0 / math.pi)
    return x * (0.5 * (1.0 + jnp.tanh(c * (x + 0.044715 * (x * x * x)))))


def _sigmoid(x):
    return 0.5 * (1.0 + jnp.tanh(0.5 * x))


def _silu(x):
    return x * _sigmoid(x)


def _softplus(x):
    return jnp.maximum(x, 0.0) + jnp.log(1.0 + jnp.exp(-jnp.abs(x)))


def _stage1_kernel(x_ref, nw_ref, wuv_ref, wqkv_ref, wz_ref, wgb_ref, wgbt_ref,
                   lnw_ref, lnb_ref, spw_ref, spbt_ref, onw_ref, alog_ref, dtb_ref, alogt_ref, dtbt_ref,
                   ya_ref, qkv_ref, z_ref, gb_ref, gbt_ref):
    tm = x_ref.shape[0]
    sub = min(tm, 1024)
    na = N_DIR * B_HEADS
    for r0 in range(0, tm, sub):
        sl = slice(r0, r0 + sub)
        x = x_ref[sl, :]
        xn = x * lax.rsqrt(jnp.mean(x * x, axis=-1, keepdims=True) + EPS) * nw_ref[...]
        xb = xn.astype(BF16)

        for grp in range(A_GROUPS):
            cols = slice(grp * A_DG, (grp + 1) * A_DG)
            uv_all = _dot(xb, wuv_ref[:, 2 * grp * A_DG:2 * (grp + 1) * A_DG])
            u_all = uv_all[:, :A_DG]
            v_all = uv_all[:, A_DG:]
            spw = spw_ref[grp]
            for c in range(sub // A_CHUNK):
                rows = slice(c * A_CHUNK, (c + 1) * A_CHUNK)
                u = _gelu_tanh(u_all[rows])
                v = _gelu_tanh(v_all[rows])
                mu = jnp.mean(v, axis=-1, keepdims=True)
                vc = v - mu
                var = jnp.mean(vc * vc, axis=-1, keepdims=True)
                vn = vc * lax.rsqrt(var + EPS) * lnw_ref[:, cols] + lnb_ref[:, cols]
                mixed = _dot(spw, vn.astype(BF16)) + spbt_ref[:, grp:grp + 1]
                gated = u * mixed
                out = gated * lax.rsqrt(jnp.mean(gated * gated, axis=-1, keepdims=True) + EPS)
                ya_ref[r0 + c * A_CHUNK:r0 + (c + 1) * A_CHUNK, cols] = (
                    out * onw_ref[:, cols]).astype(BF16)

        ab = _dot(xb, wgb_ref[...])[:, :N_GB]
        abt = _dot_nt(wgbt_ref[...], xb)
        gb_ref[sl, :na] = -jnp.exp(alog_ref[...]) * _softplus(ab[:, :na] + dtb_ref[...])
        gb_ref[sl, na:] = _sigmoid(ab[:, na:])
        gbt_ref[:na, sl] = -jnp.exp(alogt_ref[...]) * _softplus(abt[:na, :] + dtbt_ref[...])
        gbt_ref[na:, sl] = _sigmoid(abt[na:, :])

        z_ref[sl, :] = _dot(xb, wz_ref[...])
        qkv_ref[sl, :] = _dot(xb, wqkv_ref[...])


def _stage1(x, nw, wuv, wqkv, wz, wgb, wgbt, lnw, lnb, spw, spbt, onw, alog, dtb, alogt, dtbt, tm):
    t = x.shape[0]
    full = lambda shape: pl.BlockSpec(shape, lambda i: (0,) * len(shape))
    return pl.pallas_call(
        _stage1_kernel,
        grid=(t // tm,),
        in_specs=[
            pl.BlockSpec((tm, D_MODEL), lambda i: (i, 0)),
            full(nw.shape), full(wuv.shape), full(wqkv.shape), full(wz.shape),
            full(wgb.shape), full(wgbt.shape), full(lnw.shape), full(lnb.shape),
            full(spw.shape), full(spbt.shape), full(onw.shape), full(alog.shape), full(dtb.shape),
            full(alogt.shape), full(dtbt.shape),
        ],
        out_specs=[
            pl.BlockSpec((tm, D_A), lambda i: (i, 0)),
            pl.BlockSpec((tm, D_CONV), lambda i: (i, 0)),
            pl.BlockSpec((tm, D_B), lambda i: (i, 0)),
            pl.BlockSpec((tm, N_GB), lambda i: (i, 0)),
            pl.BlockSpec((N_GB, tm), lambda i: (0, i)),
        ],
        out_shape=[
            jax.ShapeDtypeStruct((t, D_A), BF16),
            jax.ShapeDtypeStruct((t, D_CONV), F32),
            jax.ShapeDtypeStruct((t, D_B), F32),
            jax.ShapeDtypeStruct((t, N_GB), F32),
            jax.ShapeDtypeStruct((N_GB, t), F32),
        ],
        compiler_params=pltpu.CompilerParams(
            dimension_semantics=("arbitrary",), vmem_limit_bytes=VMEM_LIMIT),
        name="stage1_inproj_gmlp",
    )(x, nw, wuv, wqkv, wz, wgb, wgbt, lnw, lnb, spw, spbt, onw, alog, dtb, alogt, dtbt)


def _prep_stage1_weights(norm_mix_w, w_in, a_ln_w, a_ln_b, a_spatial_w, a_spatial_b,
                         a_out_norm_w, a_log, dt_bias):
    c0 = 2 * D_A
    c1 = c0 + D_CONV
    c2 = c1 + D_B
    wuv = w_in[:, :c0].reshape(D_MODEL, 2, A_GROUPS, A_DG).transpose(0, 2, 1, 3)
    wuv = wuv.reshape(D_MODEL, c0).astype(BF16)
    wqkv = w_in[:, c0:c1].astype(BF16)
    wz = w_in[:, c1:c2].astype(BF16)
    wgb_raw = w_in[:, c2:]
    wgb = jnp.pad(wgb_raw, ((0, 0), (0, LANES - N_GB))).astype(BF16)
    wgbt = wgb_raw.T.astype(BF16)
    return (norm_mix_w.reshape(1, D_MODEL), wuv, wqkv, wz, wgb, wgbt,
            a_ln_w.reshape(1, D_A), a_ln_b.reshape(1, D_A), a_spatial_w.astype(BF16),
            a_spatial_b.T, a_out_norm_w.reshape(1, D_A),
            a_log.reshape(1, N_DIR * B_HEADS), dt_bias.reshape(1, N_DIR * B_HEADS),
            a_log.reshape(N_DIR * B_HEADS, 1), dt_bias.reshape(N_DIR * B_HEADS, 1))


PAIR = 2 * DN_CHUNK
HALO = 8
assert CONV_W == 5 and CONV_W // 2 <= HALO


def _split3(x):
    hi = x.astype(BF16)
    r1 = x - hi.astype(F32)
    mid = r1.astype(BF16)
    lo = (r1 - mid.astype(F32)).astype(BF16)
    return hi, mid, lo


def _dot_exact_rhs01(x, m01):
    hi, mid, lo = _split3(x)
    return _dot(hi, m01) + _dot(mid, m01) + _dot(lo, m01)


def _dot_exact_lhs01(m01, x):
    hi, mid, lo = _split3(x)
    return _dot(m01, hi) + _dot(m01, mid) + _dot(m01, lo)


def _unit_tri_inverses(a_negs):
    n = a_negs[0].shape[0]
    row = lax.broadcasted_iota(jnp.int32, (n, n), 0)
    col = lax.broadcasted_iota(jnp.int32, (n, n), 1)
    eye = jnp.where(row == col, 1.0, 0.0)
    r = [eye + a for a in a_negs]
    p16 = [a.astype(BF16) for a in a_negs]
    pw = [_dot(x, x) for x in p16]
    levels = int(math.log2(DN_CHUNK)) - 1
    for lvl in range(levels):
        p16 = [x.astype(BF16) for x in pw]
        if lvl < levels - 1:
            both = [_dot(jnp.concatenate([ri.astype(BF16), pi], axis=0), pi)
                    for ri, pi in zip(r, p16)]
            r = [ri + bi[:n] for ri, bi in zip(r, both)]
            pw = [bi[n:] for bi in both]
        else:
            r = [ri + _dot(ri.astype(BF16), pi) for ri, pi in zip(r, p16)]
    return r


def _stage2_kernel(n_i, x_ref, xp_ref, xn_ref, cw_ref, gb_ref, gbt_ref,
                   w_ref, qg_ref, u_ref, qk_ref, kgt_ref, gl_ref, xpad_ref, act_ref):
    i = pl.program_id(1)
    tc = x_ref.shape[1]

    xpad_ref[0:HALO, :] = jnp.where(i > 0, xp_ref[0], 0.0)
    xpad_ref[HALO:HALO + tc, :] = x_ref[0]
    xpad_ref[HALO + tc:, :] = jnp.where(i < n_i - 1, xn_ref[0], 0.0)

    win = PAIR + 2 * HALO
    for cb in range(D_CONV // LANES):
        cols = slice(cb * LANES, (cb + 1) * LANES)
        wj = [cw_ref[j:j + 1, cols] for j in range(CONV_W)]
        for r0 in range(0, tc, PAIR):
            xw = xpad_ref[r0:r0 + win, cols]
            up = pltpu.roll(wj[3] * xw + pltpu.roll(wj[4] * xw, win - 1, axis=0), win - 1, axis=0)
            dn = pltpu.roll(wj[1] * xw + pltpu.roll(wj[0] * xw, 1, axis=0), 1, axis=0)
            acc = wj[2] * xw + up + dn
            act_ref[r0:r0 + PAIR, cols] = _silu(acc[HALO:HALO + PAIR])

    row = lax.broadcasted_iota(jnp.int32, (PAIR, PAIR), 0)
    col = lax.broadcasted_iota(jnp.int32, (PAIR, PAIR), 1)
    same = (row >= DN_CHUNK) == (col >= DN_CHUNK)
    incl = (same & (row >= col), same & (row <= col))
    strict = (same & (row > col), same & (row < col))
    as01 = lambda m: jnp.where(m, 1.0, 0.0).astype(BF16)
    m_incl = tuple(as01(m) for m in incl)
    m_same = as01(same)
    e_chunk = (as01(row < DN_CHUNK), as01(row >= DN_CHUNK))
    na = N_DIR * B_HEADS

    chains, a_negs, rhss = [], [], []
    for p in range(tc // PAIR):
        rows = slice(p * PAIR, (p + 1) * PAIR)
        gbp = gb_ref[0, rows, :]
        gbtp = gbt_ref[:, rows]
        gcol = tuple(_dot_exact_lhs01(m_incl[d], gbp) for d in range(N_DIR))
        grow = tuple(_dot_exact_rhs01(gbtp, m_incl[1 - d]) for d in range(N_DIR))
        tot_row = _dot_exact_rhs01(gbtp, m_same)
        for c in range(2):
            gl_ref[0, 2 * p + c] = jnp.exp(_dot_exact_rhs01(gbtp, e_chunk[c]))

        heads = []
        for h in range(B_HEADS):
            q = act_ref[rows, h * B_DK:(h + 1) * B_DK]
            k = act_ref[rows, D_QK + h * B_DK:D_QK + (h + 1) * B_DK]
            qn = q * lax.rsqrt(jnp.sum(q * q, axis=-1, keepdims=True) + EPS) * (B_DK ** -0.5)
            kn = k * lax.rsqrt(jnp.sum(k * k, axis=-1, keepdims=True) + EPS)
            kt = kn.T
            heads.append((qn, kn, kt, kt.astype(BF16)))
        kks = [_dot(kn.astype(BF16), kt16) for _, kn, _, kt16 in heads]
        qks = [_dot(qn.astype(BF16), kt16) for qn, _, _, kt16 in heads]

        for d, h in [(d, h) for d in range(N_DIR) for h in range(B_HEADS)]:
            chains.append((rows, d, h))
            ci = d * B_HEADS + h
            lanes = slice(h * LANES, (h + 1) * LANES)
            qn, kn, kt, _ = heads[h]
            v = act_ref[rows, 2 * D_QK + h * B_DV:2 * D_QK + (h + 1) * B_DV]
            gc = gcol[d][:, ci:ci + 1]
            gr = grow[d][ci:ci + 1, :]
            beta = gbp[:, na + ci:na + ci + 1]
            decay = jnp.where(incl[d], jnp.exp(gc - gr), 0.0)
            a_negs.append(jnp.where(strict[d], -(kks[h] * beta * decay), 0.0))
            eg = jnp.exp(gc)
            rhss.append(jnp.concatenate([v * beta, kn * (beta * eg)], axis=1).astype(BF16))
            qk_ref[d, 0, rows, lanes] = (qks[h] * decay).astype(BF16)
            qg_ref[d, 0, rows, lanes] = (qn * eg).astype(BF16)
            kgt_ref[d, 0, rows, lanes] = (kt * jnp.exp(tot_row[ci:ci + 1, :] - gr)).astype(BF16)

    tinvs = _unit_tri_inverses(a_negs)
    uws = [_dot(t.astype(BF16), rhs) for t, rhs in zip(tinvs, rhss)]
    for (rows, d, h), uw in zip(chains, uws):
        lanes = slice(h * LANES, (h + 1) * LANES)
        u_ref[d, 0, rows, lanes] = uw[:, :B_DV]
        w_ref[d, 0, rows, lanes] = uw[:, B_DV:].astype(BF16)


def _stage2(qkv, conv_w, gb, gbt, tc):
    b, s, _ = qkv.shape
    n_i = s // tc
    hb = tc // HALO
    dirs = lambda shape, dtype: jax.ShapeDtypeStruct((N_DIR, b, s) + shape, dtype)
    out_block = pl.BlockSpec((N_DIR, 1, tc, D_B), lambda bi, i: (0, bi, i, 0))
    return pl.pallas_call(
        functools.partial(_stage2_kernel, n_i),
        grid=(b, n_i),
        in_specs=[
            pl.BlockSpec((1, tc, D_CONV), lambda bi, i: (bi, i, 0)),
            pl.BlockSpec((1, HALO, D_CONV), lambda bi, i: (bi, jnp.maximum(i * hb - 1, 0), 0)),
            pl.BlockSpec((1, HALO, D_CONV),
                         lambda bi, i: (bi, jnp.minimum((i + 1) * hb, s // HALO - 1), 0)),
            pl.BlockSpec((CONV_W, D_CONV), lambda bi, i: (0, 0)),
            pl.BlockSpec((1, tc, N_GB), lambda bi, i: (bi, i, 0)),
            pl.BlockSpec((N_GB, tc), lambda bi, i: (0, bi * n_i + i)),
        ],
        out_specs=[out_block, out_block, out_block, out_block, out_block,
                   pl.BlockSpec((1, tc // DN_CHUNK, N_GB, LANES), lambda bi, i: (bi, i, 0, 0))],
        out_shape=[dirs((D_B,), BF16), dirs((D_B,), BF16), dirs((D_B,), F32),
                   dirs((D_B,), BF16), dirs((D_B,), BF16),
                   jax.ShapeDtypeStruct((b, s // DN_CHUNK, N_GB, LANES), F32)],
        scratch_shapes=[pltpu.VMEM((tc + 2 * HALO, D_CONV), F32), pltpu.VMEM((tc, D_CONV), F32)],
        compiler_params=pltpu.CompilerParams(
            dimension_semantics=("arbitrary", "arbitrary"), vmem_limit_bytes=VMEM_LIMIT),
        name="stage2_gdn_chunk_prep",
    )(qkv, qkv, qkv, conv_w, gb, gbt)


def _stage3_kernel(wf, qgf, uf, qkf, kgf, wb, qgb, ub, qkb, kgb, glf, glb, of_ref, ob_ref, s_ref):
    i = pl.program_id(1)

    @pl.when(i == 0)
    def _():
        s_ref[...] = jnp.zeros_like(s_ref)

    rows_blk = wf.shape[2]
    npairs = rows_blk // PAIR
    zpad = jnp.zeros((DN_CHUNK, B_DV), BF16)
    per_dir = ((wf, qgf, uf, qkf, kgf, glf, of_ref), (wb, qgb, ub, qkb, kgb, glb, ob_ref))
    chains = [(d, h) for d in range(N_DIR) for h in range(B_HEADS)]
    for step in range(2 * npairs):
        def where(d):
            chunk = step if d == 0 else 2 * npairs - 1 - step
            pair = chunk // 2
            return (chunk, slice(chunk * DN_CHUNK, (chunk + 1) * DN_CHUNK),
                    slice(pair * PAIR, (pair + 1) * PAIR))

        states, m1s, m2s = [], [], []
        for d, h in chains:
            w_r, qg_r = per_dir[d][0], per_dir[d][1]
            _, rows, _ = where(d)
            lanes = slice(h * LANES, (h + 1) * LANES)
            s = s_ref[d, h]
            states.append(s)
            lhs1 = jnp.concatenate([w_r[0, 0, rows, lanes], qg_r[0, 0, rows, lanes]], axis=0)
            m1s.append(_dot(lhs1, s.astype(BF16)))
        for (d, h), m1 in zip(chains, m1s):
            u_r, qk_r, kg_r = per_dir[d][2], per_dir[d][3], per_dir[d][4]
            chunk, rows, prow = where(d)
            lanes = slice(h * LANES, (h + 1) * LANES)
            v_new = (u_r[0, 0, rows, lanes] - m1[:DN_CHUNK]).astype(BF16)
            v_pad = jnp.concatenate([v_new, zpad] if chunk % 2 == 0 else [zpad, v_new], axis=0)
            lhs2 = jnp.concatenate([qk_r[0, 0, rows, lanes], kg_r[0, 0, prow, lanes]], axis=0)
            m2s.append(_dot(lhs2, v_pad))
        for (d, h), s, m1, m2 in zip(chains, states, m1s, m2s):
            gl_r, o_r = per_dir[d][5], per_dir[d][6]
            chunk, rows, _ = where(d)
            lanes = slice(h * LANES, (h + 1) * LANES)
            o_r[0, rows, lanes] = m1[DN_CHUNK:] + m2[:DN_CHUNK]
            ci = d * B_HEADS + h
            s_ref[d, h] = s * gl_r[0, chunk, ci:ci + 1, :] + m2[DN_CHUNK:]


def _stage3(w, qg, u, qk, kgt, gl, rows_blk):
    _, b, s, _ = w.shape
    n_i = s // rows_blk
    cpb = rows_blk // DN_CHUNK
    fwd = pl.BlockSpec((1, 1, rows_blk, D_B), lambda bi, i: (0, bi, i, 0))
    bwd = pl.BlockSpec((1, 1, rows_blk, D_B), lambda bi, i: (1, bi, n_i - 1 - i, 0))
    return pl.pallas_call(
        _stage3_kernel,
        grid=(b, n_i),
        in_specs=[fwd] * 5 + [bwd] * 5 + [
            pl.BlockSpec((1, cpb, N_GB, LANES), lambda bi, i: (bi, i, 0, 0)),
            pl.BlockSpec((1, cpb, N_GB, LANES), lambda bi, i: (bi, n_i - 1 - i, 0, 0)),
        ],
        out_specs=[pl.BlockSpec((1, rows_blk, D_B), lambda bi, i: (bi, i, 0)),
                   pl.BlockSpec((1, rows_blk, D_B), lambda bi, i: (bi, n_i - 1 - i, 0))],
        out_shape=[jax.ShapeDtypeStruct((b, s, D_B), F32), jax.ShapeDtypeStruct((b, s, D_B), F32)],
        scratch_shapes=[pltpu.VMEM((N_DIR, B_HEADS, B_DK, B_DV), F32)],
        compiler_params=pltpu.CompilerParams(
            dimension_semantics=("arbitrary", "arbitrary"), vmem_limit_bytes=VMEM_LIMIT),
        name="stage3_gdn_scan",
    )(w, qg, u, qk, kgt, w, qg, u, qk, kgt, gl, gl)


N_ROUTE = N_EXPERT_GROUPS + N_EXPERTS


def _stage4_kernel(x_ref, ya_ref, of_ref, ob_ref, z_ref, gnw_ref, woa_ref, wob_ref, fnw_ref,
                   wrh_ref, wrl_ref, h_ref, hn_ref, ids_ref, wts_ref, cnt_ref):
    tm = x_ref.shape[0]
    for k in range(tm // MOE_TM):
        cnt_ref[k] = _stage4_rows(slice(k * MOE_TM, (k + 1) * MOE_TM), x_ref, ya_ref, of_ref,
                                  ob_ref, z_ref, gnw_ref, woa_ref, wob_ref, fnw_ref, wrh_ref,
                                  wrl_ref, h_ref, hn_ref, ids_ref, wts_ref)


def _stage4_rows(sl, x_ref, ya_ref, of_ref, ob_ref, z_ref, gnw_ref, woa_ref, wob_ref, fnw_ref,
                 wrh_ref, wrl_ref, h_ref, hn_ref, ids_ref, wts_ref):
    tm = sl.stop - sl.start
    o = of_ref[sl, :] + ob_ref[sl, :]
    z = z_ref[sl, :]
    parts = []
    for hd in range(B_HEADS):
        lanes = slice(hd * B_DV, (hd + 1) * B_DV)
        oh = o[:, lanes]
        yh = oh * lax.rsqrt(jnp.mean(oh * oh, axis=-1, keepdims=True) + EPS) * gnw_ref[...]
        parts.append((yh * _silu(z[:, lanes])).astype(BF16))
    yb = jnp.concatenate(parts, axis=1)
    h = x_ref[sl, :] + (_dot(ya_ref[sl, :], woa_ref[...]) + _dot(yb, wob_ref[...]))
    h_ref[sl, :] = h
    hn = h * lax.rsqrt(jnp.mean(h * h, axis=-1, keepdims=True) + EPS) * fnw_ref[...]

    hi = hn.astype(BF16)
    hn_ref[sl, :] = hi
    lo = (hn - hi.astype(F32)).astype(BF16)
    logits = _dot(hi, wrh_ref[...]) + (_dot(lo, wrh_ref[...]) + _dot(hi, wrl_ref[...]))

    lane = lax.broadcasted_iota(jnp.int32, (tm, LANES), 1)
    neg = -jnp.inf
    is_g = lane < N_EXPERT_GROUPS
    gl = jnp.where(is_g, logits, neg)
    gmax = jnp.max(gl, axis=-1, keepdims=True)
    gidx = jnp.min(jnp.where(gl == gmax, lane, LANES), axis=-1, keepdims=True)
    g_w = 1.0 / jnp.sum(jnp.where(is_g, jnp.exp(gl - gmax), 0.0), axis=-1, keepdims=True)

    e0 = N_EXPERT_GROUPS + gidx * EXPERTS_PER_GROUP
    in_grp = (lane >= e0) & (lane < e0 + EXPERTS_PER_GROUP)
    el = jnp.where(in_grp, logits, neg)
    m1 = jnp.max(el, axis=-1, keepdims=True)
    i1 = jnp.min(jnp.where(el == m1, lane, LANES), axis=-1, keepdims=True)
    el2 = jnp.where(lane == i1, neg, el)
    m2 = jnp.max(el2, axis=-1, keepdims=True)
    i2 = jnp.min(jnp.where(el2 == m2, lane, LANES), axis=-1, keepdims=True)
    e2 = jnp.exp(m2 - m1)
    inv = 1.0 / (1.0 + e2)
    ids_ref[sl, 0:1] = i1 - N_EXPERT_GROUPS
    ids_ref[sl, 1:2] = i2 - N_EXPERT_GROUPS
    wts_ref[sl, 0:1] = g_w * inv
    wts_ref[sl, 1:2] = g_w * (e2 * inv)
    elane = lane + N_EXPERT_GROUPS
    chosen = (elane == i1) | (elane == i2)
    return jnp.sum(jnp.where(chosen, 1.0, 0.0), axis=0, keepdims=True)


def _stage4(x, ya, o_f, o_b, z, gnw, woa, wob, fnw, wrh, wrl, tm):
    t = x.shape[0]
    full = lambda a: pl.BlockSpec(a.shape, lambda i: (0,) * a.ndim)
    tile = lambda n: pl.BlockSpec((tm, n), lambda i: (i, 0))
    return pl.pallas_call(
        _stage4_kernel,
        grid=(t // tm,),
        in_specs=[tile(D_MODEL), tile(D_A), tile(D_B), tile(D_B), tile(D_B),
                  full(gnw), full(woa), full(wob), full(fnw), full(wrh), full(wrl)],
        out_specs=[tile(D_MODEL), tile(D_MODEL), tile(TOP_K), tile(TOP_K),
                   pl.BlockSpec((tm // MOE_TM, 1, LANES), lambda i: (i, 0, 0))],
        out_shape=[jax.ShapeDtypeStruct((t, D_MODEL), F32), jax.ShapeDtypeStruct((t, D_MODEL), BF16),
                   jax.ShapeDtypeStruct((t, TOP_K), jnp.int32),
                   jax.ShapeDtypeStruct((t, TOP_K), F32),
                   jax.ShapeDtypeStruct((t // MOE_TM, 1, LANES), F32)],
        compiler_params=pltpu.CompilerParams(
            dimension_semantics=("arbitrary",), vmem_limit_bytes=VMEM_LIMIT),
        name="stage4_outproj_router",
    )(x, ya, o_f, o_b, z, gnw, woa, wob, fnw, wrh, wrl)


MOE_TM = 512
MOE_R = 16
MOE_BM = 512
MOE_L = TOP_K * MOE_TM + N_EXPERTS * MOE_R


def _moe_plan(cnt):
    n_tiles = cnt.shape[0]
    c = cnt[:, :N_EXPERTS].astype(jnp.int32)
    cpad = (c + MOE_R - 1) // MOE_R * MOE_R
    seg = jnp.sum(cpad, axis=0)
    segpad = (seg + MOE_BM - 1) // MOE_BM * MOE_BM
    pad_end = jnp.cumsum(segpad)
    pad_start = pad_end - segpad
    off = pad_start[None, :] + jnp.cumsum(cpad, axis=0) - cpad
    loc = jnp.cumsum(cpad, axis=1) - cpad
    nch = cpad // MOE_R
    n_blocks = -(-(TOP_K * MOE_TM + N_EXPERTS * (MOE_R - 1)) * n_tiles // MOE_BM) + N_EXPERTS
    n_used = pad_end[-1] // MOE_BM
    blk = jnp.arange(n_blocks, dtype=jnp.int32)
    first_row = jnp.minimum(blk, n_used - 1) * MOE_BM
    block_expert = jnp.sum((pad_end[None, :] <= first_row[:, None]).astype(jnp.int32), axis=1)
    block_expert = jnp.minimum(block_expert, N_EXPERTS - 1)
    loc_lanes = jnp.pad(loc, ((0, 0), (0, LANES - N_EXPERTS))).astype(F32)
    return dict(off=off.reshape(-1), loc=loc.reshape(-1), nch=nch.reshape(-1),
                tot=jnp.sum(nch, axis=1), tail_off=pad_start + seg,
                tail_n=(segpad - seg) // MOE_R, loc_lanes=loc_lanes.reshape(n_tiles, 1, LANES),
                block_expert=block_expert, n_used=n_used.reshape(1), n_blocks=n_blocks)


def _run_chunks(nch_ref, tile, visit):
    def per_expert(e, carry):
        def per_chunk(j, c):
            visit(e, j)
            return c
        lax.fori_loop(0, nch_ref[tile * N_EXPERTS + e], per_chunk, 0)
        return carry
    lax.fori_loop(0, N_EXPERTS, per_expert, 0)


def _chunk_row(base_ref, tile, e, j):
    return pl.multiple_of(base_ref[tile * N_EXPERTS + e] + j * MOE_R, MOE_R)


def _dispatch_kernel(n, n_first, n_blocks, off_ref, loc_ref, nch_ref, tot_ref, toff_ref, tn_ref,
                     nu_ref, ids_ref, hna_ref, hnb_ref, locl_ref, xs_ref, pos_ref,
                     hn_ref, xl, zrows, sem, tsem):
    i = pl.program_id(0)
    g = i
    slot = lax.rem(i, 2)
    tm = ids_ref.shape[0]

    @pl.when(i < n_first)
    def _():
        hn_ref[...] = hna_ref[...]

    @pl.when(i >= n_first)
    def _():
        hn_ref[...] = hnb_ref[...]

    ids = ids_ref[...]
    lane = lax.broadcasted_iota(jnp.int32, (tm, LANES), 1)
    oh0 = jnp.where(lane == ids[:, 0:1], 1.0, 0.0)
    oh1 = jnp.where(lane == ids[:, 1:2], 1.0, 0.0)
    row = lax.broadcasted_iota(jnp.int32, (tm, tm), 0)
    col = lax.broadcasted_iota(jnp.int32, (tm, tm), 1)
    earlier = jnp.where(row > col, 1.0, 0.0).astype(BF16)
    base = _dot(earlier, (oh0 + oh1).astype(BF16)) + locl_ref[0]
    m0 = base * oh0
    m1 = base * oh1
    pos_ref[:, 0:1] = jnp.sum(m0, axis=-1, keepdims=True).astype(jnp.int32)
    pos_ref[:, 1:2] = jnp.sum(m1, axis=-1, keepdims=True).astype(jnp.int32)
    ones = jnp.ones((8, LANES), BF16)
    lane_form = lambda m: sum(_dot_nt(ones, part) for part in _split3(m))[0:1].astype(jnp.int32)
    p0 = lane_form(m0)
    p1 = lane_form(m1)
    srow = lax.broadcasted_iota(jnp.int32, (MOE_L, tm), 0)
    perm = jnp.where((srow == p0) | (srow == p1), 1.0, 0.0).astype(BF16)
    nb = 256
    for cb in range(D_MODEL // nb):
        xl[slot, :, cb * nb:(cb + 1) * nb] = _dot(perm, hn_ref[:, cb * nb:(cb + 1) * nb]).astype(BF16)

    def run_copy(tile, sl, e, j):
        return pltpu.make_async_copy(
            xl.at[sl, pl.ds(_chunk_row(loc_ref, tile, e, j), MOE_R)],
            xs_ref.at[pl.ds(_chunk_row(off_ref, tile, e, j), MOE_R)], sem.at[sl])

    _run_chunks(nch_ref, g, lambda e, j: run_copy(g, slot, e, j).start())

    def wait_tile(tile, sl):
        def body(k, c):
            pltpu.make_async_copy(xl.at[sl, pl.ds(0, MOE_R)], xs_ref.at[pl.ds(0, MOE_R)],
                                  sem.at[sl]).wait()
            return c
        lax.fori_loop(0, tot_ref[tile], body, 0)

    @pl.when(i > 0)
    def _():
        wait_tile(g - 1, 1 - slot)

    @pl.when(i == n - 1)
    def _():
        wait_tile(g, slot)
        zrows[...] = jnp.zeros_like(zrows)

        def tail_copy(e, j):
            row0 = pl.multiple_of(toff_ref[e] + j * MOE_R, MOE_R)
            return pltpu.make_async_copy(zrows.at[pl.ds(0, MOE_R)],
                                         xs_ref.at[pl.ds(row0, MOE_R)], tsem.at[0])

        def block_copy(b):
            row0 = pl.multiple_of(b * MOE_BM, MOE_BM)
            return pltpu.make_async_copy(zrows, xs_ref.at[pl.ds(row0, MOE_BM)], tsem.at[0])

        def fill(act):
            def per_expert(e, carry):
                def per_chunk(j, c):
                    act(tail_copy(e, j))
                    return c
                lax.fori_loop(0, tn_ref[e], per_chunk, 0)
                return carry
            lax.fori_loop(0, N_EXPERTS, per_expert, 0)

            def per_block(b, c):
                act(block_copy(b))
                return c
            lax.fori_loop(nu_ref[0], n_blocks, per_block, 0)

        fill(lambda cp: cp.start())
        fill(lambda cp: cp.wait())


def _dispatch(plan, ids, hn_a, hn_b):
    n_a = hn_a.shape[0] // MOE_TM
    n = ids.shape[0] // MOE_TM
    n_blocks = plan['n_blocks']
    grid_spec = pltpu.PrefetchScalarGridSpec(
        num_scalar_prefetch=7,
        grid=(n,),
        in_specs=[
            pl.BlockSpec((MOE_TM, TOP_K), lambda i, *_: (i, 0)),
            pl.BlockSpec((MOE_TM, D_MODEL), lambda i, *_: (jnp.minimum(i, n_a - 1), 0)),
            pl.BlockSpec((MOE_TM, D_MODEL), lambda i, *_: (jnp.maximum(i - n_a, 0), 0)),
            pl.BlockSpec((1, 1, LANES), lambda i, *_: (i, 0, 0)),
        ],
        out_specs=[pl.BlockSpec(memory_space=pl.ANY),
                   pl.BlockSpec((MOE_TM, TOP_K), lambda i, *_: (i, 0))],
        scratch_shapes=[pltpu.VMEM((MOE_TM, D_MODEL), BF16),
                        pltpu.VMEM((2, MOE_L, D_MODEL), BF16), pltpu.VMEM((MOE_BM, D_MODEL), BF16),
                        pltpu.SemaphoreType.DMA((2,)), pltpu.SemaphoreType.DMA((1,))],
    )
    return pl.pallas_call(
        functools.partial(_dispatch_kernel, n, n_a, n_blocks),
        grid_spec=grid_spec,
        out_shape=[jax.ShapeDtypeStruct((n_blocks * MOE_BM, D_MODEL), BF16),
                   jax.ShapeDtypeStruct((ids.shape[0], TOP_K), jnp.int32)],
        compiler_params=pltpu.CompilerParams(
            dimension_semantics=("arbitrary",), vmem_limit_bytes=VMEM_LIMIT),
        name="stage5a_dispatch",
    )(plan['off'], plan['loc'], plan['nch'], plan['tot'], plan['tail_off'], plan['tail_n'],
      plan['n_used'], ids, hn_a, hn_b, plan['loc_lanes'])


def _expert_kernel(be_ref, nu_ref, xs_ref, wg_ref, wu_ref, wd_ref, y_ref):
    b = pl.program_id(0)

    @pl.when(b < nu_ref[0])
    def _():
        x = xs_ref[...]
        g = _dot(x, wg_ref[0])
        u = _dot(x, wu_ref[0])
        y_ref[...] = _dot((_silu(g) * u).astype(BF16), wd_ref[0]).astype(BF16)

    @pl.when(b >= nu_ref[0])
    def _():
        y_ref[...] = jnp.zeros_like(y_ref)


def _experts(plan, xs, wg, wu, wd):
    n_blocks = plan['n_blocks']
    used = lambda b, nu: jnp.minimum(b, nu[0] - 1)
    grid_spec = pltpu.PrefetchScalarGridSpec(
        num_scalar_prefetch=2,
        grid=(n_blocks,),
        in_specs=[
            pl.BlockSpec((MOE_BM, D_MODEL), lambda b, be, nu: (used(b, nu), 0)),
            pl.BlockSpec((1, D_MODEL, D_EXPERT), lambda b, be, nu: (be[b], 0, 0)),
            pl.BlockSpec((1, D_MODEL, D_EXPERT), lambda b, be, nu: (be[b], 0, 0)),
            pl.BlockSpec((1, D_EXPERT, D_MODEL), lambda b, be, nu: (be[b], 0, 0)),
        ],
        out_specs=pl.BlockSpec((MOE_BM, D_MODEL), lambda b, be, nu: (b, 0)),
    )
    return pl.pallas_call(
        _expert_kernel,
        grid_spec=grid_spec,
        out_shape=jax.ShapeDtypeStruct((n_blocks * MOE_BM, D_MODEL), BF16),
        compiler_params=pltpu.CompilerParams(
            dimension_semantics=("arbitrary",), vmem_limit_bytes=VMEM_LIMIT),
        name="stage5b_experts",
    )(plan['block_expert'], plan['n_used'], xs, wg, wu, wd)


def _combine_kernel(n, tile_base, off_ref, loc_ref, nch_ref, tot_ref, pos_ref, wts_ref, h_ref,
                    fw_ref, y_hbm, out_ref, yl, sem):
    i = pl.program_id(0)
    g = tile_base + i
    slot = lax.rem(i, 2)
    tm = h_ref.shape[0]

    def run_copy(tile, sl, e, j):
        return pltpu.make_async_copy(
            y_hbm.at[pl.ds(_chunk_row(off_ref, tile, e, j), MOE_R)],
            yl.at[sl, pl.ds(_chunk_row(loc_ref, tile, e, j), MOE_R)], sem.at[sl])

    @pl.when(i == 0)
    def _():
        yl[...] = jnp.zeros_like(yl)
        _run_chunks(nch_ref, g, lambda e, j: run_copy(g, 0, e, j).start())

    @pl.when(i + 1 < n)
    def _():
        _run_chunks(nch_ref, g + 1, lambda e, j: run_copy(g + 1, 1 - slot, e, j).start())

    def wait_one(k, c):
        pltpu.make_async_copy(y_hbm.at[pl.ds(0, MOE_R)], yl.at[slot, pl.ds(0, MOE_R)],
                              sem.at[slot]).wait()
        return c
    lax.fori_loop(0, tot_ref[g], wait_one, 0)

    lane = lax.broadcasted_iota(jnp.int32, (tm, MOE_L), 1)
    pos = pos_ref[...]
    w = wts_ref[...]
    sel = (jnp.where(lane == pos[:, 0:1], w[:, 0:1], 0.0)
           + jnp.where(lane == pos[:, 1:2], w[:, 1:2], 0.0)).astype(BF16)
    h = h_ref[...] + _dot(sel, yl[slot])
    out_ref[...] = h * lax.rsqrt(jnp.mean(h * h, axis=-1, keepdims=True) + EPS) * fw_ref[...]


def _combine(plan, tile_base, pos, wts, h, fw, y_rows):
    t = h.shape[0]
    n = t // MOE_TM
    grid_spec = pltpu.PrefetchScalarGridSpec(
        num_scalar_prefetch=4,
        grid=(n,),
        in_specs=[
            pl.BlockSpec((MOE_TM, TOP_K), lambda i, *_: (i, 0)),
            pl.BlockSpec((MOE_TM, TOP_K), lambda i, *_: (i, 0)),
            pl.BlockSpec((MOE_TM, D_MODEL), lambda i, *_: (i, 0)),
            pl.BlockSpec((1, D_MODEL), lambda i, *_: (0, 0)),
            pl.BlockSpec(memory_space=pl.ANY),
        ],
        out_specs=pl.BlockSpec((MOE_TM, D_MODEL), lambda i, *_: (i, 0)),
        scratch_shapes=[pltpu.VMEM((2, MOE_L, D_MODEL), BF16), pltpu.SemaphoreType.DMA((2,))],
    )
    return pl.pallas_call(
        functools.partial(_combine_kernel, n, tile_base),
        grid_spec=grid_spec,
        out_shape=jax.ShapeDtypeStruct((t, D_MODEL), F32),
        compiler_params=pltpu.CompilerParams(
            dimension_semantics=("arbitrary",), vmem_limit_bytes=VMEM_LIMIT),
        name="stage6_combine_norm",
    )(plan['off'], plan['loc'], plan['nch'], plan['tot'], pos, wts, h, fw, y_rows)


def kernel(x_prompt, x_sample, norm_mix_w, w_in, a_ln_w, a_ln_b, a_spatial_w, a_spatial_b, a_out_norm_w, conv_w, a_log, dt_bias, gdn_norm_w, w_out, norm_ffn_w, w_router_group, w_router_expert, w_gate, w_up, w_down, norm_final_w):
    s1w = _prep_stage1_weights(norm_mix_w[0], w_in[0], a_ln_w[0], a_ln_b[0], a_spatial_w[0],
                               a_spatial_b[0], a_out_norm_w[0], a_log[0], dt_bias[0])
    woa = w_out[0, :D_A].astype(BF16)
    wob = w_out[0, D_A:].astype(BF16)
    w_r = jnp.concatenate([w_router_group[0], w_router_expert[0]], axis=1)
    w_r = jnp.pad(w_r, ((0, 0), (0, LANES - N_ROUTE)))
    wrh = w_r.astype(BF16)
    wrl = (w_r - wrh.astype(F32)).astype(BF16)
    gnw = gdn_norm_w[0].reshape(1, B_DV)
    fnw = norm_ffn_w[0].reshape(1, D_MODEL)

    per_run = []
    for x in (x_prompt, x_sample):
        b, s, d = x.shape
        x2 = x.reshape(b * s, d)
        ya, qkv, z, gb, gbt = _stage1(x2, *s1w, tm=1024)
        w, qg, u, qk, kgt, gl = _stage2(qkv.reshape(b, s, D_CONV), conv_w[0],
                                        gb.reshape(b, s, N_GB), gbt, tc=256)
        o_f, o_b = _stage3(w, qg, u, qk, kgt, gl, rows_blk=512)
        h, hn, ids, wts, cnt = _stage4(x2, ya, o_f.reshape(b * s, D_B), o_b.reshape(b * s, D_B),
                                       z, gnw, woa, wob, fnw, wrh, wrl, tm=2 * MOE_TM)
        per_run.append(dict(shape=x.shape, h=h, hn=hn, ids=ids, wts=wts, cnt=cnt[:, 0, :]))

    plan = _moe_plan(jnp.concatenate([r['cnt'] for r in per_run], axis=0))
    ids_all = jnp.concatenate([r['ids'] for r in per_run], axis=0)
    xs, pos_all = _dispatch(plan, ids_all, per_run[0]['hn'], per_run[1]['hn'])
    y_rows = _experts(plan, xs, w_gate[0].astype(BF16), w_up[0].astype(BF16),
                      w_down[0].astype(BF16))
    fw = norm_final_w.reshape(1, D_MODEL)
    outs = []
    t0 = 0
    for r in per_run:
        t = r['h'].shape[0]
        outs.append(_combine(plan, t0 // MOE_TM, pos_all[t0:t0 + t], r['wts'], r['h'], fw,
                             y_rows).reshape(r['shape']))
        t0 += t
    return tuple(outs)
```

```python
import functools
import math

import jax
import jax.numpy as jnp
from jax import lax
from jax.experimental import pallas as pl
from jax.experimental.pallas import tpu as pltpu

D_MODEL = 1024
D_A = 512
A_GROUPS = 4
A_DG = 128
A_CHUNK = 128
D_B = 512
B_HEADS = 4
B_DK = 128
B_DV = 128
D_QK = 512
DN_CHUNK = 64
CONV_W = 5
N_DIR = 2
N_EXPERT_GROUPS = 4
EXPERTS_PER_GROUP = 8
N_EXPERTS = 32
TOP_K = 2
D_EXPERT = 512
EPS = 1e-6
D_CONV = 2 * D_QK + D_B
N_GB = 2 * N_DIR * B_HEADS

LANES = 128
VMEM_LIMIT = 48 * 1024 * 1024

BF16 = jnp.bfloat16
F32 = jnp.float32


def _dot(a, b):
    return jnp.dot(a, b, preferred_element_type=F32)


def _dot_nt(a, b):
    return lax.dot_general(a, b, (((1,), (1,)), ((), ())), preferred_element_type=F32)


def _gelu_tanh(x):
    c = math.sqrt(2.0 / math.pi)
    return x * (0.5 * (1.0 + jnp.tanh(c * (x + 0.044715 * (x * x * x)))))


def _sigmoid(x):
    return 0.5 * (1.0 + jnp.tanh(0.5 * x))


def _silu(x):
    return x * _sigmoid(x)


def _softplus(x):
    return jnp.maximum(x, 0.0) + jnp.log(1.0 + jnp.exp(-jnp.abs(x)))


def _stage1_kernel(x_ref, nw_ref, wuv_ref, wqkv_ref, wz_ref, wgb_ref, wgbt_ref,
                   lnw_ref, lnb_ref, spw_ref, spbt_ref, onw_ref, alog_ref, dtb_ref, alogt_ref, dtbt_ref,
                   ya_ref, qkv_ref, z_ref, gb_ref, gbt_ref):
    tm = x_ref.shape[0]
    sub = min(tm, 1024)
    na = N_DIR * B_HEADS
    for r0 in range(0, tm, sub):
        sl = slice(r0, r0 + sub)
        x = x_ref[sl, :]
        xn = x * lax.rsqrt(jnp.mean(x * x, axis=-1, keepdims=True) + EPS) * nw_ref[...]
        xb = xn.astype(BF16)

        for grp in range(A_GROUPS):
            cols = slice(grp * A_DG, (grp + 1) * A_DG)
            uv_all = _dot(xb, wuv_ref[:, 2 * grp * A_DG:2 * (grp + 1) * A_DG])
            u_all = uv_all[:, :A_DG]
            v_all = uv_all[:, A_DG:]
            spw = spw_ref[grp]
            for c in range(sub // A_CHUNK):
                rows = slice(c * A_CHUNK, (c + 1) * A_CHUNK)
                u = _gelu_tanh(u_all[rows])
                v = _gelu_tanh(v_all[rows])
                mu = jnp.mean(v, axis=-1, keepdims=True)
                vc = v - mu
                var = jnp.mean(vc * vc, axis=-1, keepdims=True)
                vn = vc * lax.rsqrt(var + EPS) * lnw_ref[:, cols] + lnb_ref[:, cols]
                mixed = _dot(spw, vn.astype(BF16)) + spbt_ref[:, grp:grp + 1]
                gated = u * mixed
                out = gated * lax.rsqrt(jnp.mean(gated * gated, axis=-1, keepdims=True) + EPS)
                ya_ref[r0 + c * A_CHUNK:r0 + (c + 1) * A_CHUNK, cols] = (
                    out * onw_ref[:, cols]).astype(BF16)

        ab = _dot(xb, wgb_ref[...])[:, :N_GB]
        abt = _dot_nt(wgbt_ref[...], xb)
        gb_ref[sl, :na] = -jnp.exp(alog_ref[...]) * _softplus(ab[:, :na] + dtb_ref[...])
        gb_ref[sl, na:] = _sigmoid(ab[:, na:])
        gbt_ref[:na, sl] = -jnp.exp(alogt_ref[...]) * _softplus(abt[:na, :] + dtbt_ref[...])
        gbt_ref[na:, sl] = _sigmoid(abt[na:, :])

        z_ref[sl, :] = _dot(xb, wz_ref[...])
        qkv_ref[sl, :] = _dot(xb, wqkv_ref[...])


def _stage1(x, nw, wuv, wqkv, wz, wgb, wgbt, lnw, lnb, spw, spbt, onw, alog, dtb, alogt, dtbt, tm):
    t = x.shape[0]
    full = lambda shape: pl.BlockSpec(shape, lambda i: (0,) * len(shape))
    return pl.pallas_call(
        _stage1_kernel,
        grid=(t // tm,),
        in_specs=[
            pl.BlockSpec((tm, D_MODEL), lambda i: (i, 0)),
            full(nw.shape), full(wuv.shape), full(wqkv.shape), full(wz.shape),
            full(wgb.shape), full(wgbt.shape), full(lnw.shape), full(lnb.shape),
            full(spw.shape), full(spbt.shape), full(onw.shape), full(alog.shape), full(dtb.shape),
            full(alogt.shape), full(dtbt.shape),
        ],
        out_specs=[
            pl.BlockSpec((tm, D_A), lambda i: (i, 0)),
            pl.BlockSpec((tm, D_CONV), lambda i: (i, 0)),
            pl.BlockSpec((tm, D_B), lambda i: (i, 0)),
            pl.BlockSpec((tm, N_GB), lambda i: (i, 0)),
            pl.BlockSpec((N_GB, tm), lambda i: (0, i)),
        ],
        out_shape=[
            jax.ShapeDtypeStruct((t, D_A), BF16),
            jax.ShapeDtypeStruct((t, D_CONV), F32),
            jax.ShapeDtypeStruct((t, D_B), F32),
            jax.ShapeDtypeStruct((t, N_GB), F32),
            jax.ShapeDtypeStruct((N_GB, t), F32),
        ],
        compiler_params=pltpu.CompilerParams(
            dimension_semantics=("arbitrary",), vmem_limit_bytes=VMEM_LIMIT),
        name="stage1_inproj_gmlp",
    )(x, nw, wuv, wqkv, wz, wgb, wgbt, lnw, lnb, spw, spbt, onw, alog, dtb, alogt, dtbt)


def _prep_stage1_weights(norm_mix_w, w_in, a_ln_w, a_ln_b, a_spatial_w, a_spatial_b,
                         a_out_norm_w, a_log, dt_bias):
    c0 = 2 * D_A
    c1 = c0 + D_CONV
    c2 = c1 + D_B
    wuv = w_in[:, :c0].reshape(D_MODEL, 2, A_GROUPS, A_DG).transpose(0, 2, 1, 3)
    wuv = wuv.reshape(D_MODEL, c0).astype(BF16)
    wqkv = w_in[:, c0:c1].astype(BF16)
    wz = w_in[:, c1:c2].astype(BF16)
    wgb_raw = w_in[:, c2:]
    wgb = jnp.pad(wgb_raw, ((0, 0), (0, LANES - N_GB))).astype(BF16)
    wgbt = wgb_raw.T.astype(BF16)
    return (norm_mix_w.reshape(1, D_MODEL), wuv, wqkv, wz, wgb, wgbt,
            a_ln_w.reshape(1, D_A), a_ln_b.reshape(1, D_A), a_spatial_w.astype(BF16),
            a_spatial_b.T, a_out_norm_w.reshape(1, D_A),
            a_log.reshape(1, N_DIR * B_HEADS), dt_bias.reshape(1, N_DIR * B_HEADS),
            a_log.reshape(N_DIR * B_HEADS, 1), dt_bias.reshape(N_DIR * B_HEADS, 1))


PAIR = 2 * DN_CHUNK
HALO = 8
assert CONV_W == 5 and CONV_W // 2 <= HALO


def _split3(x):
    hi = x.astype(BF16)
    r1 = x - hi.astype(F32)
    mid = r1.astype(BF16)
    lo = (r1 - mid.astype(F32)).astype(BF16)
    return hi, mid, lo


def _dot_exact_rhs01(x, m01):
    hi, mid, lo = _split3(x)
    return _dot(hi, m01) + _dot(mid, m01) + _dot(lo, m01)


def _dot_exact_lhs01(m01, x):
    hi, mid, lo = _split3(x)
    return _dot(m01, hi) + _dot(m01, mid) + _dot(m01, lo)


def _unit_tri_inverses(a_negs):
    n = a_negs[0].shape[0]
    row = lax.broadcasted_iota(jnp.int32, (n, n), 0)
    col = lax.broadcasted_iota(jnp.int32, (n, n), 1)
    eye = jnp.where(row == col, 1.0, 0.0)
    r = [eye + a for a in a_negs]
    p16 = [a.astype(BF16) for a in a_negs]
    pw = [_dot(x, x) for x in p16]
    levels = int(math.log2(DN_CHUNK)) - 1
    for lvl in range(levels):
        p16 = [x.astype(BF16) for x in pw]
        if lvl < levels - 1:
            both = [_dot(jnp.concatenate([ri.astype(BF16), pi], axis=0), pi)
                    for ri, pi in zip(r, p16)]
            r = [ri + bi[:n] for ri, bi in zip(r, both)]
            pw = [bi[n:] for bi in both]
        else:
            r = [ri + _dot(ri.astype(BF16), pi) for ri, pi in zip(r, p16)]
    return r


def _stage2_kernel(n_i, x_ref, xp_ref, xn_ref, cw_ref, gb_ref, gbt_ref,
                   w_ref, qg_ref, u_ref, qk_ref, kgt_ref, gl_ref, xpad_ref, act_ref):
    i = pl.program_id(1)
    tc = x_ref.shape[1]

    xpad_ref[0:HALO, :] = jnp.where(i > 0, xp_ref[0], 0.0)
    xpad_ref[HALO:HALO + tc, :] = x_ref[0]
    xpad_ref[HALO + tc:, :] = jnp.where(i < n_i - 1, xn_ref[0], 0.0)

    win = PAIR + 2 * HALO
    for cb in range(D_CONV // LANES):
        cols = slice(cb * LANES, (cb + 1) * LANES)
        wj = [cw_ref[j:j + 1, cols] for j in range(CONV_W)]
        for r0 in range(0, tc, PAIR):
            xw = xpad_ref[r0:r0 + win, cols]
            up = pltpu.roll(wj[3] * xw + pltpu.roll(wj[4] * xw, win - 1, axis=0), win - 1, axis=0)
            dn = pltpu.roll(wj[1] * xw + pltpu.roll(wj[0] * xw, 1, axis=0), 1, axis=0)
            acc = wj[2] * xw + up + dn
            act_ref[r0:r0 + PAIR, cols] = _silu(acc[HALO:HALO + PAIR])

    row = lax.broadcasted_iota(jnp.int32, (PAIR, PAIR), 0)
    col = lax.broadcasted_iota(jnp.int32, (PAIR, PAIR), 1)
    same = (row >= DN_CHUNK) == (col >= DN_CHUNK)
    incl = (same & (row >= col), same & (row <= col))
    strict = (same & (row > col), same & (row < col))
    as01 = lambda m: jnp.where(m, 1.0, 0.0).astype(BF16)
    m_incl = tuple(as01(m) for m in incl)
    m_same = as01(same)
    e_chunk = (as01(row < DN_CHUNK), as01(row >= DN_CHUNK))
    na = N_DIR * B_HEADS

    chains, a_negs, rhss = [], [], []
    for p in range(tc // PAIR):
        rows = slice(p * PAIR, (p + 1) * PAIR)
        gbp = gb_ref[0, rows, :]
        gbtp = gbt_ref[:, rows]
        gcol = tuple(_dot_exact_lhs01(m_incl[d], gbp) for d in range(N_DIR))
        grow = tuple(_dot_exact_rhs01(gbtp, m_incl[1 - d]) for d in range(N_DIR))
        tot_row = _dot_exact_rhs01(gbtp, m_same)
        for c in range(2):
            gl_ref[0, 2 * p + c] = jnp.exp(_dot_exact_rhs01(gbtp, e_chunk[c]))

        heads = []
        for h in range(B_HEADS):
            q = act_ref[rows, h * B_DK:(h + 1) * B_DK]
            k = act_ref[rows, D_QK + h * B_DK:D_QK + (h + 1) * B_DK]
            qn = q * lax.rsqrt(jnp.sum(q * q, axis=-1, keepdims=True) + EPS) * (B_DK ** -0.5)
            kn = k * lax.rsqrt(jnp.sum(k * k, axis=-1, keepdims=True) + EPS)
            kt = kn.T
            heads.append((qn, kn, kt, kt.astype(BF16)))
        kks = [_dot(kn.astype(BF16), kt16) for _, kn, _, kt16 in heads]
        qks = [_dot(qn.astype(BF16), kt16) for qn, _, _, kt16 in heads]

        for d, h in [(d, h) for d in range(N_DIR) for h in range(B_HEADS)]:
            chains.append((rows, d, h))
            ci = d * B_HEADS + h
            lanes = slice(h * LANES, (h + 1) * LANES)
            qn, kn, kt, _ = heads[h]
            v = act_ref[rows, 2 * D_QK + h * B_DV:2 * D_QK + (h + 1) * B_DV]
            gc = gcol[d][:, ci:ci + 1]
            gr = grow[d][ci:ci + 1, :]
            beta = gbp[:, na + ci:na + ci + 1]
            decay = jnp.where(incl[d], jnp.exp(gc - gr), 0.0)
            a_negs.append(jnp.where(strict[d], -(kks[h] * beta * decay), 0.0))
            eg = jnp.exp(gc)
            rhss.append(jnp.concatenate([v * beta, kn * (beta * eg)], axis=1).astype(BF16))
            qk_ref[d, 0, rows, lanes] = (qks[h] * decay).astype(BF16)
            qg_ref[d, 0, rows, lanes] = (qn * eg).astype(BF16)
            kgt_ref[d, 0, rows, lanes] = (kt * jnp.exp(tot_row[ci:ci + 1, :] - gr)).astype(BF16)

    tinvs = _unit_tri_inverses(a_negs)
    uws = [_dot(t.astype(BF16), rhs) for t, rhs in zip(tinvs, rhss)]
    for (rows, d, h), uw in zip(chains, uws):
        lanes = slice(h * LANES, (h + 1) * LANES)
        u_ref[d, 0, rows, lanes] = uw[:, :B_DV]
        w_ref[d, 0, rows, lanes] = uw[:, B_DV:].astype(BF16)


def _stage2(qkv, conv_w, gb, gbt, tc):
    b, s, _ = qkv.shape
    n_i = s // tc
    hb = tc // HALO
    dirs = lambda shape, dtype: jax.ShapeDtypeStruct((N_DIR, b, s) + shape, dtype)
    out_block = pl.BlockSpec((N_DIR, 1, tc, D_B), lambda bi, i: (0, bi, i, 0))
    return pl.pallas_call(
        functools.partial(_stage2_kernel, n_i),
        grid=(b, n_i),
        in_specs=[
            pl.BlockSpec((1, tc, D_CONV), lambda bi, i: (bi, i, 0)),
            pl.BlockSpec((1, HALO, D_CONV), lambda bi, i: (bi, jnp.maximum(i * hb - 1, 0), 0)),
            pl.BlockSpec((1, HALO, D_CONV),
                         lambda bi, i: (bi, jnp.minimum((i + 1) * hb, s // HALO - 1), 0)),
            pl.BlockSpec((CONV_W, D_CONV), lambda bi, i: (0, 0)),
            pl.BlockSpec((1, tc, N_GB), lambda bi, i: (bi, i, 0)),
            pl.BlockSpec((N_GB, tc), lambda bi, i: (0, bi * n_i + i)),
        ],
        out_specs=[out_block, out_block, out_block, out_block, out_block,
                   pl.BlockSpec((1, tc // DN_CHUNK, N_GB, LANES), lambda bi, i: (bi, i, 0, 0))],
        out_shape=[dirs((D_B,), BF16), dirs((D_B,), BF16), dirs((D_B,), F32),
                   dirs((D_B,), BF16), dirs((D_B,), BF16),
                   jax.ShapeDtypeStruct((b, s // DN_CHUNK, N_GB, LANES), F32)],
        scratch_shapes=[pltpu.VMEM((tc + 2 * HALO, D_CONV), F32), pltpu.VMEM((tc, D_CONV), F32)],
        compiler_params=pltpu.CompilerParams(
            dimension_semantics=("arbitrary", "arbitrary"), vmem_limit_bytes=VMEM_LIMIT),
        name="stage2_gdn_chunk_prep",
    )(qkv, qkv, qkv, conv_w, gb, gbt)


def _stage3_kernel(wf, qgf, uf, qkf, kgf, wb, qgb, ub, qkb, kgb, glf, glb, of_ref, ob_ref, s_ref):
    i = pl.program_id(1)

    @pl.when(i == 0)
    def _():
        s_ref[...] = jnp.zeros_like(s_ref)

    nseq, rows_blk = wf.shape[1], wf.shape[2]
    npairs = rows_blk // PAIR
    zpad = jnp.zeros((DN_CHUNK, B_DV), BF16)
    per_dir = ((wf, qgf, uf, qkf, kgf, glf, of_ref), (wb, qgb, ub, qkb, kgb, glb, ob_ref))
    chains = [(q, d, h) for q in range(nseq) for d in range(N_DIR) for h in range(B_HEADS)]
    for step in range(2 * npairs):
        def where(d):
            chunk = step if d == 0 else 2 * npairs - 1 - step
            pair = chunk // 2
            return (chunk, slice(chunk * DN_CHUNK, (chunk + 1) * DN_CHUNK),
                    slice(pair * PAIR, (pair + 1) * PAIR))

        states, m1s, m2s = [], [], []
        for q, d, h in chains:
            w_r, qg_r = per_dir[d][0], per_dir[d][1]
            _, rows, _ = where(d)
            lanes = slice(h * LANES, (h + 1) * LANES)
            s = s_ref[q, d, h]
            states.append(s)
            lhs1 = jnp.concatenate([w_r[0, q, rows, lanes], qg_r[0, q, rows, lanes]], axis=0)
            m1s.append(_dot(lhs1, s.astype(BF16)))
        for (q, d, h), m1 in zip(chains, m1s):
            u_r, qk_r, kg_r = per_dir[d][2], per_dir[d][3], per_dir[d][4]
            chunk, rows, prow = where(d)
            lanes = slice(h * LANES, (h + 1) * LANES)
            v_new = (u_r[0, q, rows, lanes] - m1[:DN_CHUNK]).astype(BF16)
            v_pad = jnp.concatenate([v_new, zpad] if chunk % 2 == 0 else [zpad, v_new], axis=0)
            lhs2 = jnp.concatenate([qk_r[0, q, rows, lanes], kg_r[0, q, prow, lanes]], axis=0)
            m2s.append(_dot(lhs2, v_pad))
        for (q, d, h), s, m1, m2 in zip(chains, states, m1s, m2s):
            gl_r, o_r = per_dir[d][5], per_dir[d][6]
            chunk, rows, _ = where(d)
            lanes = slice(h * LANES, (h + 1) * LANES)
            o_r[q, rows, lanes] = m1[DN_CHUNK:] + m2[:DN_CHUNK]
            ci = d * B_HEADS + h
            s_ref[q, d, h] = s * gl_r[q, chunk, ci:ci + 1, :] + m2[DN_CHUNK:]


def _stage3(w, qg, u, qk, kgt, gl, rows_blk):
    _, b, s, _ = w.shape
    n_i = s // rows_blk
    cpb = rows_blk // DN_CHUNK
    nseq = 2 if b % 2 == 0 else 1
    fwd = pl.BlockSpec((1, nseq, rows_blk, D_B), lambda bi, i: (0, bi, i, 0))
    bwd = pl.BlockSpec((1, nseq, rows_blk, D_B), lambda bi, i: (1, bi, n_i - 1 - i, 0))
    return pl.pallas_call(
        _stage3_kernel,
        grid=(b // nseq, n_i),
        in_specs=[fwd] * 5 + [bwd] * 5 + [
            pl.BlockSpec((nseq, cpb, N_GB, LANES), lambda bi, i: (bi, i, 0, 0)),
            pl.BlockSpec((nseq, cpb, N_GB, LANES), lambda bi, i: (bi, n_i - 1 - i, 0, 0)),
        ],
        out_specs=[pl.BlockSpec((nseq, rows_blk, D_B), lambda bi, i: (bi, i, 0)),
                   pl.BlockSpec((nseq, rows_blk, D_B), lambda bi, i: (bi, n_i - 1 - i, 0))],
        out_shape=[jax.ShapeDtypeStruct((b, s, D_B), F32), jax.ShapeDtypeStruct((b, s, D_B), F32)],
        scratch_shapes=[pltpu.VMEM((nseq, N_DIR, B_HEADS, B_DK, B_DV), F32)],
        compiler_params=pltpu.CompilerParams(
            dimension_semantics=("arbitrary", "arbitrary"), vmem_limit_bytes=VMEM_LIMIT),
        name="stage3_gdn_scan",
    )(w, qg, u, qk, kgt, w, qg, u, qk, kgt, gl, gl)


N_ROUTE = N_EXPERT_GROUPS + N_EXPERTS


def _stage4_kernel(x_ref, ya_ref, of_ref, ob_ref, z_ref, gnw_ref, woa_ref, wob_ref, fnw_ref,
                   wr_ref, h_ref, hn_ref, ids_ref, wts_ref, cnt_ref):
    tm = x_ref.shape[0]
    for k in range(tm // MOE_TM):
        cnt_ref[k] = _stage4_rows(slice(k * MOE_TM, (k + 1) * MOE_TM), x_ref, ya_ref, of_ref,
                                  ob_ref, z_ref, gnw_ref, woa_ref, wob_ref, fnw_ref, wr_ref,
                                  h_ref, hn_ref, ids_ref, wts_ref)


def _stage4_rows(sl, x_ref, ya_ref, of_ref, ob_ref, z_ref, gnw_ref, woa_ref, wob_ref, fnw_ref,
                 wr_ref, h_ref, hn_ref, ids_ref, wts_ref):
    tm = sl.stop - sl.start
    o = of_ref[sl, :] + ob_ref[sl, :]
    z = z_ref[sl, :]
    parts = []
    for hd in range(B_HEADS):
        lanes = slice(hd * B_DV, (hd + 1) * B_DV)
        oh = o[:, lanes]
        yh = oh * lax.rsqrt(jnp.mean(oh * oh, axis=-1, keepdims=True) + EPS) * gnw_ref[...]
        parts.append((yh * _silu(z[:, lanes])).astype(BF16))
    yb = jnp.concatenate(parts, axis=1)
    h = x_ref[sl, :] + (_dot(ya_ref[sl, :], woa_ref[...]) + _dot(yb, wob_ref[...]))
    h_ref[sl, :] = h
    hn = h * lax.rsqrt(jnp.mean(h * h, axis=-1, keepdims=True) + EPS) * fnw_ref[...]

    hi = hn.astype(BF16)
    hn_ref[sl, :] = hi
    lo = (hn - hi.astype(F32)).astype(BF16)
    prod = _dot(jnp.concatenate([hi, lo], axis=0), wr_ref[...])
    logits = ((prod[:tm, :LANES] + prod[tm:, :LANES])
              + (prod[:tm, LANES:] + prod[tm:, LANES:]))

    lane = lax.broadcasted_iota(jnp.int32, (tm, LANES), 1)
    neg = -jnp.inf
    is_g = lane < N_EXPERT_GROUPS
    gl = jnp.where(is_g, logits, neg)
    gmax = jnp.max(gl, axis=-1, keepdims=True)
    gidx = jnp.min(jnp.where(gl == gmax, lane, LANES), axis=-1, keepdims=True)
    g_w = 1.0 / jnp.sum(jnp.where(is_g, jnp.exp(gl - gmax), 0.0), axis=-1, keepdims=True)

    e0 = N_EXPERT_GROUPS + gidx * EXPERTS_PER_GROUP
    in_grp = (lane >= e0) & (lane < e0 + EXPERTS_PER_GROUP)
    el = jnp.where(in_grp, logits, neg)
    m1 = jnp.max(el, axis=-1, keepdims=True)
    i1 = jnp.min(jnp.where(el == m1, lane, LANES), axis=-1, keepdims=True)
    el2 = jnp.where(lane == i1, neg, el)
    m2 = jnp.max(el2, axis=-1, keepdims=True)
    i2 = jnp.min(jnp.where(el2 == m2, lane, LANES), axis=-1, keepdims=True)
    e2 = jnp.exp(m2 - m1)
    inv = 1.0 / (1.0 + e2)
    ids_ref[sl, 0:1] = i1 - N_EXPERT_GROUPS
    ids_ref[sl, 1:2] = i2 - N_EXPERT_GROUPS
    wts_ref[sl, 0:1] = g_w * inv
    wts_ref[sl, 1:2] = g_w * (e2 * inv)
    elane = lane + N_EXPERT_GROUPS
    chosen = (elane == i1) | (elane == i2)
    return jnp.sum(jnp.where(chosen, 1.0, 0.0), axis=0, keepdims=True)


def _stage4(x, ya, o_f, o_b, z, gnw, woa, wob, fnw, wr, tm):
    t = x.shape[0]
    full = lambda a: pl.BlockSpec(a.shape, lambda i: (0,) * a.ndim)
    tile = lambda n: pl.BlockSpec((tm, n), lambda i: (i, 0))
    return pl.pallas_call(
        _stage4_kernel,
        grid=(t // tm,),
        in_specs=[tile(D_MODEL), tile(D_A), tile(D_B), tile(D_B), tile(D_B),
                  full(gnw), full(woa), full(wob), full(fnw), full(wr)],
        out_specs=[tile(D_MODEL), tile(D_MODEL), tile(TOP_K), tile(TOP_K),
                   pl.BlockSpec((tm // MOE_TM, 1, LANES), lambda i: (i, 0, 0))],
        out_shape=[jax.ShapeDtypeStruct((t, D_MODEL), F32), jax.ShapeDtypeStruct((t, D_MODEL), BF16),
                   jax.ShapeDtypeStruct((t, TOP_K), jnp.int32),
                   jax.ShapeDtypeStruct((t, TOP_K), F32),
                   jax.ShapeDtypeStruct((t // MOE_TM, 1, LANES), F32)],
        compiler_params=pltpu.CompilerParams(
            dimension_semantics=("arbitrary",), vmem_limit_bytes=VMEM_LIMIT),
        name="stage4_outproj_router",
    )(x, ya, o_f, o_b, z, gnw, woa, wob, fnw, wr)


MOE_TM = 512
MOE_R = 16
MOE_BM = 512
MOE_L = TOP_K * MOE_TM + N_EXPERTS * MOE_R


def _moe_plan(cnt):
    n_tiles = cnt.shape[0]
    c = cnt[:, :N_EXPERTS].astype(jnp.int32)
    cpad = (c + MOE_R - 1) // MOE_R * MOE_R
    seg = jnp.sum(cpad, axis=0)
    segpad = (seg + MOE_BM - 1) // MOE_BM * MOE_BM
    pad_end = jnp.cumsum(segpad)
    pad_start = pad_end - segpad
    off = pad_start[None, :] + jnp.cumsum(cpad, axis=0) - cpad
    loc = jnp.cumsum(cpad, axis=1) - cpad
    nch = cpad // MOE_R
    n_blocks = -(-(TOP_K * MOE_TM + N_EXPERTS * (MOE_R - 1)) * n_tiles // MOE_BM) + N_EXPERTS
    n_used = pad_end[-1] // MOE_BM
    blk = jnp.arange(n_blocks, dtype=jnp.int32)
    first_row = jnp.minimum(blk, n_used - 1) * MOE_BM
    block_expert = jnp.sum((pad_end[None, :] <= first_row[:, None]).astype(jnp.int32), axis=1)
    block_expert = jnp.minimum(block_expert, N_EXPERTS - 1)
    loc_lanes = jnp.pad(loc, ((0, 0), (0, LANES - N_EXPERTS))).astype(F32)
    return dict(off=off.reshape(-1), loc=loc.reshape(-1), nch=nch.reshape(-1),
                tot=jnp.sum(nch, axis=1), tail_off=pad_start + seg,
                tail_n=(segpad - seg) // MOE_R, loc_lanes=loc_lanes.reshape(n_tiles, 1, LANES),
                block_expert=block_expert, n_used=n_used.reshape(1), n_blocks=n_blocks)


def _run_chunks(nch_ref, loc_ref, off_ref, tile, visit):
    def per_expert(e, carry):
        idx = tile * N_EXPERTS + e
        loc0 = loc_ref[idx]
        off0 = off_ref[idx]

        def per_chunk(j, c):
            visit(pl.multiple_of(loc0 + j * MOE_R, MOE_R), pl.multiple_of(off0 + j * MOE_R, MOE_R))
            return c
        lax.fori_loop(0, nch_ref[idx], per_chunk, 0)
        return carry
    lax.fori_loop(0, N_EXPERTS, per_expert, 0)


def _wait_chunks(count, chunk_wait):
    for bit in range((MOE_L // MOE_R).bit_length()):
        @pl.when(((count >> bit) & 1) == 1)
        def _():
            chunk_wait((1 << bit) * MOE_R)


def _dispatch_kernel(n, n_first, n_blocks, off_ref, loc_ref, nch_ref, tot_ref, toff_ref, tn_ref,
                     nu_ref, ids_ref, hna_ref, hnb_ref, locl_ref, xs_ref, pos_ref,
                     hn_ref, xl, zrows, sem, tsem):
    i = pl.program_id(0)
    g = i
    slot = lax.rem(i, 2)
    tm = ids_ref.shape[0]

    @pl.when(i < n_first)
    def _():
        hn_ref[...] = hna_ref[...]

    @pl.when(i >= n_first)
    def _():
        hn_ref[...] = hnb_ref[...]

    ids = ids_ref[...]
    lane = lax.broadcasted_iota(jnp.int32, (tm, LANES), 1)
    oh0 = jnp.where(lane == ids[:, 0:1], 1.0, 0.0)
    oh1 = jnp.where(lane == ids[:, 1:2], 1.0, 0.0)
    row = lax.broadcasted_iota(jnp.int32, (tm, tm), 0)
    col = lax.broadcasted_iota(jnp.int32, (tm, tm), 1)
    earlier = jnp.where(row > col, 1.0, 0.0).astype(BF16)
    base = _dot(earlier, (oh0 + oh1).astype(BF16)) + locl_ref[0]
    m0 = base * oh0
    m1 = base * oh1
    pos_ref[:, 0:1] = jnp.sum(m0, axis=-1, keepdims=True).astype(jnp.int32)
    pos_ref[:, 1:2] = jnp.sum(m1, axis=-1, keepdims=True).astype(jnp.int32)
    ones = jnp.ones((8, LANES), BF16)
    lane_form = lambda m: sum(_dot_nt(ones, part) for part in _split3(m))[0:1].astype(jnp.int32)
    p0 = lane_form(m0)
    p1 = lane_form(m1)
    srow = lax.broadcasted_iota(jnp.int32, (MOE_L, tm), 0)
    perm = jnp.where((srow == p0) | (srow == p1), 1.0, 0.0).astype(BF16)
    nb = 256
    for cb in range(D_MODEL // nb):
        xl[slot, :, cb * nb:(cb + 1) * nb] = _dot(perm, hn_ref[:, cb * nb:(cb + 1) * nb]).astype(BF16)

    def rows_copy(sl, lrow, grow, nrows):
        return pltpu.make_async_copy(xl.at[sl, pl.ds(lrow, nrows)],
                                     xs_ref.at[pl.ds(grow, nrows)], sem.at[sl])

    _run_chunks(nch_ref, loc_ref, off_ref, g,
                lambda lrow, grow: rows_copy(slot, lrow, grow, MOE_R).start())

    def wait_tile(tile, sl):
        _wait_chunks(tot_ref[tile], lambda nrows: rows_copy(sl, 0, 0, nrows).wait())

    @pl.when(i > 0)
    def _():
        wait_tile(g - 1, 1 - slot)

    @pl.when(i == n - 1)
    def _():
        wait_tile(g, slot)
        zrows[...] = jnp.zeros_like(zrows)

        def tail_copy(e, j):
            row0 = pl.multiple_of(toff_ref[e] + j * MOE_R, MOE_R)
            return pltpu.make_async_copy(zrows.at[pl.ds(0, MOE_R)],
                                         xs_ref.at[pl.ds(row0, MOE_R)], tsem.at[0])

        def block_copy(b):
            row0 = pl.multiple_of(b * MOE_BM, MOE_BM)
            return pltpu.make_async_copy(zrows, xs_ref.at[pl.ds(row0, MOE_BM)], tsem.at[0])

        def fill(act):
            def per_expert(e, carry):
                def per_chunk(j, c):
                    act(tail_copy(e, j))
                    return c
                lax.fori_loop(0, tn_ref[e], per_chunk, 0)
                return carry
            lax.fori_loop(0, N_EXPERTS, per_expert, 0)

            def per_block(b, c):
                act(block_copy(b))
                return c
            lax.fori_loop(nu_ref[0], n_blocks, per_block, 0)

        fill(lambda cp: cp.start())
        fill(lambda cp: cp.wait())


def _dispatch(plan, ids, hn_a, hn_b):
    n_a = hn_a.shape[0] // MOE_TM
    n = ids.shape[0] // MOE_TM
    n_blocks = plan['n_blocks']
    grid_spec = pltpu.PrefetchScalarGridSpec(
        num_scalar_prefetch=7,
        grid=(n,),
        in_specs=[
            pl.BlockSpec((MOE_TM, TOP_K), lambda i, *_: (i, 0)),
            pl.BlockSpec((MOE_TM, D_MODEL), lambda i, *_: (jnp.minimum(i, n_a - 1), 0)),
            pl.BlockSpec((MOE_TM, D_MODEL), lambda i, *_: (jnp.maximum(i - n_a, 0), 0)),
            pl.BlockSpec((1, 1, LANES), lambda i, *_: (i, 0, 0)),
        ],
        out_specs=[pl.BlockSpec(memory_space=pl.ANY),
                   pl.BlockSpec((MOE_TM, TOP_K), lambda i, *_: (i, 0))],
        scratch_shapes=[pltpu.VMEM((MOE_TM, D_MODEL), BF16),
                        pltpu.VMEM((2, MOE_L, D_MODEL), BF16), pltpu.VMEM((MOE_BM, D_MODEL), BF16),
                        pltpu.SemaphoreType.DMA((2,)), pltpu.SemaphoreType.DMA((1,))],
    )
    return pl.pallas_call(
        functools.partial(_dispatch_kernel, n, n_a, n_blocks),
        grid_spec=grid_spec,
        out_shape=[jax.ShapeDtypeStruct((n_blocks * MOE_BM, D_MODEL), BF16),
                   jax.ShapeDtypeStruct((ids.shape[0], TOP_K), jnp.int32)],
        compiler_params=pltpu.CompilerParams(
            dimension_semantics=("arbitrary",), vmem_limit_bytes=VMEM_LIMIT),
        name="stage5a_dispatch",
    )(plan['off'], plan['loc'], plan['nch'], plan['tot'], plan['tail_off'], plan['tail_n'],
      plan['n_used'], ids, hn_a, hn_b, plan['loc_lanes'])


def _expert_kernel(be_ref, nu_ref, xs_ref, wg_ref, wu_ref, wd_ref, y_ref, wg16, wu16, wd16):
    b = pl.program_id(0)

    @pl.when((b == 0) | (be_ref[b] != be_ref[jnp.maximum(b - 1, 0)]))
    def _():
        wg16[...] = wg_ref[0].astype(BF16)
        wu16[...] = wu_ref[0].astype(BF16)
        wd16[...] = wd_ref[0].astype(BF16)

    @pl.when(b < nu_ref[0])
    def _():
        x = xs_ref[...]
        g = _dot(x, wg16[...])
        u = _dot(x, wu16[...])
        y_ref[...] = _dot((_silu(g) * u).astype(BF16), wd16[...]).astype(BF16)

    @pl.when(b >= nu_ref[0])
    def _():
        y_ref[...] = jnp.zeros_like(y_ref)


def _experts(plan, xs, wg, wu, wd):
    n_blocks = plan['n_blocks']
    used = lambda b, nu: jnp.minimum(b, nu[0] - 1)
    grid_spec = pltpu.PrefetchScalarGridSpec(
        num_scalar_prefetch=2,
        grid=(n_blocks,),
        in_specs=[
            pl.BlockSpec((MOE_BM, D_MODEL), lambda b, be, nu: (used(b, nu), 0)),
            pl.BlockSpec((1, D_MODEL, D_EXPERT), lambda b, be, nu: (be[b], 0, 0)),
            pl.BlockSpec((1, D_MODEL, D_EXPERT), lambda b, be, nu: (be[b], 0, 0)),
            pl.BlockSpec((1, D_EXPERT, D_MODEL), lambda b, be, nu: (be[b], 0, 0)),
        ],
        out_specs=pl.BlockSpec((MOE_BM, D_MODEL), lambda b, be, nu: (b, 0)),
        scratch_shapes=[pltpu.VMEM((D_MODEL, D_EXPERT), BF16), pltpu.VMEM((D_MODEL, D_EXPERT), BF16),
                        pltpu.VMEM((D_EXPERT, D_MODEL), BF16)],
    )
    return pl.pallas_call(
        _expert_kernel,
        grid_spec=grid_spec,
        out_shape=jax.ShapeDtypeStruct((n_blocks * MOE_BM, D_MODEL), BF16),
        compiler_params=pltpu.CompilerParams(
            dimension_semantics=("arbitrary",), vmem_limit_bytes=VMEM_LIMIT),
        name="stage5b_experts",
    )(plan['block_expert'], plan['n_used'], xs, wg, wu, wd)


def _combine_kernel(n, tile_base, off_ref, loc_ref, nch_ref, tot_ref, pos_ref, wts_ref, h_ref,
                    fw_ref, y_hbm, out_ref, yl, sem):
    i = pl.program_id(0)
    g = tile_base + i
    slot = lax.rem(i, 2)
    tm = h_ref.shape[0]

    def rows_copy(sl, lrow, grow, nrows):
        return pltpu.make_async_copy(y_hbm.at[pl.ds(grow, nrows)],
                                     yl.at[sl, pl.ds(lrow, nrows)], sem.at[sl])

    def fetch(tile, sl):
        _run_chunks(nch_ref, loc_ref, off_ref, tile,
                    lambda lrow, grow: rows_copy(sl, lrow, grow, MOE_R).start())

    @pl.when(i == 0)
    def _():
        yl[...] = jnp.zeros_like(yl)
        fetch(g, 0)

    @pl.when(i + 1 < n)
    def _():
        fetch(g + 1, 1 - slot)

    _wait_chunks(tot_ref[g], lambda nrows: rows_copy(slot, 0, 0, nrows).wait())

    lane = lax.broadcasted_iota(jnp.int32, (tm, MOE_L), 1)
    pos = pos_ref[...]
    w = wts_ref[...]
    sel = (jnp.where(lane == pos[:, 0:1], w[:, 0:1], 0.0)
           + jnp.where(lane == pos[:, 1:2], w[:, 1:2], 0.0)).astype(BF16)
    h = h_ref[...] + _dot(sel, yl[slot])
    out_ref[...] = h * lax.rsqrt(jnp.mean(h * h, axis=-1, keepdims=True) + EPS) * fw_ref[...]


def _combine(plan, tile_base, pos, wts, h, fw, y_rows):
    t = h.shape[0]
    n = t // MOE_TM
    grid_spec = pltpu.PrefetchScalarGridSpec(
        num_scalar_prefetch=4,
        grid=(n,),
        in_specs=[
            pl.BlockSpec((MOE_TM, TOP_K), lambda i, *_: (i, 0)),
            pl.BlockSpec((MOE_TM, TOP_K), lambda i, *_: (i, 0)),
            pl.BlockSpec((MOE_TM, D_MODEL), lambda i, *_: (i, 0)),
            pl.BlockSpec((1, D_MODEL), lambda i, *_: (0, 0)),
            pl.BlockSpec(memory_space=pl.ANY),
        ],
        out_specs=pl.BlockSpec((MOE_TM, D_MODEL), lambda i, *_: (i, 0)),
        scratch_shapes=[pltpu.VMEM((2, MOE_L, D_MODEL), BF16), pltpu.SemaphoreType.DMA((2,))],
    )
    return pl.pallas_call(
        functools.partial(_combine_kernel, n, tile_base),
        grid_spec=grid_spec,
        out_shape=jax.ShapeDtypeStruct((t, D_MODEL), F32),
        compiler_params=pltpu.CompilerParams(
            dimension_semantics=("arbitrary",), vmem_limit_bytes=VMEM_LIMIT),
        name="stage6_combine_norm",
    )(plan['off'], plan['loc'], plan['nch'], plan['tot'], pos, wts, h, fw, y_rows)


def kernel(x_prompt, x_sample, norm_mix_w, w_in, a_ln_w, a_ln_b, a_spatial_w, a_spatial_b, a_out_norm_w, conv_w, a_log, dt_bias, gdn_norm_w, w_out, norm_ffn_w, w_router_group, w_router_expert, w_gate, w_up, w_down, norm_final_w):
    s1w = _prep_stage1_weights(norm_mix_w[0], w_in[0], a_ln_w[0], a_ln_b[0], a_spatial_w[0],
                               a_spatial_b[0], a_out_norm_w[0], a_log[0], dt_bias[0])
    woa = w_out[0, :D_A].astype(BF16)
    wob = w_out[0, D_A:].astype(BF16)
    w_r = jnp.concatenate([w_router_group[0], w_router_expert[0]], axis=1)
    w_r = jnp.pad(w_r, ((0, 0), (0, LANES - N_ROUTE)))
    wrh = w_r.astype(BF16)
    wr = jnp.concatenate([wrh, (w_r - wrh.astype(F32)).astype(BF16)], axis=1)
    gnw = gdn_norm_w[0].reshape(1, B_DV)
    fnw = norm_ffn_w[0].reshape(1, D_MODEL)

    per_run = []
    for x in (x_prompt, x_sample):
        b, s, d = x.shape
        x2 = x.reshape(b * s, d)
        ya, qkv, z, gb, gbt = _stage1(x2, *s1w, tm=1024)
        w, qg, u, qk, kgt, gl = _stage2(qkv.reshape(b, s, D_CONV), conv_w[0],
                                        gb.reshape(b, s, N_GB), gbt, tc=256)
        o_f, o_b = _stage3(w, qg, u, qk, kgt, gl, rows_blk=512)
        h, hn, ids, wts, cnt = _stage4(x2, ya, o_f.reshape(b * s, D_B), o_b.reshape(b * s, D_B),
                                       z, gnw, woa, wob, fnw, wr, tm=2 * MOE_TM)
        per_run.append(dict(shape=x.shape, h=h, hn=hn, ids=ids, wts=wts, cnt=cnt[:, 0, :]))

    plan = _moe_plan(jnp.concatenate([r['cnt'] for r in per_run], axis=0))
    ids_all = jnp.concatenate([r['ids'] for r in per_run], axis=0)
    xs, pos_all = _dispatch(plan, ids_all, per_run[0]['hn'], per_run[1]['hn'])
    y_rows = _experts(plan, xs, w_gate[0], w_up[0], w_down[0])
    fw = norm_final_w.reshape(1, D_MODEL)
    outs = []
    t0 = 0
    for r in per_run:
        t = r['h'].shape[0]
        outs.append(_combine(plan, t0 // MOE_TM, pos_all[t0:t0 + t], r['wts'], r['h'], fw,
                             y_rows).reshape(r['shape']))
        t0 += t
    return tuple(outs)
```

```python
import functools
import math

import jax
import jax.numpy as jnp
from jax import lax
from jax.experimental import pallas as pl
from jax.experimental.pallas import tpu as pltpu

D_MODEL = 1024
D_A = 512
A_GROUPS = 4
A_DG = 128
A_CHUNK = 128
D_B = 512
B_HEADS = 4
B_DK = 128
B_DV = 128
D_QK = 512
DN_CHUNK = 64
CONV_W = 5
N_DIR = 2
N_EXPERT_GROUPS = 4
EXPERTS_PER_GROUP = 8
N_EXPERTS = 32
TOP_K = 2
D_EXPERT = 512
EPS = 1e-6
D_CONV = 2 * D_QK + D_B
N_GB = 2 * N_DIR * B_HEADS

LANES = 128
VMEM_LIMIT = 48 * 1024 * 1024

BF16 = jnp.bfloat16
F32 = jnp.float32


def _dot(a, b):
    return jnp.dot(a, b, preferred_element_type=F32)


def _dot_nt(a, b):
    return lax.dot_general(a, b, (((1,), (1,)), ((), ())), preferred_element_type=F32)


def _gelu_tanh(x):
    c = math.sqrt(2.0 / math.pi)
    return x * (0.5 * (1.0 + jnp.tanh(c * (x + 0.044715 * (x * x * x)))))


def _sigmoid(x):
    return 0.5 * (1.0 + jnp.tanh(0.5 * x))


def _silu(x):
    return x * _sigmoid(x)


def _softplus(x):
    return jnp.maximum(x, 0.0) + jnp.log(1.0 + jnp.exp(-jnp.abs(x)))


GDN_TC = 256


def _stage1_kernel(tiles_per_seq, x_ref, xp_ref, xn_ref, nw_ref, wuv_ref, wqkv_ref,
                   wz_ref, wgb_ref, wgbt_ref, lnw_ref, lnb_ref, spw_ref, spbt_ref, onw_ref,
                   alog_ref, dtb_ref, alogt_ref, dtbt_ref, cw_ref,
                   ya_ref, z_ref, w_ref, qg_ref, u_ref, qk_ref, kgt_ref, gl_ref,
                   xpad_ref, act_ref, gb_ref, gbt_ref):
    tm = x_ref.shape[0]
    na = N_DIR * B_HEADS
    seq_pos = lax.rem(pl.program_id(0), tiles_per_seq)

    def normed(x):
        xn = x * lax.rsqrt(jnp.mean(x * x, axis=-1, keepdims=True) + EPS) * nw_ref[...]
        return xn.astype(BF16)

    xb = normed(x_ref[...])
    halo = normed(jnp.concatenate([xp_ref[...], xn_ref[...]], axis=0))
    qkv = _dot(jnp.concatenate([xb, halo], axis=0), wqkv_ref[...])
    xpad_ref[0:HALO, :] = jnp.where(seq_pos > 0, qkv[tm:tm + HALO], 0.0)
    xpad_ref[HALO:HALO + tm, :] = qkv[:tm]
    xpad_ref[HALO + tm:, :] = jnp.where(seq_pos < tiles_per_seq - 1, qkv[tm + HALO:], 0.0)

    ab = _dot(xb, wgb_ref[...])[:, :N_GB]
    abt = _dot_nt(wgbt_ref[...], xb)
    gb_ref[:, :na] = -jnp.exp(alog_ref[...]) * _softplus(ab[:, :na] + dtb_ref[...])
    gb_ref[:, na:] = _sigmoid(ab[:, na:])
    gbt_ref[:na, :] = -jnp.exp(alogt_ref[...]) * _softplus(abt[:na, :] + dtbt_ref[...])
    gbt_ref[na:, :] = _sigmoid(abt[na:, :])

    z_ref[...] = _dot(xb, wz_ref[...])

    def gmlp_group(grp):
        cols = slice(grp * A_DG, (grp + 1) * A_DG)
        uv_all = _dot(xb, wuv_ref[:, 2 * grp * A_DG:2 * (grp + 1) * A_DG])
        u_all = uv_all[:, :A_DG]
        v_all = uv_all[:, A_DG:]
        spw = spw_ref[grp]
        for c in range(tm // A_CHUNK):
            rows = slice(c * A_CHUNK, (c + 1) * A_CHUNK)
            u = _gelu_tanh(u_all[rows])
            v = _gelu_tanh(v_all[rows])
            mu = jnp.mean(v, axis=-1, keepdims=True)
            vc = v - mu
            var = jnp.mean(vc * vc, axis=-1, keepdims=True)
            vn = vc * lax.rsqrt(var + EPS) * lnw_ref[:, cols] + lnb_ref[:, cols]
            mixed = _dot(spw, vn.astype(BF16)) + spbt_ref[:, grp:grp + 1]
            gated = u * mixed
            out = gated * lax.rsqrt(jnp.mean(gated * gated, axis=-1, keepdims=True) + EPS)
            ya_ref[rows, cols] = (out * onw_ref[:, cols]).astype(BF16)

    n_sub = tm // GDN_TC
    for k in range(n_sub):
        _gdn_prep(k * GDN_TC, GDN_TC, xpad_ref, act_ref, cw_ref, gb_ref, gbt_ref,
                  w_ref, qg_ref, u_ref, qk_ref, kgt_ref, gl_ref)
        for grp in range(k * A_GROUPS // n_sub, (k + 1) * A_GROUPS // n_sub):
            gmlp_group(grp)


def _stage1(x, seq_len, nw, wuv, wqkv, wz, wgb, wgbt, lnw, lnb, spw, spbt, onw, alog, dtb,
            alogt, dtbt, conv_w, tm):
    t = x.shape[0]
    hb = tm // HALO
    consts = (nw, wuv, wqkv, wz, wgb, wgbt, lnw, lnb, spw, spbt, onw, alog, dtb, alogt, dtbt,
              conv_w)
    full = lambda a: pl.BlockSpec(a.shape, lambda i: (0,) * a.ndim)
    dirs = pl.BlockSpec((N_DIR, tm, D_B), lambda i: (0, i, 0))
    dir_shape = lambda dtype: jax.ShapeDtypeStruct((N_DIR, t, D_B), dtype)
    return pl.pallas_call(
        functools.partial(_stage1_kernel, seq_len // tm),
        grid=(t // tm,),
        in_specs=[
            pl.BlockSpec((tm, D_MODEL), lambda i: (i, 0)),
            pl.BlockSpec((HALO, D_MODEL), lambda i: (jnp.maximum(i * hb - 1, 0), 0)),
            pl.BlockSpec((HALO, D_MODEL), lambda i: (jnp.minimum((i + 1) * hb, t // HALO - 1), 0)),
        ] + [full(a) for a in consts],
        out_specs=[
            pl.BlockSpec((tm, D_A), lambda i: (i, 0)),
            pl.BlockSpec((tm, D_B), lambda i: (i, 0)),
            dirs, dirs, dirs, dirs, dirs,
            pl.BlockSpec((tm // DN_CHUNK, N_GB, LANES), lambda i: (i, 0, 0)),
        ],
        out_shape=[
            jax.ShapeDtypeStruct((t, D_A), BF16),
            jax.ShapeDtypeStruct((t, D_B), F32),
            dir_shape(BF16), dir_shape(BF16), dir_shape(F32), dir_shape(BF16), dir_shape(BF16),
            jax.ShapeDtypeStruct((t // DN_CHUNK, N_GB, LANES), F32),
        ],
        scratch_shapes=[pltpu.VMEM((tm + 2 * HALO, D_CONV), F32), pltpu.VMEM((tm, D_CONV), F32),
                        pltpu.VMEM((tm, N_GB), F32), pltpu.VMEM((N_GB, tm), F32)],
        compiler_params=pltpu.CompilerParams(
            dimension_semantics=("arbitrary",), vmem_limit_bytes=VMEM_LIMIT),
        name="stage1_inproj_gmlp_gdnprep",
    )(x, x, x, *consts)


def _prep_stage1_weights(norm_mix_w, w_in, a_ln_w, a_ln_b, a_spatial_w, a_spatial_b,
                         a_out_norm_w, a_log, dt_bias):
    c0 = 2 * D_A
    c1 = c0 + D_CONV
    c2 = c1 + D_B
    wuv = w_in[:, :c0].reshape(D_MODEL, 2, A_GROUPS, A_DG).transpose(0, 2, 1, 3)
    wuv = wuv.reshape(D_MODEL, c0).astype(BF16)
    wqkv = w_in[:, c0:c1].astype(BF16)
    wz = w_in[:, c1:c2].astype(BF16)
    wgb_raw = w_in[:, c2:]
    wgb = jnp.pad(wgb_raw, ((0, 0), (0, LANES - N_GB))).astype(BF16)
    wgbt = wgb_raw.T.astype(BF16)
    return (norm_mix_w.reshape(1, D_MODEL), wuv, wqkv, wz, wgb, wgbt,
            a_ln_w.reshape(1, D_A), a_ln_b.reshape(1, D_A), a_spatial_w.astype(BF16),
            a_spatial_b.T, a_out_norm_w.reshape(1, D_A),
            a_log.reshape(1, N_DIR * B_HEADS), dt_bias.reshape(1, N_DIR * B_HEADS),
            a_log.reshape(N_DIR * B_HEADS, 1), dt_bias.reshape(N_DIR * B_HEADS, 1))


PAIR = 2 * DN_CHUNK
HALO = 8
assert CONV_W == 5 and CONV_W // 2 <= HALO


def _split3(x):
    hi = x.astype(BF16)
    r1 = x - hi.astype(F32)
    mid = r1.astype(BF16)
    lo = (r1 - mid.astype(F32)).astype(BF16)
    return hi, mid, lo


def _dot_exact_rhs01(x, m01):
    hi, mid, lo = _split3(x)
    return _dot(hi, m01) + _dot(mid, m01) + _dot(lo, m01)


def _dot_exact_lhs01(m01, x):
    hi, mid, lo = _split3(x)
    return _dot(m01, hi) + _dot(m01, mid) + _dot(m01, lo)


def _unit_tri_inverses(a_negs):
    n = a_negs[0].shape[0]
    row = lax.broadcasted_iota(jnp.int32, (n, n), 0)
    col = lax.broadcasted_iota(jnp.int32, (n, n), 1)
    eye = jnp.where(row == col, 1.0, 0.0)
    r = [eye + a for a in a_negs]
    p16 = [a.astype(BF16) for a in a_negs]
    pw = [_dot(x, x) for x in p16]
    levels = int(math.log2(DN_CHUNK)) - 1
    for lvl in range(levels):
        p16 = [x.astype(BF16) for x in pw]
        if lvl < levels - 1:
            both = [_dot(jnp.concatenate([ri.astype(BF16), pi], axis=0), pi)
                    for ri, pi in zip(r, p16)]
            r = [ri + bi[:n] for ri, bi in zip(r, both)]
            pw = [bi[n:] for bi in both]
        else:
            r = [ri + _dot(ri.astype(BF16), pi) for ri, pi in zip(r, p16)]
    return r


def _gdn_prep(t0, tc, xpad_ref, act_ref, cw_ref, gb_ref, gbt_ref,
              w_ref, qg_ref, u_ref, qk_ref, kgt_ref, gl_ref):
    win = PAIR + 2 * HALO
    for cb in range(D_CONV // LANES):
        cols = slice(cb * LANES, (cb + 1) * LANES)
        wj = [cw_ref[j:j + 1, cols] for j in range(CONV_W)]
        for r0 in range(t0, t0 + tc, PAIR):
            xw = xpad_ref[r0:r0 + win, cols]
            up = pltpu.roll(wj[3] * xw + pltpu.roll(wj[4] * xw, win - 1, axis=0), win - 1, axis=0)
            dn = pltpu.roll(wj[1] * xw + pltpu.roll(wj[0] * xw, 1, axis=0), 1, axis=0)
            acc = wj[2] * xw + up + dn
            act_ref[r0:r0 + PAIR, cols] = _silu(acc[HALO:HALO + PAIR])

    row = lax.broadcasted_iota(jnp.int32, (PAIR, PAIR), 0)
    col = lax.broadcasted_iota(jnp.int32, (PAIR, PAIR), 1)
    same = (row >= DN_CHUNK) == (col >= DN_CHUNK)
    incl = (same & (row >= col), same & (row <= col))
    strict = (same & (row > col), same & (row < col))
    as01 = lambda m: jnp.where(m, 1.0, 0.0).astype(BF16)
    m_incl = tuple(as01(m) for m in incl)
    m_same = as01(same)
    e_chunk = (as01(row < DN_CHUNK), as01(row >= DN_CHUNK))
    na = N_DIR * B_HEADS

    chains, a_negs, rhss = [], [], []
    for p in range(t0 // PAIR, (t0 + tc) // PAIR):
        rows = slice(p * PAIR, (p + 1) * PAIR)
        gbp = gb_ref[rows, :]
        gbtp = gbt_ref[:, rows]
        gcol = tuple(_dot_exact_lhs01(m_incl[d], gbp) for d in range(N_DIR))
        grow = tuple(_dot_exact_rhs01(gbtp, m_incl[1 - d]) for d in range(N_DIR))
        tot_row = _dot_exact_rhs01(gbtp, m_same)
        for c in range(2):
            gl_ref[2 * p + c] = jnp.exp(_dot_exact_rhs01(gbtp, e_chunk[c]))

        heads = []
        for h in range(B_HEADS):
            q = act_ref[rows, h * B_DK:(h + 1) * B_DK]
            k = act_ref[rows, D_QK + h * B_DK:D_QK + (h + 1) * B_DK]
            qn = q * lax.rsqrt(jnp.sum(q * q, axis=-1, keepdims=True) + EPS) * (B_DK ** -0.5)
            kn = k * lax.rsqrt(jnp.sum(k * k, axis=-1, keepdims=True) + EPS)
            kt = kn.T
            heads.append((qn, kn, kt, kt.astype(BF16)))
        kks = [_dot(kn.astype(BF16), kt16) for _, kn, _, kt16 in heads]
        qks = [_dot(qn.astype(BF16), kt16) for qn, _, _, kt16 in heads]

        for d, h in [(d, h) for d in range(N_DIR) for h in range(B_HEADS)]:
            chains.append((rows, d, h))
            ci = d * B_HEADS + h
            lanes = slice(h * LANES, (h + 1) * LANES)
            qn, kn, kt, _ = heads[h]
            v = act_ref[rows, 2 * D_QK + h * B_DV:2 * D_QK + (h + 1) * B_DV]
            gc = gcol[d][:, ci:ci + 1]
            gr = grow[d][ci:ci + 1, :]
            beta = gbp[:, na + ci:na + ci + 1]
            decay = jnp.where(incl[d], jnp.exp(gc - gr), 0.0)
            a_negs.append(jnp.where(strict[d], -(kks[h] * beta * decay), 0.0))
            eg = jnp.exp(gc)
            rhss.append(jnp.concatenate([v * beta, kn * (beta * eg)], axis=1).astype(BF16))
            qk_ref[d, rows, lanes] = (qks[h] * decay).astype(BF16)
            qg_ref[d, rows, lanes] = (qn * eg).astype(BF16)
            kgt_ref[d, rows, lanes] = (kt * jnp.exp(tot_row[ci:ci + 1, :] - gr)).astype(BF16)

    tinvs = _unit_tri_inverses(a_negs)
    uws = [_dot(t.astype(BF16), rhs) for t, rhs in zip(tinvs, rhss)]
    for (rows, d, h), uw in zip(chains, uws):
        lanes = slice(h * LANES, (h + 1) * LANES)
        u_ref[d, rows, lanes] = uw[:, :B_DV]
        w_ref[d, rows, lanes] = uw[:, B_DV:].astype(BF16)


def _stage3_kernel(wf, qgf, uf, qkf, kgf, wb, qgb, ub, qkb, kgb, glf, glb, of_ref, ob_ref, s_ref):
    i = pl.program_id(1)

    @pl.when(i == 0)
    def _():
        s_ref[...] = jnp.zeros_like(s_ref)

    nseq, rows_blk = wf.shape[1], wf.shape[2]
    npairs = rows_blk // PAIR
    zpad = jnp.zeros((DN_CHUNK, B_DV), BF16)
    per_dir = ((wf, qgf, uf, qkf, kgf, glf, of_ref), (wb, qgb, ub, qkb, kgb, glb, ob_ref))
    chains = [(q, d, h) for q in range(nseq) for d in range(N_DIR) for h in range(B_HEADS)]
    for step in range(2 * npairs):
        def where(d):
            chunk = step if d == 0 else 2 * npairs - 1 - step
            pair = chunk // 2
            return (chunk, slice(chunk * DN_CHUNK, (chunk + 1) * DN_CHUNK),
                    slice(pair * PAIR, (pair + 1) * PAIR))

        states, m1s, m2s = [], [], []
        for q, d, h in chains:
            w_r, qg_r = per_dir[d][0], per_dir[d][1]
            _, rows, _ = where(d)
            lanes = slice(h * LANES, (h + 1) * LANES)
            s = s_ref[q, d, h]
            states.append(s)
            lhs1 = jnp.concatenate([w_r[0, q, rows, lanes], qg_r[0, q, rows, lanes]], axis=0)
            m1s.append(_dot(lhs1, s.astype(BF16)))
        for (q, d, h), m1 in zip(chains, m1s):
            u_r, qk_r, kg_r = per_dir[d][2], per_dir[d][3], per_dir[d][4]
            chunk, rows, prow = where(d)
            lanes = slice(h * LANES, (h + 1) * LANES)
            v_new = (u_r[0, q, rows, lanes] - m1[:DN_CHUNK]).astype(BF16)
            v_pad = jnp.concatenate([v_new, zpad] if chunk % 2 == 0 else [zpad, v_new], axis=0)
            lhs2 = jnp.concatenate([qk_r[0, q, rows, lanes], kg_r[0, q, prow, lanes]], axis=0)
            m2s.append(_dot(lhs2, v_pad))
        for (q, d, h), s, m1, m2 in zip(chains, states, m1s, m2s):
            gl_r, o_r = per_dir[d][5], per_dir[d][6]
            chunk, rows, _ = where(d)
            lanes = slice(h * LANES, (h + 1) * LANES)
            o_r[q, rows, lanes] = m1[DN_CHUNK:] + m2[:DN_CHUNK]
            ci = d * B_HEADS + h
            s_ref[q, d, h] = s * gl_r[q, chunk, ci:ci + 1, :] + m2[DN_CHUNK:]


def _stage3(w, qg, u, qk, kgt, gl, rows_blk):
    _, b, s, _ = w.shape
    n_i = s // rows_blk
    cpb = rows_blk // DN_CHUNK
    nseq = 2 if b % 2 == 0 else 1
    fwd = pl.BlockSpec((1, nseq, rows_blk, D_B), lambda bi, i: (0, bi, i, 0))
    bwd = pl.BlockSpec((1, nseq, rows_blk, D_B), lambda bi, i: (1, bi, n_i - 1 - i, 0))
    return pl.pallas_call(
        _stage3_kernel,
        grid=(b // nseq, n_i),
        in_specs=[fwd] * 5 + [bwd] * 5 + [
            pl.BlockSpec((nseq, cpb, N_GB, LANES), lambda bi, i: (bi, i, 0, 0)),
            pl.BlockSpec((nseq, cpb, N_GB, LANES), lambda bi, i: (bi, n_i - 1 - i, 0, 0)),
        ],
        out_specs=[pl.BlockSpec((nseq, rows_blk, D_B), lambda bi, i: (bi, i, 0)),
                   pl.BlockSpec((nseq, rows_blk, D_B), lambda bi, i: (bi, n_i - 1 - i, 0))],
        out_shape=[jax.ShapeDtypeStruct((b, s, D_B), F32), jax.ShapeDtypeStruct((b, s, D_B), F32)],
        scratch_shapes=[pltpu.VMEM((nseq, N_DIR, B_HEADS, B_DK, B_DV), F32)],
        compiler_params=pltpu.CompilerParams(
            dimension_semantics=("arbitrary", "arbitrary"), vmem_limit_bytes=VMEM_LIMIT),
        name="stage3_gdn_scan",
    )(w, qg, u, qk, kgt, w, qg, u, qk, kgt, gl, gl)


N_ROUTE = N_EXPERT_GROUPS + N_EXPERTS


def _stage4_kernel(x_ref, ya_ref, of_ref, ob_ref, z_ref, gnw_ref, woa_ref, wob_ref, fnw_ref,
                   wr_ref, h_ref, hn_ref, ids_ref, wts_ref, cnt_ref):
    tm = x_ref.shape[0]
    for k in range(tm // MOE_TM):
        cnt_ref[k] = _stage4_rows(slice(k * MOE_TM, (k + 1) * MOE_TM), x_ref, ya_ref, of_ref,
                                  ob_ref, z_ref, gnw_ref, woa_ref, wob_ref, fnw_ref, wr_ref,
                                  h_ref, hn_ref, ids_ref, wts_ref)


def _stage4_rows(sl, x_ref, ya_ref, of_ref, ob_ref, z_ref, gnw_ref, woa_ref, wob_ref, fnw_ref,
                 wr_ref, h_ref, hn_ref, ids_ref, wts_ref):
    tm = sl.stop - sl.start
    o = of_ref[sl, :] + ob_ref[sl, :]
    z = z_ref[sl, :]
    parts = []
    for hd in range(B_HEADS):
        lanes = slice(hd * B_DV, (hd + 1) * B_DV)
        oh = o[:, lanes]
        yh = oh * lax.rsqrt(jnp.mean(oh * oh, axis=-1, keepdims=True) + EPS) * gnw_ref[...]
        parts.append((yh * _silu(z[:, lanes])).astype(BF16))
    yb = jnp.concatenate(parts, axis=1)
    h = x_ref[sl, :] + (_dot(ya_ref[sl, :], woa_ref[...]) + _dot(yb, wob_ref[...]))
    h_ref[sl, :] = h
    hn = h * lax.rsqrt(jnp.mean(h * h, axis=-1, keepdims=True) + EPS) * fnw_ref[...]

    hi = hn.astype(BF16)
    hn_ref[sl, :] = hi
    lo = (hn - hi.astype(F32)).astype(BF16)
    prod = _dot(jnp.concatenate([hi, lo], axis=0), wr_ref[...])
    logits = ((prod[:tm, :LANES] + prod[tm:, :LANES])
              + (prod[:tm, LANES:] + prod[tm:, LANES:]))

    lane = lax.broadcasted_iota(jnp.int32, (tm, LANES), 1)
    neg = -jnp.inf
    is_g = lane < N_EXPERT_GROUPS
    gl = jnp.where(is_g, logits, neg)
    gmax = jnp.max(gl, axis=-1, keepdims=True)
    gidx = jnp.min(jnp.where(gl == gmax, lane, LANES), axis=-1, keepdims=True)
    g_w = 1.0 / jnp.sum(jnp.where(is_g, jnp.exp(gl - gmax), 0.0), axis=-1, keepdims=True)

    e0 = N_EXPERT_GROUPS + gidx * EXPERTS_PER_GROUP
    in_grp = (lane >= e0) & (lane < e0 + EXPERTS_PER_GROUP)
    el = jnp.where(in_grp, logits, neg)
    m1 = jnp.max(el, axis=-1, keepdims=True)
    i1 = jnp.min(jnp.where(el == m1, lane, LANES), axis=-1, keepdims=True)
    el2 = jnp.where(lane == i1, neg, el)
    m2 = jnp.max(el2, axis=-1, keepdims=True)
    i2 = jnp.min(jnp.where(el2 == m2, lane, LANES), axis=-1, keepdims=True)
    e2 = jnp.exp(m2 - m1)
    inv = 1.0 / (1.0 + e2)
    ids_ref[sl, 0:1] = i1 - N_EXPERT_GROUPS
    ids_ref[sl, 1:2] = i2 - N_EXPERT_GROUPS
    wts_ref[sl, 0:1] = g_w * inv
    wts_ref[sl, 1:2] = g_w * (e2 * inv)
    elane = lane + N_EXPERT_GROUPS
    chosen = (elane == i1) | (elane == i2)
    return jnp.sum(jnp.where(chosen, 1.0, 0.0), axis=0, keepdims=True)


def _stage4(x, ya, o_f, o_b, z, gnw, woa, wob, fnw, wr, tm):
    t = x.shape[0]
    full = lambda a: pl.BlockSpec(a.shape, lambda i: (0,) * a.ndim)
    tile = lambda n: pl.BlockSpec((tm, n), lambda i: (i, 0))
    return pl.pallas_call(
        _stage4_kernel,
        grid=(t // tm,),
        in_specs=[tile(D_MODEL), tile(D_A), tile(D_B), tile(D_B), tile(D_B),
                  full(gnw), full(woa), full(wob), full(fnw), full(wr)],
        out_specs=[tile(D_MODEL), tile(D_MODEL), tile(TOP_K), tile(TOP_K),
                   pl.BlockSpec((tm // MOE_TM, 1, LANES), lambda i: (i, 0, 0))],
        out_shape=[jax.ShapeDtypeStruct((t, D_MODEL), F32), jax.ShapeDtypeStruct((t, D_MODEL), BF16),
                   jax.ShapeDtypeStruct((t, TOP_K), jnp.int32),
                   jax.ShapeDtypeStruct((t, TOP_K), F32),
                   jax.ShapeDtypeStruct((t // MOE_TM, 1, LANES), F32)],
        compiler_params=pltpu.CompilerParams(
            dimension_semantics=("arbitrary",), vmem_limit_bytes=VMEM_LIMIT),
        name="stage4_outproj_router",
    )(x, ya, o_f, o_b, z, gnw, woa, wob, fnw, wr)


MOE_TM = 512
MOE_R = 16
MOE_BM = 512
MOE_L = TOP_K * MOE_TM + N_EXPERTS * MOE_R


def _moe_plan(cnt):
    n_tiles = cnt.shape[0]
    c = cnt[:, :N_EXPERTS].astype(jnp.int32)
    cpad = (c + MOE_R - 1) // MOE_R * MOE_R
    seg = jnp.sum(cpad, axis=0)
    segpad = (seg + MOE_BM - 1) // MOE_BM * MOE_BM
    pad_end = jnp.cumsum(segpad)
    pad_start = pad_end - segpad
    off = pad_start[None, :] + jnp.cumsum(cpad, axis=0) - cpad
    loc = jnp.cumsum(cpad, axis=1) - cpad
    nch = cpad // MOE_R
    n_blocks = -(-(TOP_K * MOE_TM + N_EXPERTS * (MOE_R - 1)) * n_tiles // MOE_BM) + N_EXPERTS
    n_used = pad_end[-1] // MOE_BM
    blk = jnp.arange(n_blocks, dtype=jnp.int32)
    first_row = jnp.minimum(blk, n_used - 1) * MOE_BM
    block_expert = jnp.sum((pad_end[None, :] <= first_row[:, None]).astype(jnp.int32), axis=1)
    block_expert = jnp.minimum(block_expert, N_EXPERTS - 1)
    loc_lanes = jnp.pad(loc, ((0, 0), (0, LANES - N_EXPERTS))).astype(F32)
    return dict(off=off.reshape(-1), loc=loc.reshape(-1), nch=nch.reshape(-1),
                tot=jnp.sum(nch, axis=1), tail_off=pad_start + seg,
                tail_n=(segpad - seg) // MOE_R, loc_lanes=loc_lanes.reshape(n_tiles, 1, LANES),
                block_expert=block_expert, n_used=n_used.reshape(1), n_blocks=n_blocks)


def _run_chunks(nch_ref, loc_ref, off_ref, tile, visit):
    def per_expert(e, carry):
        idx = tile * N_EXPERTS + e
        loc0 = loc_ref[idx]
        off0 = off_ref[idx]

        def per_chunk(j, c):
            visit(pl.multiple_of(loc0 + j * MOE_R, MOE_R), pl.multiple_of(off0 + j * MOE_R, MOE_R))
            return c
        lax.fori_loop(0, nch_ref[idx], per_chunk, 0)
        return carry
    lax.fori_loop(0, N_EXPERTS, per_expert, 0)


def _wait_chunks(count, chunk_wait):
    for bit in range((MOE_L // MOE_R).bit_length()):
        @pl.when(((count >> bit) & 1) == 1)
        def _():
            chunk_wait((1 << bit) * MOE_R)


def _dispatch_kernel(n, n_first, n_blocks, off_ref, loc_ref, nch_ref, tot_ref, toff_ref, tn_ref,
                     nu_ref, ids_ref, hna_ref, hnb_ref, locl_ref, xs_ref, pos_ref,
                     hn_ref, xl, zrows, sem, tsem):
    i = pl.program_id(0)
    g = i
    slot = lax.rem(i, 2)
    tm = ids_ref.shape[0]

    @pl.when(i < n_first)
    def _():
        hn_ref[...] = hna_ref[...]

    @pl.when(i >= n_first)
    def _():
        hn_ref[...] = hnb_ref[...]

    ids = ids_ref[...]
    lane = lax.broadcasted_iota(jnp.int32, (tm, LANES), 1)
    oh0 = jnp.where(lane == ids[:, 0:1], 1.0, 0.0)
    oh1 = jnp.where(lane == ids[:, 1:2], 1.0, 0.0)
    row = lax.broadcasted_iota(jnp.int32, (tm, tm), 0)
    col = lax.broadcasted_iota(jnp.int32, (tm, tm), 1)
    earlier = jnp.where(row > col, 1.0, 0.0).astype(BF16)
    base = _dot(earlier, (oh0 + oh1).astype(BF16)) + locl_ref[0]
    m0 = base * oh0
    m1 = base * oh1
    pos_ref[:, 0:1] = jnp.sum(m0, axis=-1, keepdims=True).astype(jnp.int32)
    pos_ref[:, 1:2] = jnp.sum(m1, axis=-1, keepdims=True).astype(jnp.int32)
    ones = jnp.ones((8, LANES), BF16)
    lane_form = lambda m: sum(_dot_nt(ones, part) for part in _split3(m))[0:1].astype(jnp.int32)
    p0 = lane_form(m0)
    p1 = lane_form(m1)
    srow = lax.broadcasted_iota(jnp.int32, (MOE_L, tm), 0)
    perm = jnp.where((srow == p0) | (srow == p1), 1.0, 0.0).astype(BF16)
    nb = 256
    for cb in range(D_MODEL // nb):
        xl[slot, :, cb * nb:(cb + 1) * nb] = _dot(perm, hn_ref[:, cb * nb:(cb + 1) * nb]).astype(BF16)

    def rows_copy(sl, lrow, grow, nrows):
        return pltpu.make_async_copy(xl.at[sl, pl.ds(lrow, nrows)],
                                     xs_ref.at[pl.ds(grow, nrows)], sem.at[sl])

    _run_chunks(nch_ref, loc_ref, off_ref, g,
                lambda lrow, grow: rows_copy(slot, lrow, grow, MOE_R).start())

    def wait_tile(tile, sl):
        _wait_chunks(tot_ref[tile], lambda nrows: rows_copy(sl, 0, 0, nrows).wait())

    @pl.when(i > 0)
    def _():
        wait_tile(g - 1, 1 - slot)

    @pl.when(i == n - 1)
    def _():
        wait_tile(g, slot)
        zrows[...] = jnp.zeros_like(zrows)

        def tail_copy(e, j):
            row0 = pl.multiple_of(toff_ref[e] + j * MOE_R, MOE_R)
            return pltpu.make_async_copy(zrows.at[pl.ds(0, MOE_R)],
                                         xs_ref.at[pl.ds(row0, MOE_R)], tsem.at[0])

        def block_copy(b):
            row0 = pl.multiple_of(b * MOE_BM, MOE_BM)
            return pltpu.make_async_copy(zrows, xs_ref.at[pl.ds(row0, MOE_BM)], tsem.at[0])

        def fill(act):
            def per_expert(e, carry):
                def per_chunk(j, c):
                    act(tail_copy(e, j))
                    return c
                lax.fori_loop(0, tn_ref[e], per_chunk, 0)
                return carry
            lax.fori_loop(0, N_EXPERTS, per_expert, 0)

            def per_block(b, c):
                act(block_copy(b))
                return c
            lax.fori_loop(nu_ref[0], n_blocks, per_block, 0)

        fill(lambda cp: cp.start())
        fill(lambda cp: cp.wait())


def _dispatch(plan, ids, hn_a, hn_b):
    n_a = hn_a.shape[0] // MOE_TM
    n = ids.shape[0] // MOE_TM
    n_blocks = plan['n_blocks']
    grid_spec = pltpu.PrefetchScalarGridSpec(
        num_scalar_prefetch=7,
        grid=(n,),
        in_specs=[
            pl.BlockSpec((MOE_TM, TOP_K), lambda i, *_: (i, 0)),
            pl.BlockSpec((MOE_TM, D_MODEL), lambda i, *_: (jnp.minimum(i, n_a - 1), 0)),
            pl.BlockSpec((MOE_TM, D_MODEL), lambda i, *_: (jnp.maximum(i - n_a, 0), 0)),
            pl.BlockSpec((1, 1, LANES), lambda i, *_: (i, 0, 0)),
        ],
        out_specs=[pl.BlockSpec(memory_space=pl.ANY),
                   pl.BlockSpec((MOE_TM, TOP_K), lambda i, *_: (i, 0))],
        scratch_shapes=[pltpu.VMEM((MOE_TM, D_MODEL), BF16),
                        pltpu.VMEM((2, MOE_L, D_MODEL), BF16), pltpu.VMEM((MOE_BM, D_MODEL), BF16),
                        pltpu.SemaphoreType.DMA((2,)), pltpu.SemaphoreType.DMA((1,))],
    )
    return pl.pallas_call(
        functools.partial(_dispatch_kernel, n, n_a, n_blocks),
        grid_spec=grid_spec,
        out_shape=[jax.ShapeDtypeStruct((n_blocks * MOE_BM, D_MODEL), BF16),
                   jax.ShapeDtypeStruct((ids.shape[0], TOP_K), jnp.int32)],
        compiler_params=pltpu.CompilerParams(
            dimension_semantics=("arbitrary",), vmem_limit_bytes=VMEM_LIMIT),
        name="stage5a_dispatch",
    )(plan['off'], plan['loc'], plan['nch'], plan['tot'], plan['tail_off'], plan['tail_n'],
      plan['n_used'], ids, hn_a, hn_b, plan['loc_lanes'])


def _expert_kernel(be_ref, nu_ref, xs_ref, wg_ref, wu_ref, wd_ref, y_ref, wg16, wu16, wd16):
    b = pl.program_id(0)

    @pl.when((b == 0) | (be_ref[b] != be_ref[jnp.maximum(b - 1, 0)]))
    def _():
        wg16[...] = wg_ref[0].astype(BF16)
        wu16[...] = wu_ref[0].astype(BF16)
        wd16[...] = wd_ref[0].astype(BF16)

    @pl.when(b < nu_ref[0])
    def _():
        x = xs_ref[...]
        g = _dot(x, wg16[...])
        u = _dot(x, wu16[...])
        y_ref[...] = _dot((_silu(g) * u).astype(BF16), wd16[...]).astype(BF16)

    @pl.when(b >= nu_ref[0])
    def _():
        y_ref[...] = jnp.zeros_like(y_ref)


def _experts(plan, xs, wg, wu, wd):
    n_blocks = plan['n_blocks']
    used = lambda b, nu: jnp.minimum(b, nu[0] - 1)
    grid_spec = pltpu.PrefetchScalarGridSpec(
        num_scalar_prefetch=2,
        grid=(n_blocks,),
        in_specs=[
            pl.BlockSpec((MOE_BM, D_MODEL), lambda b, be, nu: (used(b, nu), 0)),
            pl.BlockSpec((1, D_MODEL, D_EXPERT), lambda b, be, nu: (be[b], 0, 0)),
            pl.BlockSpec((1, D_MODEL, D_EXPERT), lambda b, be, nu: (be[b], 0, 0)),
            pl.BlockSpec((1, D_EXPERT, D_MODEL), lambda b, be, nu: (be[b], 0, 0)),
        ],
        out_specs=pl.BlockSpec((MOE_BM, D_MODEL), lambda b, be, nu: (b, 0)),
        scratch_shapes=[pltpu.VMEM((D_MODEL, D_EXPERT), BF16), pltpu.VMEM((D_MODEL, D_EXPERT), BF16),
                        pltpu.VMEM((D_EXPERT, D_MODEL), BF16)],
    )
    return pl.pallas_call(
        _expert_kernel,
        grid_spec=grid_spec,
        out_shape=jax.ShapeDtypeStruct((n_blocks * MOE_BM, D_MODEL), BF16),
        compiler_params=pltpu.CompilerParams(
            dimension_semantics=("arbitrary",), vmem_limit_bytes=VMEM_LIMIT),
        name="stage5b_experts",
    )(plan['block_expert'], plan['n_used'], xs, wg, wu, wd)


def _combine_kernel(n, tile_base, off_ref, loc_ref, nch_ref, tot_ref, pos_ref, wts_ref, h_ref,
                    fw_ref, y_hbm, out_ref, yl, sem):
    i = pl.program_id(0)
    g = tile_base + i
    slot = lax.rem(i, 2)
    tm = h_ref.shape[0]

    def rows_copy(sl, lrow, grow, nrows):
        return pltpu.make_async_copy(y_hbm.at[pl.ds(grow, nrows)],
                                     yl.at[sl, pl.ds(lrow, nrows)], sem.at[sl])

    def fetch(tile, sl):
        _run_chunks(nch_ref, loc_ref, off_ref, tile,
                    lambda lrow, grow: rows_copy(sl, lrow, grow, MOE_R).start())

    @pl.when(i == 0)
    def _():
        yl[...] = jnp.zeros_like(yl)
        fetch(g, 0)

    @pl.when(i + 1 < n)
    def _():
        fetch(g + 1, 1 - slot)

    _wait_chunks(tot_ref[g], lambda nrows: rows_copy(slot, 0, 0, nrows).wait())

    lane = lax.broadcasted_iota(jnp.int32, (tm, MOE_L), 1)
    pos = pos_ref[...]
    w = wts_ref[...]
    sel = (jnp.where(lane == pos[:, 0:1], w[:, 0:1], 0.0)
           + jnp.where(lane == pos[:, 1:2], w[:, 1:2], 0.0)).astype(BF16)
    h = h_ref[...] + _dot(sel, yl[slot])
    out_ref[...] = h * lax.rsqrt(jnp.mean(h * h, axis=-1, keepdims=True) + EPS) * fw_ref[...]


def _combine(plan, tile_base, pos, wts, h, fw, y_rows):
    t = h.shape[0]
    n = t // MOE_TM
    grid_spec = pltpu.PrefetchScalarGridSpec(
        num_scalar_prefetch=4,
        grid=(n,),
        in_specs=[
            pl.BlockSpec((MOE_TM, TOP_K), lambda i, *_: (i, 0)),
            pl.BlockSpec((MOE_TM, TOP_K), lambda i, *_: (i, 0)),
            pl.BlockSpec((MOE_TM, D_MODEL), lambda i, *_: (i, 0)),
            pl.BlockSpec((1, D_MODEL), lambda i, *_: (0, 0)),
            pl.BlockSpec(memory_space=pl.ANY),
        ],
        out_specs=pl.BlockSpec((MOE_TM, D_MODEL), lambda i, *_: (i, 0)),
        scratch_shapes=[pltpu.VMEM((2, MOE_L, D_MODEL), BF16), pltpu.SemaphoreType.DMA((2,))],
    )
    return pl.pallas_call(
        functools.partial(_combine_kernel, n, tile_base),
        grid_spec=grid_spec,
        out_shape=jax.ShapeDtypeStruct((t, D_MODEL), F32),
        compiler_params=pltpu.CompilerParams(
            dimension_semantics=("arbitrary",), vmem_limit_bytes=VMEM_LIMIT),
        name="stage6_combine_norm",
    )(plan['off'], plan['loc'], plan['nch'], plan['tot'], pos, wts, h, fw, y_rows)


def kernel(x_prompt, x_sample, norm_mix_w, w_in, a_ln_w, a_ln_b, a_spatial_w, a_spatial_b, a_out_norm_w, conv_w, a_log, dt_bias, gdn_norm_w, w_out, norm_ffn_w, w_router_group, w_router_expert, w_gate, w_up, w_down, norm_final_w):
    s1w = _prep_stage1_weights(norm_mix_w[0], w_in[0], a_ln_w[0], a_ln_b[0], a_spatial_w[0],
                               a_spatial_b[0], a_out_norm_w[0], a_log[0], dt_bias[0])
    woa = w_out[0, :D_A].astype(BF16)
    wob = w_out[0, D_A:].astype(BF16)
    w_r = jnp.concatenate([w_router_group[0], w_router_expert[0]], axis=1)
    w_r = jnp.pad(w_r, ((0, 0), (0, LANES - N_ROUTE)))
    wrh = w_r.astype(BF16)
    wr = jnp.concatenate([wrh, (w_r - wrh.astype(F32)).astype(BF16)], axis=1)
    gnw = gdn_norm_w[0].reshape(1, B_DV)
    fnw = norm_ffn_w[0].reshape(1, D_MODEL)

    per_run = []
    for x in (x_prompt, x_sample):
        b, s, d = x.shape
        x2 = x.reshape(b * s, d)
        ya, z, w, qg, u, qk, kgt, gl = _stage1(x2, s, *s1w, conv_w[0], tm=512)
        per_seq = lambda a: a.reshape(N_DIR, b, s, D_B)
        o_f, o_b = _stage3(per_seq(w), per_seq(qg), per_seq(u), per_seq(qk), per_seq(kgt),
                           gl.reshape(b, s // DN_CHUNK, N_GB, LANES), rows_blk=512)
        h, hn, ids, wts, cnt = _stage4(x2, ya, o_f.reshape(b * s, D_B), o_b.reshape(b * s, D_B),
                                       z, gnw, woa, wob, fnw, wr, tm=2 * MOE_TM)
        per_run.append(dict(shape=x.shape, h=h, hn=hn, ids=ids, wts=wts, cnt=cnt[:, 0, :]))

    plan = _moe_plan(jnp.concatenate([r['cnt'] for r in per_run], axis=0))
    ids_all = jnp.concatenate([r['ids'] for r in per_run], axis=0)
    xs, pos_all = _dispatch(plan, ids_all, per_run[0]['hn'], per_run[1]['hn'])
    y_rows = _experts(plan, xs, w_gate[0], w_up[0], w_down[0])
    fw = norm_final_w.reshape(1, D_MODEL)
    outs = []
    t0 = 0
    for r in per_run:
        t = r['h'].shape[0]
        outs.append(_combine(plan, t0 // MOE_TM, pos_all[t0:t0 + t], r['wts'], r['h'], fw,
                             y_rows).reshape(r['shape']))
        t0 += t
    return tuple(outs)
```

```python
import functools
import math

import jax
import jax.numpy as jnp
from jax import lax
from jax.experimental import pallas as pl
from jax.experimental.pallas import tpu as pltpu

D_MODEL = 1024
D_A = 512
A_GROUPS = 4
A_DG = 128
A_CHUNK = 128
D_B = 512
B_HEADS = 4
B_DK = 128
B_DV = 128
D_QK = 512
DN_CHUNK = 64
CONV_W = 5
N_DIR = 2
N_EXPERT_GROUPS = 4
EXPERTS_PER_GROUP = 8
N_EXPERTS = 32
TOP_K = 2
D_EXPERT = 512
EPS = 1e-6
D_CONV = 2 * D_QK + D_B
N_GB = 2 * N_DIR * B_HEADS

LANES = 128
VMEM_LIMIT = 48 * 1024 * 1024

BF16 = jnp.bfloat16
F32 = jnp.float32


def _dot(a, b):
    return jnp.dot(a, b, preferred_element_type=F32)


def _dot_nt(a, b):
    return lax.dot_general(a, b, (((1,), (1,)), ((), ())), preferred_element_type=F32)


def _gelu_tanh(x):
    c = math.sqrt(2.0 / math.pi)
    return x * (0.5 * (1.0 + jnp.tanh(c * (x + 0.044715 * (x * x * x)))))


def _sigmoid(x):
    return 0.5 * (1.0 + jnp.tanh(0.5 * x))


def _silu(x):
    return x * _sigmoid(x)


def _softplus(x):
    return jnp.maximum(x, 0.0) + jnp.log(1.0 + jnp.exp(-jnp.abs(x)))


GDN_TC = 256


def _stage1_kernel(tiles_per_seq, x_ref, xp_ref, xn_ref, nw_ref, wuv_ref, wqkv_ref,
                   wz_ref, wgb_ref, wgbt_ref, lnw_ref, lnb_ref, spw_ref, spbt_ref, onw_ref,
                   alog_ref, dtb_ref, alogt_ref, dtbt_ref, cw_ref,
                   ya_ref, z_ref, w_ref, qg_ref, u_ref, qk_ref, kgt_ref, gl_ref,
                   xpad_ref, act_ref, gb_ref, gbt_ref):
    tm = x_ref.shape[0]
    na = N_DIR * B_HEADS
    seq_pos = lax.rem(pl.program_id(0), tiles_per_seq)

    def normed(x):
        xn = x * lax.rsqrt(jnp.mean(x * x, axis=-1, keepdims=True) + EPS) * nw_ref[...]
        return xn.astype(BF16)

    xb = normed(x_ref[...])
    halo = normed(jnp.concatenate([xp_ref[...], xn_ref[...]], axis=0))
    qkv = _dot(jnp.concatenate([xb, halo], axis=0), wqkv_ref[...])
    xpad_ref[0:HALO, :] = jnp.where(seq_pos > 0, qkv[tm:tm + HALO], 0.0)
    xpad_ref[HALO:HALO + tm, :] = qkv[:tm]
    xpad_ref[HALO + tm:, :] = jnp.where(seq_pos < tiles_per_seq - 1, qkv[tm + HALO:], 0.0)

    ab = _dot(xb, wgb_ref[...])[:, :N_GB]
    abt = _dot_nt(wgbt_ref[...], xb)
    gb_ref[:, :na] = -jnp.exp(alog_ref[...]) * _softplus(ab[:, :na] + dtb_ref[...])
    gb_ref[:, na:] = _sigmoid(ab[:, na:])
    gbt_ref[:na, :] = -jnp.exp(alogt_ref[...]) * _softplus(abt[:na, :] + dtbt_ref[...])
    gbt_ref[na:, :] = _sigmoid(abt[na:, :])

    def gmlp_group(grp):
        cols = slice(grp * A_DG, (grp + 1) * A_DG)
        uv_all = _dot(xb, wuv_ref[:, 2 * grp * A_DG:2 * (grp + 1) * A_DG])
        u_all = uv_all[:, :A_DG]
        v_all = uv_all[:, A_DG:]
        spw = spw_ref[grp]
        for c in range(tm // A_CHUNK):
            rows = slice(c * A_CHUNK, (c + 1) * A_CHUNK)
            u = _gelu_tanh(u_all[rows])
            v = _gelu_tanh(v_all[rows])
            mu = jnp.mean(v, axis=-1, keepdims=True)
            vc = v - mu
            var = jnp.mean(vc * vc, axis=-1, keepdims=True)
            vn = vc * lax.rsqrt(var + EPS) * lnw_ref[:, cols] + lnb_ref[:, cols]
            mixed = _dot(spw, vn.astype(BF16)) + spbt_ref[:, grp:grp + 1]
            gated = u * mixed
            out = gated * lax.rsqrt(jnp.mean(gated * gated, axis=-1, keepdims=True) + EPS)
            ya_ref[rows, cols] = (out * onw_ref[:, cols]).astype(BF16)

    n_sub = tm // GDN_TC
    for k in range(n_sub):
        _short_conv_silu(k * GDN_TC, GDN_TC, xpad_ref, act_ref, cw_ref)
        if k == 0:
            z_ref[...] = _dot(xb, wz_ref[...])
        _gdn_prep(k * GDN_TC, GDN_TC, act_ref, gb_ref, gbt_ref,
                  w_ref, qg_ref, u_ref, qk_ref, kgt_ref, gl_ref)
        for grp in range(k * A_GROUPS // n_sub, (k + 1) * A_GROUPS // n_sub):
            gmlp_group(grp)


def _stage1(x, seq_len, nw, wuv, wqkv, wz, wgb, wgbt, lnw, lnb, spw, spbt, onw, alog, dtb,
            alogt, dtbt, conv_w, tm):
    t = x.shape[0]
    hb = tm // HALO
    consts = (nw, wuv, wqkv, wz, wgb, wgbt, lnw, lnb, spw, spbt, onw, alog, dtb, alogt, dtbt,
              conv_w)
    full = lambda a: pl.BlockSpec(a.shape, lambda i: (0,) * a.ndim)
    dirs = pl.BlockSpec((N_DIR, tm, D_B), lambda i: (0, i, 0))
    dir_shape = lambda dtype: jax.ShapeDtypeStruct((N_DIR, t, D_B), dtype)
    return pl.pallas_call(
        functools.partial(_stage1_kernel, seq_len // tm),
        grid=(t // tm,),
        in_specs=[
            pl.BlockSpec((tm, D_MODEL), lambda i: (i, 0)),
            pl.BlockSpec((HALO, D_MODEL), lambda i: (jnp.maximum(i * hb - 1, 0), 0)),
            pl.BlockSpec((HALO, D_MODEL), lambda i: (jnp.minimum((i + 1) * hb, t // HALO - 1), 0)),
        ] + [full(a) for a in consts],
        out_specs=[
            pl.BlockSpec((tm, D_A), lambda i: (i, 0)),
            pl.BlockSpec((tm, D_B), lambda i: (i, 0)),
            dirs, dirs, dirs, dirs, dirs,
            pl.BlockSpec((tm // DN_CHUNK, N_GB, LANES), lambda i: (i, 0, 0)),
        ],
        out_shape=[
            jax.ShapeDtypeStruct((t, D_A), BF16),
            jax.ShapeDtypeStruct((t, D_B), F32),
            dir_shape(BF16), dir_shape(BF16), dir_shape(F32), dir_shape(BF16), dir_shape(BF16),
            jax.ShapeDtypeStruct((t // DN_CHUNK, N_GB, LANES), F32),
        ],
        scratch_shapes=[pltpu.VMEM((tm + 2 * HALO, D_CONV), F32), pltpu.VMEM((tm, D_CONV), F32),
                        pltpu.VMEM((tm, N_GB), F32), pltpu.VMEM((N_GB, tm), F32)],
        compiler_params=pltpu.CompilerParams(
            dimension_semantics=("arbitrary",), vmem_limit_bytes=VMEM_LIMIT),
        name="stage1_inproj_gmlp_gdnprep",
    )(x, x, x, *consts)


def _prep_stage1_weights(norm_mix_w, w_in, a_ln_w, a_ln_b, a_spatial_w, a_spatial_b,
                         a_out_norm_w, a_log, dt_bias):
    c0 = 2 * D_A
    c1 = c0 + D_CONV
    c2 = c1 + D_B
    wuv = w_in[:, :c0].reshape(D_MODEL, 2, A_GROUPS, A_DG).transpose(0, 2, 1, 3)
    wuv = wuv.reshape(D_MODEL, c0).astype(BF16)
    wqkv = w_in[:, c0:c1].astype(BF16)
    wz = w_in[:, c1:c2].astype(BF16)
    wgb_raw = w_in[:, c2:]
    wgb = jnp.pad(wgb_raw, ((0, 0), (0, LANES - N_GB))).astype(BF16)
    wgbt = wgb_raw.T.astype(BF16)
    return (norm_mix_w.reshape(1, D_MODEL), wuv, wqkv, wz, wgb, wgbt,
            a_ln_w.reshape(1, D_A), a_ln_b.reshape(1, D_A), a_spatial_w.astype(BF16),
            a_spatial_b.T, a_out_norm_w.reshape(1, D_A),
            a_log.reshape(1, N_DIR * B_HEADS), dt_bias.reshape(1, N_DIR * B_HEADS),
            a_log.reshape(N_DIR * B_HEADS, 1), dt_bias.reshape(N_DIR * B_HEADS, 1))


PAIR = 2 * DN_CHUNK
HALO = 8
assert CONV_W == 5 and CONV_W // 2 <= HALO


def _split3(x):
    hi = x.astype(BF16)
    r1 = x - hi.astype(F32)
    mid = r1.astype(BF16)
    lo = (r1 - mid.astype(F32)).astype(BF16)
    return hi, mid, lo


def _dot_exact_rhs01(x, m01):
    hi, mid, lo = _split3(x)
    return _dot(hi, m01) + _dot(mid, m01) + _dot(lo, m01)


def _dot_exact_lhs01(m01, x):
    hi, mid, lo = _split3(x)
    return _dot(m01, hi) + _dot(m01, mid) + _dot(m01, lo)


INV_BASE = 8


def _unit_tri_inverses(a_negs):
    n = a_negs[0].shape[0]
    row = lax.broadcasted_iota(jnp.int32, (n, n), 0)
    col = lax.broadcasted_iota(jnp.int32, (n, n), 1)
    same_block = lambda size: ((row >> int(math.log2(size))) == (col >> int(math.log2(size))))
    eye = jnp.where(row == col, 1.0, 0.0)

    diag = [jnp.where(same_block(INV_BASE), a, 0.0) for a in a_negs]
    t = [eye + d for d in diag]
    d16 = [d.astype(BF16) for d in diag]
    p16 = [_dot(x, x).astype(BF16) for x in d16]
    both = [_dot(jnp.concatenate([ti.astype(BF16), pi], axis=0), pi) for ti, pi in zip(t, p16)]
    t = [ti + bi[:n] for ti, bi in zip(t, both)]
    t = [ti + _dot(ti.astype(BF16), bi[n:].astype(BF16)) for ti, bi in zip(t, both)]

    size = INV_BASE
    while size < DN_CHUNK:
        off = same_block(2 * size) & jnp.logical_not(same_block(size))
        m16 = [_dot(jnp.where(off, a, 0.0).astype(BF16), ti.astype(BF16)).astype(BF16)
               for a, ti in zip(a_negs, t)]
        t = [ti + _dot(ti.astype(BF16), mi) for ti, mi in zip(t, m16)]
        size *= 2
    return t


def _short_conv_silu(t0, tc, xpad_ref, act_ref, cw_ref):
    win = PAIR + 2 * HALO
    for cb in range(D_CONV // LANES):
        cols = slice(cb * LANES, (cb + 1) * LANES)
        wj = [cw_ref[j:j + 1, cols] for j in range(CONV_W)]
        for r0 in range(t0, t0 + tc, PAIR):
            xw = xpad_ref[r0:r0 + win, cols]
            up = pltpu.roll(wj[3] * xw + pltpu.roll(wj[4] * xw, win - 1, axis=0), win - 1, axis=0)
            dn = pltpu.roll(wj[1] * xw + pltpu.roll(wj[0] * xw, 1, axis=0), 1, axis=0)
            acc = wj[2] * xw + up + dn
            act_ref[r0:r0 + PAIR, cols] = _silu(acc[HALO:HALO + PAIR])


def _gdn_prep(t0, tc, act_ref, gb_ref, gbt_ref, w_ref, qg_ref, u_ref, qk_ref, kgt_ref, gl_ref):
    row = lax.broadcasted_iota(jnp.int32, (PAIR, PAIR), 0)
    col = lax.broadcasted_iota(jnp.int32, (PAIR, PAIR), 1)
    same = (row >= DN_CHUNK) == (col >= DN_CHUNK)
    incl = (same & (row >= col), same & (row <= col))
    strict = (same & (row > col), same & (row < col))
    as01 = lambda m: jnp.where(m, 1.0, 0.0).astype(BF16)
    m_incl = tuple(as01(m) for m in incl)
    m_same = as01(same)
    e_chunk = (as01(row < DN_CHUNK), as01(row >= DN_CHUNK))
    na = N_DIR * B_HEADS

    chains, a_negs, rhss = [], [], []
    for p in range(t0 // PAIR, (t0 + tc) // PAIR):
        rows = slice(p * PAIR, (p + 1) * PAIR)
        gbp = gb_ref[rows, :]
        gbtp = gbt_ref[:, rows]
        gcol = tuple(_dot_exact_lhs01(m_incl[d], gbp) for d in range(N_DIR))
        grow = tuple(_dot_exact_rhs01(gbtp, m_incl[1 - d]) for d in range(N_DIR))
        tot_row = _dot_exact_rhs01(gbtp, m_same)
        for c in range(2):
            gl_ref[2 * p + c] = jnp.exp(_dot_exact_rhs01(gbtp, e_chunk[c]))

        heads = []
        for h in range(B_HEADS):
            q = act_ref[rows, h * B_DK:(h + 1) * B_DK]
            k = act_ref[rows, D_QK + h * B_DK:D_QK + (h + 1) * B_DK]
            qn = q * lax.rsqrt(jnp.sum(q * q, axis=-1, keepdims=True) + EPS) * (B_DK ** -0.5)
            kn = k * lax.rsqrt(jnp.sum(k * k, axis=-1, keepdims=True) + EPS)
            kt = kn.T
            heads.append((qn, kn, kt, kt.astype(BF16)))
        kks = [_dot(kn.astype(BF16), kt16) for _, kn, _, kt16 in heads]
        qks = [_dot(qn.astype(BF16), kt16) for qn, _, _, kt16 in heads]

        for d, h in [(d, h) for d in range(N_DIR) for h in range(B_HEADS)]:
            chains.append((rows, d, h))
            ci = d * B_HEADS + h
            lanes = slice(h * LANES, (h + 1) * LANES)
            qn, kn, kt, _ = heads[h]
            v = act_ref[rows, 2 * D_QK + h * B_DV:2 * D_QK + (h + 1) * B_DV]
            gc = gcol[d][:, ci:ci + 1]
            gr = grow[d][ci:ci + 1, :]
            beta = gbp[:, na + ci:na + ci + 1]
            decay = jnp.where(incl[d], jnp.exp(gc - gr), 0.0)
            a_negs.append(jnp.where(strict[d], -(kks[h] * beta * decay), 0.0))
            eg = jnp.exp(gc)
            rhss.append(jnp.concatenate([v * beta, kn * (beta * eg)], axis=1).astype(BF16))
            qk_ref[d, rows, lanes] = (qks[h] * decay).astype(BF16)
            qg_ref[d, rows, lanes] = (qn * eg).astype(BF16)
            kgt_ref[d, rows, lanes] = (kt * jnp.exp(tot_row[ci:ci + 1, :] - gr)).astype(BF16)

    tinvs = _unit_tri_inverses(a_negs)
    uws = [_dot(t.astype(BF16), rhs) for t, rhs in zip(tinvs, rhss)]
    for (rows, d, h), uw in zip(chains, uws):
        lanes = slice(h * LANES, (h + 1) * LANES)
        u_ref[d, rows, lanes] = uw[:, :B_DV]
        w_ref[d, rows, lanes] = uw[:, B_DV:].astype(BF16)


def _stage3_kernel(wf, qgf, uf, qkf, kgf, wb, qgb, ub, qkb, kgb, glf, glb, of_ref, ob_ref, s_ref):
    i = pl.program_id(1)

    @pl.when(i == 0)
    def _():
        s_ref[...] = jnp.zeros_like(s_ref)

    nseq, rows_blk = wf.shape[1], wf.shape[2]
    npairs = rows_blk // PAIR
    zpad = jnp.zeros((DN_CHUNK, B_DV), BF16)
    per_dir = ((wf, qgf, uf, qkf, kgf, glf, of_ref), (wb, qgb, ub, qkb, kgb, glb, ob_ref))
    chains = [(q, d, h) for q in range(nseq) for d in range(N_DIR) for h in range(B_HEADS)]
    for step in range(2 * npairs):
        def where(d):
            chunk = step if d == 0 else 2 * npairs - 1 - step
            pair = chunk // 2
            return (chunk, slice(chunk * DN_CHUNK, (chunk + 1) * DN_CHUNK),
                    slice(pair * PAIR, (pair + 1) * PAIR))

        states, m1s, m2s = [], [], []
        for q, d, h in chains:
            w_r, qg_r = per_dir[d][0], per_dir[d][1]
            _, rows, _ = where(d)
            lanes = slice(h * LANES, (h + 1) * LANES)
            s = s_ref[q, d, h]
            states.append(s)
            lhs1 = jnp.concatenate([w_r[0, q, rows, lanes], qg_r[0, q, rows, lanes]], axis=0)
            m1s.append(_dot(lhs1, s.astype(BF16)))
        for (q, d, h), m1 in zip(chains, m1s):
            u_r, qk_r, kg_r = per_dir[d][2], per_dir[d][3], per_dir[d][4]
            chunk, rows, prow = where(d)
            lanes = slice(h * LANES, (h + 1) * LANES)
            v_new = (u_r[0, q, rows, lanes] - m1[:DN_CHUNK]).astype(BF16)
            v_pad = jnp.concatenate([v_new, zpad] if chunk % 2 == 0 else [zpad, v_new], axis=0)
            lhs2 = jnp.concatenate([qk_r[0, q, rows, lanes], kg_r[0, q, prow, lanes]], axis=0)
            m2s.append(_dot(lhs2, v_pad))
        for (q, d, h), s, m1, m2 in zip(chains, states, m1s, m2s):
            gl_r, o_r = per_dir[d][5], per_dir[d][6]
            chunk, rows, _ = where(d)
            lanes = slice(h * LANES, (h + 1) * LANES)
            o_r[q, rows, lanes] = m1[DN_CHUNK:] + m2[:DN_CHUNK]
            ci = d * B_HEADS + h
            s_ref[q, d, h] = s * gl_r[q, chunk, ci:ci + 1, :] + m2[DN_CHUNK:]


def _stage3(w, qg, u, qk, kgt, gl, rows_blk):
    _, b, s, _ = w.shape
    n_i = s // rows_blk
    cpb = rows_blk // DN_CHUNK
    nseq = 2 if b % 2 == 0 else 1
    fwd = pl.BlockSpec((1, nseq, rows_blk, D_B), lambda bi, i: (0, bi, i, 0))
    bwd = pl.BlockSpec((1, nseq, rows_blk, D_B), lambda bi, i: (1, bi, n_i - 1 - i, 0))
    return pl.pallas_call(
        _stage3_kernel,
        grid=(b // nseq, n_i),
        in_specs=[fwd] * 5 + [bwd] * 5 + [
            pl.BlockSpec((nseq, cpb, N_GB, LANES), lambda bi, i: (bi, i, 0, 0)),
            pl.BlockSpec((nseq, cpb, N_GB, LANES), lambda bi, i: (bi, n_i - 1 - i, 0, 0)),
        ],
        out_specs=[pl.BlockSpec((nseq, rows_blk, D_B), lambda bi, i: (bi, i, 0)),
                   pl.BlockSpec((nseq, rows_blk, D_B), lambda bi, i: (bi, n_i - 1 - i, 0))],
        out_shape=[jax.ShapeDtypeStruct((b, s, D_B), F32), jax.ShapeDtypeStruct((b, s, D_B), F32)],
        scratch_shapes=[pltpu.VMEM((nseq, N_DIR, B_HEADS, B_DK, B_DV), F32)],
        compiler_params=pltpu.CompilerParams(
            dimension_semantics=("arbitrary", "arbitrary"), vmem_limit_bytes=VMEM_LIMIT),
        name="stage3_gdn_scan",
    )(w, qg, u, qk, kgt, w, qg, u, qk, kgt, gl, gl)


N_ROUTE = N_EXPERT_GROUPS + N_EXPERTS


def _stage4_kernel(x_ref, ya_ref, of_ref, ob_ref, z_ref, gnw_ref, woa_ref, wob_ref, fnw_ref,
                   wr_ref, h_ref, hn_ref, ids_ref, wts_ref, cnt_ref):
    tm = x_ref.shape[0]
    for k in range(tm // MOE_TM):
        cnt_ref[k] = _stage4_rows(slice(k * MOE_TM, (k + 1) * MOE_TM), x_ref, ya_ref, of_ref,
                                  ob_ref, z_ref, gnw_ref, woa_ref, wob_ref, fnw_ref, wr_ref,
                                  h_ref, hn_ref, ids_ref, wts_ref)


def _stage4_rows(sl, x_ref, ya_ref, of_ref, ob_ref, z_ref, gnw_ref, woa_ref, wob_ref, fnw_ref,
                 wr_ref, h_ref, hn_ref, ids_ref, wts_ref):
    tm = sl.stop - sl.start
    o = of_ref[sl, :] + ob_ref[sl, :]
    z = z_ref[sl, :]
    parts = []
    for hd in range(B_HEADS):
        lanes = slice(hd * B_DV, (hd + 1) * B_DV)
        oh = o[:, lanes]
        yh = oh * lax.rsqrt(jnp.mean(oh * oh, axis=-1, keepdims=True) + EPS) * gnw_ref[...]
        parts.append((yh * _silu(z[:, lanes])).astype(BF16))
    yb = jnp.concatenate(parts, axis=1)
    h = x_ref[sl, :] + (_dot(ya_ref[sl, :], woa_ref[...]) + _dot(yb, wob_ref[...]))
    h_ref[sl, :] = h
    hn = h * lax.rsqrt(jnp.mean(h * h, axis=-1, keepdims=True) + EPS) * fnw_ref[...]

    hi = hn.astype(BF16)
    hn_ref[sl, :] = hi
    lo = (hn - hi.astype(F32)).astype(BF16)
    prod = _dot(jnp.concatenate([hi, lo], axis=0), wr_ref[...])
    logits = ((prod[:tm, :LANES] + prod[tm:, :LANES])
              + (prod[:tm, LANES:] + prod[tm:, LANES:]))

    lane = lax.broadcasted_iota(jnp.int32, (tm, LANES), 1)
    neg = -jnp.inf
    is_g = lane < N_EXPERT_GROUPS
    gl = jnp.where(is_g, logits, neg)
    gmax = jnp.max(gl, axis=-1, keepdims=True)
    gidx = jnp.min(jnp.where(gl == gmax, lane, LANES), axis=-1, keepdims=True)
    g_w = 1.0 / jnp.sum(jnp.where(is_g, jnp.exp(gl - gmax), 0.0), axis=-1, keepdims=True)

    e0 = N_EXPERT_GROUPS + gidx * EXPERTS_PER_GROUP
    in_grp = (lane >= e0) & (lane < e0 + EXPERTS_PER_GROUP)
    el = jnp.where(in_grp, logits, neg)
    m1 = jnp.max(el, axis=-1, keepdims=True)
    i1 = jnp.min(jnp.where(el == m1, lane, LANES), axis=-1, keepdims=True)
    el2 = jnp.where(lane == i1, neg, el)
    m2 = jnp.max(el2, axis=-1, keepdims=True)
    i2 = jnp.min(jnp.where(el2 == m2, lane, LANES), axis=-1, keepdims=True)
    e2 = jnp.exp(m2 - m1)
    inv = 1.0 / (1.0 + e2)
    ids_ref[sl, 0:1] = i1 - N_EXPERT_GROUPS
    ids_ref[sl, 1:2] = i2 - N_EXPERT_GROUPS
    wts_ref[sl, 0:1] = g_w * inv
    wts_ref[sl, 1:2] = g_w * (e2 * inv)
    elane = lane + N_EXPERT_GROUPS
    chosen = (elane == i1) | (elane == i2)
    return jnp.sum(jnp.where(chosen, 1.0, 0.0), axis=0, keepdims=True)


def _stage4(x, ya, o_f, o_b, z, gnw, woa, wob, fnw, wr, tm):
    t = x.shape[0]
    full = lambda a: pl.BlockSpec(a.shape, lambda i: (0,) * a.ndim)
    tile = lambda n: pl.BlockSpec((tm, n), lambda i: (i, 0))
    return pl.pallas_call(
        _stage4_kernel,
        grid=(t // tm,),
        in_specs=[tile(D_MODEL), tile(D_A), tile(D_B), tile(D_B), tile(D_B),
                  full(gnw), full(woa), full(wob), full(fnw), full(wr)],
        out_specs=[tile(D_MODEL), tile(D_MODEL), tile(TOP_K), tile(TOP_K),
                   pl.BlockSpec((tm // MOE_TM, 1, LANES), lambda i: (i, 0, 0))],
        out_shape=[jax.ShapeDtypeStruct((t, D_MODEL), F32), jax.ShapeDtypeStruct((t, D_MODEL), BF16),
                   jax.ShapeDtypeStruct((t, TOP_K), jnp.int32),
                   jax.ShapeDtypeStruct((t, TOP_K), F32),
                   jax.ShapeDtypeStruct((t // MOE_TM, 1, LANES), F32)],
        compiler_params=pltpu.CompilerParams(
            dimension_semantics=("arbitrary",), vmem_limit_bytes=VMEM_LIMIT),
        name="stage4_outproj_router",
    )(x, ya, o_f, o_b, z, gnw, woa, wob, fnw, wr)


MOE_TM = 512
MOE_R = 16
MOE_BM = 512
MOE_L = TOP_K * MOE_TM + N_EXPERTS * MOE_R


def _moe_plan(cnt):
    n_tiles = cnt.shape[0]
    c = cnt[:, :N_EXPERTS].astype(jnp.int32)
    cpad = (c + MOE_R - 1) // MOE_R * MOE_R
    seg = jnp.sum(cpad, axis=0)
    segpad = (seg + MOE_BM - 1) // MOE_BM * MOE_BM
    pad_end = jnp.cumsum(segpad)
    pad_start = pad_end - segpad
    off = pad_start[None, :] + jnp.cumsum(cpad, axis=0) - cpad
    loc = jnp.cumsum(cpad, axis=1) - cpad
    nch = cpad // MOE_R
    n_blocks = -(-(TOP_K * MOE_TM + N_EXPERTS * (MOE_R - 1)) * n_tiles // MOE_BM) + N_EXPERTS
    n_used = pad_end[-1] // MOE_BM
    blk = jnp.arange(n_blocks, dtype=jnp.int32)
    first_row = jnp.minimum(blk, n_used - 1) * MOE_BM
    block_expert = jnp.sum((pad_end[None, :] <= first_row[:, None]).astype(jnp.int32), axis=1)
    block_expert = jnp.minimum(block_expert, N_EXPERTS - 1)
    loc_lanes = jnp.pad(loc, ((0, 0), (0, LANES - N_EXPERTS))).astype(F32)
    return dict(off=off.reshape(-1), loc=loc.reshape(-1), nch=nch.reshape(-1),
                tot=jnp.sum(nch, axis=1), tail_off=pad_start + seg,
                tail_n=(segpad - seg) // MOE_R, loc_lanes=loc_lanes.reshape(n_tiles, 1, LANES),
                block_expert=block_expert, n_used=n_used.reshape(1), n_blocks=n_blocks)


def _run_chunks(nch_ref, loc_ref, off_ref, tile, visit):
    def per_expert(e, carry):
        idx = tile * N_EXPERTS + e
        loc0 = loc_ref[idx]
        off0 = off_ref[idx]

        def per_chunk(j, c):
            visit(pl.multiple_of(loc0 + j * MOE_R, MOE_R), pl.multiple_of(off0 + j * MOE_R, MOE_R))
            return c
        lax.fori_loop(0, nch_ref[idx], per_chunk, 0)
        return carry
    lax.fori_loop(0, N_EXPERTS, per_expert, 0)


def _wait_chunks(count, chunk_wait):
    for bit in range((MOE_L // MOE_R).bit_length()):
        @pl.when(((count >> bit) & 1) == 1)
        def _():
            chunk_wait((1 << bit) * MOE_R)


def _dispatch_kernel(n, n_first, n_blocks, off_ref, loc_ref, nch_ref, tot_ref, toff_ref, tn_ref,
                     nu_ref, ids_ref, hna_ref, hnb_ref, locl_ref, xs_ref, pos_ref,
                     hn_ref, xl, zrows, sem, tsem):
    i = pl.program_id(0)
    g = i
    slot = lax.rem(i, 2)
    tm = ids_ref.shape[0]

    @pl.when(i < n_first)
    def _():
        hn_ref[...] = hna_ref[...]

    @pl.when(i >= n_first)
    def _():
        hn_ref[...] = hnb_ref[...]

    ids = ids_ref[...]
    lane = lax.broadcasted_iota(jnp.int32, (tm, LANES), 1)
    oh0 = jnp.where(lane == ids[:, 0:1], 1.0, 0.0)
    oh1 = jnp.where(lane == ids[:, 1:2], 1.0, 0.0)
    row = lax.broadcasted_iota(jnp.int32, (tm, tm), 0)
    col = lax.broadcasted_iota(jnp.int32, (tm, tm), 1)
    earlier = jnp.where(row > col, 1.0, 0.0).astype(BF16)
    base = _dot(earlier, (oh0 + oh1).astype(BF16)) + locl_ref[0]
    m0 = base * oh0
    m1 = base * oh1
    pos_ref[:, 0:1] = jnp.sum(m0, axis=-1, keepdims=True).astype(jnp.int32)
    pos_ref[:, 1:2] = jnp.sum(m1, axis=-1, keepdims=True).astype(jnp.int32)
    ones = jnp.ones((8, LANES), BF16)
    lane_form = lambda m: sum(_dot_nt(ones, part) for part in _split3(m))[0:1].astype(jnp.int32)
    p0 = lane_form(m0)
    p1 = lane_form(m1)
    srow = lax.broadcasted_iota(jnp.int32, (MOE_L, tm), 0)
    perm = jnp.where((srow == p0) | (srow == p1), 1.0, 0.0).astype(BF16)
    nb = 256
    for cb in range(D_MODEL // nb):
        xl[slot, :, cb * nb:(cb + 1) * nb] = _dot(perm, hn_ref[:, cb * nb:(cb + 1) * nb]).astype(BF16)

    def rows_copy(sl, lrow, grow, nrows):
        return pltpu.make_async_copy(xl.at[sl, pl.ds(lrow, nrows)],
                                     xs_ref.at[pl.ds(grow, nrows)], sem.at[sl])

    _run_chunks(nch_ref, loc_ref, off_ref, g,
                lambda lrow, grow: rows_copy(slot, lrow, grow, MOE_R).start())

    def wait_tile(tile, sl):
        _wait_chunks(tot_ref[tile], lambda nrows: rows_copy(sl, 0, 0, nrows).wait())

    @pl.when(i > 0)
    def _():
        wait_tile(g - 1, 1 - slot)

    @pl.when(i == n - 1)
    def _():
        wait_tile(g, slot)
        zrows[...] = jnp.zeros_like(zrows)

        def tail_copy(e, j):
            row0 = pl.multiple_of(toff_ref[e] + j * MOE_R, MOE_R)
            return pltpu.make_async_copy(zrows.at[pl.ds(0, MOE_R)],
                                         xs_ref.at[pl.ds(row0, MOE_R)], tsem.at[0])

        def block_copy(b):
            row0 = pl.multiple_of(b * MOE_BM, MOE_BM)
            return pltpu.make_async_copy(zrows, xs_ref.at[pl.ds(row0, MOE_BM)], tsem.at[0])

        def fill(act):
            def per_expert(e, carry):
                def per_chunk(j, c):
                    act(tail_copy(e, j))
                    return c
                lax.fori_loop(0, tn_ref[e], per_chunk, 0)
                return carry
            lax.fori_loop(0, N_EXPERTS, per_expert, 0)

            def per_block(b, c):
                act(block_copy(b))
                return c
            lax.fori_loop(nu_ref[0], n_blocks, per_block, 0)

        fill(lambda cp: cp.start())
        fill(lambda cp: cp.wait())


def _dispatch(plan, ids, hn_a, hn_b):
    n_a = hn_a.shape[0] // MOE_TM
    n = ids.shape[0] // MOE_TM
    n_blocks = plan['n_blocks']
    grid_spec = pltpu.PrefetchScalarGridSpec(
        num_scalar_prefetch=7,
        grid=(n,),
        in_specs=[
            pl.BlockSpec((MOE_TM, TOP_K), lambda i, *_: (i, 0)),
            pl.BlockSpec((MOE_TM, D_MODEL), lambda i, *_: (jnp.minimum(i, n_a - 1), 0)),
            pl.BlockSpec((MOE_TM, D_MODEL), lambda i, *_: (jnp.maximum(i - n_a, 0), 0)),
            pl.BlockSpec((1, 1, LANES), lambda i, *_: (i, 0, 0)),
        ],
        out_specs=[pl.BlockSpec(memory_space=pl.ANY),
                   pl.BlockSpec((MOE_TM, TOP_K), lambda i, *_: (i, 0))],
        scratch_shapes=[pltpu.VMEM((MOE_TM, D_MODEL), BF16),
                        pltpu.VMEM((2, MOE_L, D_MODEL), BF16), pltpu.VMEM((MOE_BM, D_MODEL), BF16),
                        pltpu.SemaphoreType.DMA((2,)), pltpu.SemaphoreType.DMA((1,))],
    )
    return pl.pallas_call(
        functools.partial(_dispatch_kernel, n, n_a, n_blocks),
        grid_spec=grid_spec,
        out_shape=[jax.ShapeDtypeStruct((n_blocks * MOE_BM, D_MODEL), BF16),
                   jax.ShapeDtypeStruct((ids.shape[0], TOP_K), jnp.int32)],
        compiler_params=pltpu.CompilerParams(
            dimension_semantics=("arbitrary",), vmem_limit_bytes=VMEM_LIMIT),
        name="stage5a_dispatch",
    )(plan['off'], plan['loc'], plan['nch'], plan['tot'], plan['tail_off'], plan['tail_n'],
      plan['n_used'], ids, hn_a, hn_b, plan['loc_lanes'])


def _expert_kernel(be_ref, nu_ref, xs_ref, wg_ref, wu_ref, wd_ref, y_ref, wg16, wu16, wd16):
    b = pl.program_id(0)

    @pl.when((b == 0) | (be_ref[b] != be_ref[jnp.maximum(b - 1, 0)]))
    def _():
        wg16[...] = wg_ref[0].astype(BF16)
        wu16[...] = wu_ref[0].astype(BF16)
        wd16[...] = wd_ref[0].astype(BF16)

    @pl.when(b < nu_ref[0])
    def _():
        x = xs_ref[...]
        g = _dot(x, wg16[...])
        u = _dot(x, wu16[...])
        y_ref[...] = _dot((_silu(g) * u).astype(BF16), wd16[...]).astype(BF16)

    @pl.when(b >= nu_ref[0])
    def _():
        y_ref[...] = jnp.zeros_like(y_ref)


def _experts(plan, xs, wg, wu, wd):
    n_blocks = plan['n_blocks']
    used = lambda b, nu: jnp.minimum(b, nu[0] - 1)
    grid_spec = pltpu.PrefetchScalarGridSpec(
        num_scalar_prefetch=2,
        grid=(n_blocks,),
        in_specs=[
            pl.BlockSpec((MOE_BM, D_MODEL), lambda b, be, nu: (used(b, nu), 0)),
            pl.BlockSpec((1, D_MODEL, D_EXPERT), lambda b, be, nu: (be[b], 0, 0)),
            pl.BlockSpec((1, D_MODEL, D_EXPERT), lambda b, be, nu: (be[b], 0, 0)),
            pl.BlockSpec((1, D_EXPERT, D_MODEL), lambda b, be, nu: (be[b], 0, 0)),
        ],
        out_specs=pl.BlockSpec((MOE_BM, D_MODEL), lambda b, be, nu: (b, 0)),
        scratch_shapes=[pltpu.VMEM((D_MODEL, D_EXPERT), BF16), pltpu.VMEM((D_MODEL, D_EXPERT), BF16),
                        pltpu.VMEM((D_EXPERT, D_MODEL), BF16)],
    )
    return pl.pallas_call(
        _expert_kernel,
        grid_spec=grid_spec,
        out_shape=jax.ShapeDtypeStruct((n_blocks * MOE_BM, D_MODEL), BF16),
        compiler_params=pltpu.CompilerParams(
            dimension_semantics=("arbitrary",), vmem_limit_bytes=VMEM_LIMIT),
        name="stage5b_experts",
    )(plan['block_expert'], plan['n_used'], xs, wg, wu, wd)


def _combine_kernel(n, tile_base, off_ref, loc_ref, nch_ref, tot_ref, pos_ref, wts_ref, h_ref,
                    fw_ref, y_hbm, out_ref, yl, sem):
    i = pl.program_id(0)
    g = tile_base + i
    slot = lax.rem(i, 2)
    tm = h_ref.shape[0]

    def rows_copy(sl, lrow, grow, nrows):
        return pltpu.make_async_copy(y_hbm.at[pl.ds(grow, nrows)],
                                     yl.at[sl, pl.ds(lrow, nrows)], sem.at[sl])

    def fetch(tile, sl):
        _run_chunks(nch_ref, loc_ref, off_ref, tile,
                    lambda lrow, grow: rows_copy(sl, lrow, grow, MOE_R).start())

    @pl.when(i == 0)
    def _():
        yl[...] = jnp.zeros_like(yl)
        fetch(g, 0)

    @pl.when(i + 1 < n)
    def _():
        fetch(g + 1, 1 - slot)

    _wait_chunks(tot_ref[g], lambda nrows: rows_copy(slot, 0, 0, nrows).wait())

    lane = lax.broadcasted_iota(jnp.int32, (tm, MOE_L), 1)
    pos = pos_ref[...]
    w = wts_ref[...]
    sel = (jnp.where(lane == pos[:, 0:1], w[:, 0:1], 0.0)
           + jnp.where(lane == pos[:, 1:2], w[:, 1:2], 0.0)).astype(BF16)
    h = h_ref[...] + _dot(sel, yl[slot])
    out_ref[...] = h * lax.rsqrt(jnp.mean(h * h, axis=-1, keepdims=True) + EPS) * fw_ref[...]


def _combine(plan, tile_base, pos, wts, h, fw, y_rows):
    t = h.shape[0]
    n = t // MOE_TM
    grid_spec = pltpu.PrefetchScalarGridSpec(
        num_scalar_prefetch=4,
        grid=(n,),
        in_specs=[
            pl.BlockSpec((MOE_TM, TOP_K), lambda i, *_: (i, 0)),
            pl.BlockSpec((MOE_TM, TOP_K), lambda i, *_: (i, 0)),
            pl.BlockSpec((MOE_TM, D_MODEL), lambda i, *_: (i, 0)),
            pl.BlockSpec((1, D_MODEL), lambda i, *_: (0, 0)),
            pl.BlockSpec(memory_space=pl.ANY),
        ],
        out_specs=pl.BlockSpec((MOE_TM, D_MODEL), lambda i, *_: (i, 0)),
        scratch_shapes=[pltpu.VMEM((2, MOE_L, D_MODEL), BF16), pltpu.SemaphoreType.DMA((2,))],
    )
    return pl.pallas_call(
        functools.partial(_combine_kernel, n, tile_base),
        grid_spec=grid_spec,
        out_shape=jax.ShapeDtypeStruct((t, D_MODEL), F32),
        compiler_params=pltpu.CompilerParams(
            dimension_semantics=("arbitrary",), vmem_limit_bytes=VMEM_LIMIT),
        name="stage6_combine_norm",
    )(plan['off'], plan['loc'], plan['nch'], plan['tot'], pos, wts, h, fw, y_rows)


def kernel(x_prompt, x_sample, norm_mix_w, w_in, a_ln_w, a_ln_b, a_spatial_w, a_spatial_b, a_out_norm_w, conv_w, a_log, dt_bias, gdn_norm_w, w_out, norm_ffn_w, w_router_group, w_router_expert, w_gate, w_up, w_down, norm_final_w):
    s1w = _prep_stage1_weights(norm_mix_w[0], w_in[0], a_ln_w[0], a_ln_b[0], a_spatial_w[0],
                               a_spatial_b[0], a_out_norm_w[0], a_log[0], dt_bias[0])
    woa = w_out[0, :D_A].astype(BF16)
    wob = w_out[0, D_A:].astype(BF16)
    w_r = jnp.concatenate([w_router_group[0], w_router_expert[0]], axis=1)
    w_r = jnp.pad(w_r, ((0, 0), (0, LANES - N_ROUTE)))
    wrh = w_r.astype(BF16)
    wr = jnp.concatenate([wrh, (w_r - wrh.astype(F32)).astype(BF16)], axis=1)
    gnw = gdn_norm_w[0].reshape(1, B_DV)
    fnw = norm_ffn_w[0].reshape(1, D_MODEL)

    per_run = []
    for x in (x_prompt, x_sample):
        b, s, d = x.shape
        x2 = x.reshape(b * s, d)
        ya, z, w, qg, u, qk, kgt, gl = _stage1(x2, s, *s1w, conv_w[0], tm=512)
        per_seq = lambda a: a.reshape(N_DIR, b, s, D_B)
        o_f, o_b = _stage3(per_seq(w), per_seq(qg), per_seq(u), per_seq(qk), per_seq(kgt),
                           gl.reshape(b, s // DN_CHUNK, N_GB, LANES), rows_blk=512)
        h, hn, ids, wts, cnt = _stage4(x2, ya, o_f.reshape(b * s, D_B), o_b.reshape(b * s, D_B),
                                       z, gnw, woa, wob, fnw, wr, tm=2 * MOE_TM)
        per_run.append(dict(shape=x.shape, h=h, hn=hn, ids=ids, wts=wts, cnt=cnt[:, 0, :]))

    plan = _moe_plan(jnp.concatenate([r['cnt'] for r in per_run], axis=0))
    ids_all = jnp.concatenate([r['ids'] for r in per_run], axis=0)
    xs, pos_all = _dispatch(plan, ids_all, per_run[0]['hn'], per_run[1]['hn'])
    y_rows = _experts(plan, xs, w_gate[0], w_up[0], w_down[0])
    fw = norm_final_w.reshape(1, D_MODEL)
    outs = []
    t0 = 0
    for r in per_run:
        t = r['h'].shape[0]
        outs.append(_combine(plan, t0 // MOE_TM, pos_all[t0:t0 + t], r['wts'], r['h'], fw,
                             y_rows).reshape(r['shape']))
        t0 += t
    return tuple(outs)
```

```python
import functools
import math

import jax
import jax.numpy as jnp
from jax import lax
from jax.experimental import pallas as pl
from jax.experimental.pallas import tpu as pltpu

D_MODEL = 1024
D_A = 512
A_GROUPS = 4
A_DG = 128
A_CHUNK = 128
D_B = 512
B_HEADS = 4
B_DK = 128
B_DV = 128
D_QK = 512
DN_CHUNK = 64
CONV_W = 5
N_DIR = 2
N_EXPERT_GROUPS = 4
EXPERTS_PER_GROUP = 8
N_EXPERTS = 32
TOP_K = 2
D_EXPERT = 512
EPS = 1e-6
D_CONV = 2 * D_QK + D_B
N_GB = 2 * N_DIR * B_HEADS

LANES = 128
VMEM_LIMIT = 48 * 1024 * 1024

BF16 = jnp.bfloat16
F32 = jnp.float32


def _dot(a, b):
    return jnp.dot(a, b, preferred_element_type=F32)


def _dot_nt(a, b):
    return lax.dot_general(a, b, (((1,), (1,)), ((), ())), preferred_element_type=F32)


def _gelu_tanh(x):
    c = math.sqrt(2.0 / math.pi)
    return x * (0.5 * (1.0 + jnp.tanh(c * (x + 0.044715 * (x * x * x)))))


def _sigmoid(x):
    return 0.5 * (1.0 + jnp.tanh(0.5 * x))


def _silu(x):
    return x * _sigmoid(x)


def _softplus(x):
    return jnp.maximum(x, 0.0) + jnp.log(1.0 + jnp.exp(-jnp.abs(x)))


GDN_TC = 256


def _stage1_kernel(tiles_per_seq, x_ref, xp_ref, xn_ref, nw_ref, wuv_ref, wqkv_ref,
                   wz_ref, wgb_ref, wgbt_ref, lnw_ref, lnb_ref, spw_ref, spbt_ref, onw_ref,
                   alog_ref, dtb_ref, alogt_ref, dtbt_ref, cw_ref,
                   ya_ref, z_ref, w_ref, qg_ref, u_ref, qk_ref, kgt_ref, gl_ref,
                   xpad_ref, act_ref, gb_ref, gbt_ref):
    tm = x_ref.shape[0]
    na = N_DIR * B_HEADS
    seq_pos = lax.rem(pl.program_id(0), tiles_per_seq)

    def normed(x):
        xn = x * lax.rsqrt(jnp.mean(x * x, axis=-1, keepdims=True) + EPS) * nw_ref[...]
        return xn.astype(BF16)

    xb = normed(x_ref[...])
    halo = normed(jnp.concatenate([xp_ref[...], xn_ref[...]], axis=0))
    qkv = _dot(jnp.concatenate([xb, halo], axis=0), wqkv_ref[...])
    xpad_ref[0:HALO, :] = jnp.where(seq_pos > 0, qkv[tm:tm + HALO], 0.0)
    xpad_ref[HALO:HALO + tm, :] = qkv[:tm]
    xpad_ref[HALO + tm:, :] = jnp.where(seq_pos < tiles_per_seq - 1, qkv[tm + HALO:], 0.0)

    ab = _dot(xb, wgb_ref[...])[:, :N_GB]
    abt = _dot_nt(wgbt_ref[...], xb)
    gb_ref[:, :na] = -jnp.exp(alog_ref[...]) * _softplus(ab[:, :na] + dtb_ref[...])
    gb_ref[:, na:] = _sigmoid(ab[:, na:])
    gbt_ref[:na, :] = -jnp.exp(alogt_ref[...]) * _softplus(abt[:na, :] + dtbt_ref[...])
    gbt_ref[na:, :] = _sigmoid(abt[na:, :])

    def gmlp_group(grp):
        cols = slice(grp * A_DG, (grp + 1) * A_DG)
        uv_all = _dot(xb, wuv_ref[:, 2 * grp * A_DG:2 * (grp + 1) * A_DG])
        u_all = uv_all[:, :A_DG]
        v_all = uv_all[:, A_DG:]
        spw = spw_ref[grp]
        for c in range(tm // A_CHUNK):
            rows = slice(c * A_CHUNK, (c + 1) * A_CHUNK)
            u = _gelu_tanh(u_all[rows])
            v = _gelu_tanh(v_all[rows])
            mu = jnp.mean(v, axis=-1, keepdims=True)
            vc = v - mu
            var = jnp.mean(vc * vc, axis=-1, keepdims=True)
            vn = vc * lax.rsqrt(var + EPS) * lnw_ref[:, cols] + lnb_ref[:, cols]
            mixed = _dot(spw, vn.astype(BF16)) + spbt_ref[:, grp:grp + 1]
            gated = u * mixed
            out = gated * lax.rsqrt(jnp.mean(gated * gated, axis=-1, keepdims=True) + EPS)
            ya_ref[rows, cols] = (out * onw_ref[:, cols]).astype(BF16)

    n_sub = tm // GDN_TC
    for k in range(n_sub):
        _short_conv_silu(k * GDN_TC, GDN_TC, xpad_ref, act_ref, cw_ref)
        if k == 0:
            z_ref[...] = _dot(xb, wz_ref[...])
        _gdn_prep(k * GDN_TC, GDN_TC, act_ref, gb_ref, gbt_ref,
                  w_ref, qg_ref, u_ref, qk_ref, kgt_ref, gl_ref)
        for grp in range(k * A_GROUPS // n_sub, (k + 1) * A_GROUPS // n_sub):
            gmlp_group(grp)


def _stage1(x, seq_len, nw, wuv, wqkv, wz, wgb, wgbt, lnw, lnb, spw, spbt, onw, alog, dtb,
            alogt, dtbt, conv_w, tm):
    t = x.shape[0]
    hb = tm // HALO
    consts = (nw, wuv, wqkv, wz, wgb, wgbt, lnw, lnb, spw, spbt, onw, alog, dtb, alogt, dtbt,
              conv_w)
    full = lambda a: pl.BlockSpec(a.shape, lambda i: (0,) * a.ndim)
    dirs = pl.BlockSpec((N_DIR, tm, D_B), lambda i: (0, i, 0))
    dir_shape = lambda dtype: jax.ShapeDtypeStruct((N_DIR, t, D_B), dtype)
    return pl.pallas_call(
        functools.partial(_stage1_kernel, seq_len // tm),
        grid=(t // tm,),
        in_specs=[
            pl.BlockSpec((tm, D_MODEL), lambda i: (i, 0)),
            pl.BlockSpec((HALO, D_MODEL), lambda i: (jnp.maximum(i * hb - 1, 0), 0)),
            pl.BlockSpec((HALO, D_MODEL), lambda i: (jnp.minimum((i + 1) * hb, t // HALO - 1), 0)),
        ] + [full(a) for a in consts],
        out_specs=[
            pl.BlockSpec((tm, D_A), lambda i: (i, 0)),
            pl.BlockSpec((tm, D_B), lambda i: (i, 0)),
            dirs, dirs, dirs, dirs, dirs,
            pl.BlockSpec((tm // DN_CHUNK, N_GB, LANES), lambda i: (i, 0, 0)),
        ],
        out_shape=[
            jax.ShapeDtypeStruct((t, D_A), BF16),
            jax.ShapeDtypeStruct((t, D_B), F32),
            dir_shape(BF16), dir_shape(BF16), dir_shape(F32), dir_shape(BF16), dir_shape(BF16),
            jax.ShapeDtypeStruct((t // DN_CHUNK, N_GB, LANES), F32),
        ],
        scratch_shapes=[pltpu.VMEM((tm + 2 * HALO, D_CONV), F32), pltpu.VMEM((tm, D_CONV), F32),
                        pltpu.VMEM((tm, N_GB), F32), pltpu.VMEM((N_GB, tm), F32)],
        compiler_params=pltpu.CompilerParams(
            dimension_semantics=("arbitrary",), vmem_limit_bytes=VMEM_LIMIT),
        name="stage1_inproj_gmlp_gdnprep",
    )(x, x, x, *consts)


def _prep_stage1_weights(norm_mix_w, w_in, a_ln_w, a_ln_b, a_spatial_w, a_spatial_b,
                         a_out_norm_w, a_log, dt_bias):
    c0 = 2 * D_A
    c1 = c0 + D_CONV
    c2 = c1 + D_B
    wuv = w_in[:, :c0].reshape(D_MODEL, 2, A_GROUPS, A_DG).transpose(0, 2, 1, 3)
    wuv = wuv.reshape(D_MODEL, c0).astype(BF16)
    wqkv = w_in[:, c0:c1].astype(BF16)
    wz = w_in[:, c1:c2].astype(BF16)
    wgb_raw = w_in[:, c2:]
    wgb = jnp.pad(wgb_raw, ((0, 0), (0, LANES - N_GB))).astype(BF16)
    wgbt = wgb_raw.T.astype(BF16)
    return (norm_mix_w.reshape(1, D_MODEL), wuv, wqkv, wz, wgb, wgbt,
            a_ln_w.reshape(1, D_A), a_ln_b.reshape(1, D_A), a_spatial_w.astype(BF16),
            a_spatial_b.T, a_out_norm_w.reshape(1, D_A),
            a_log.reshape(1, N_DIR * B_HEADS), dt_bias.reshape(1, N_DIR * B_HEADS),
            a_log.reshape(N_DIR * B_HEADS, 1), dt_bias.reshape(N_DIR * B_HEADS, 1))


PAIR = 2 * DN_CHUNK
HALO = 8
assert CONV_W == 5 and CONV_W // 2 <= HALO


def _split3(x):
    hi = x.astype(BF16)
    r1 = x - hi.astype(F32)
    mid = r1.astype(BF16)
    lo = (r1 - mid.astype(F32)).astype(BF16)
    return hi, mid, lo


def _dot_exact_rhs01(x, m01):
    hi, mid, lo = _split3(x)
    return _dot(hi, m01) + _dot(mid, m01) + _dot(lo, m01)


def _dot_exact_lhs01(m01, x):
    hi, mid, lo = _split3(x)
    return _dot(m01, hi) + _dot(m01, mid) + _dot(m01, lo)


INV_BASE = 8


def _unit_tri_inverses(a_negs):
    n = a_negs[0].shape[0]
    row = lax.broadcasted_iota(jnp.int32, (n, n), 0)
    col = lax.broadcasted_iota(jnp.int32, (n, n), 1)
    same_block = lambda size: ((row >> int(math.log2(size))) == (col >> int(math.log2(size))))
    eye = jnp.where(row == col, 1.0, 0.0)

    diag = [jnp.where(same_block(INV_BASE), a, 0.0) for a in a_negs]
    t = [eye + d for d in diag]
    d16 = [d.astype(BF16) for d in diag]
    p16 = [_dot(x, x).astype(BF16) for x in d16]
    both = [_dot(jnp.concatenate([ti.astype(BF16), pi], axis=0), pi) for ti, pi in zip(t, p16)]
    t = [ti + bi[:n] for ti, bi in zip(t, both)]
    t = [ti + _dot(ti.astype(BF16), bi[n:].astype(BF16)) for ti, bi in zip(t, both)]

    size = INV_BASE
    while size < DN_CHUNK:
        off = same_block(2 * size) & jnp.logical_not(same_block(size))
        m16 = [_dot(jnp.where(off, a, 0.0).astype(BF16), ti.astype(BF16)).astype(BF16)
               for a, ti in zip(a_negs, t)]
        t = [ti + _dot(ti.astype(BF16), mi) for ti, mi in zip(t, m16)]
        size *= 2
    return t


def _short_conv_silu(t0, tc, xpad_ref, act_ref, cw_ref):
    win = PAIR + 2 * HALO
    for cb in range(D_CONV // LANES):
        cols = slice(cb * LANES, (cb + 1) * LANES)
        wj = [cw_ref[j:j + 1, cols] for j in range(CONV_W)]
        for r0 in range(t0, t0 + tc, PAIR):
            xw = xpad_ref[r0:r0 + win, cols]
            up = pltpu.roll(wj[3] * xw + pltpu.roll(wj[4] * xw, win - 1, axis=0), win - 1, axis=0)
            dn = pltpu.roll(wj[1] * xw + pltpu.roll(wj[0] * xw, 1, axis=0), 1, axis=0)
            acc = wj[2] * xw + up + dn
            act_ref[r0:r0 + PAIR, cols] = _silu(acc[HALO:HALO + PAIR])


def _gdn_prep(t0, tc, act_ref, gb_ref, gbt_ref, w_ref, qg_ref, u_ref, qk_ref, kgt_ref, gl_ref):
    row = lax.broadcasted_iota(jnp.int32, (PAIR, PAIR), 0)
    col = lax.broadcasted_iota(jnp.int32, (PAIR, PAIR), 1)
    same = (row >= DN_CHUNK) == (col >= DN_CHUNK)
    incl = (same & (row >= col), same & (row <= col))
    strict = (same & (row > col), same & (row < col))
    as01 = lambda m: jnp.where(m, 1.0, 0.0).astype(BF16)
    m_incl = tuple(as01(m) for m in incl)
    m_same = as01(same)
    e_chunk = (as01(row < DN_CHUNK), as01(row >= DN_CHUNK))
    na = N_DIR * B_HEADS

    chains, a_negs, rhss = [], [], []
    for p in range(t0 // PAIR, (t0 + tc) // PAIR):
        rows = slice(p * PAIR, (p + 1) * PAIR)
        gbp = gb_ref[rows, :]
        gbtp = gbt_ref[:, rows]
        gcol = tuple(_dot_exact_lhs01(m_incl[d], gbp) for d in range(N_DIR))
        grow = tuple(_dot_exact_rhs01(gbtp, m_incl[1 - d]) for d in range(N_DIR))
        tot_row = _dot_exact_rhs01(gbtp, m_same)
        for c in range(2):
            gl_ref[2 * p + c] = jnp.exp(_dot_exact_rhs01(gbtp, e_chunk[c]))

        heads = []
        for h in range(B_HEADS):
            q = act_ref[rows, h * B_DK:(h + 1) * B_DK]
            k = act_ref[rows, D_QK + h * B_DK:D_QK + (h + 1) * B_DK]
            qn = q * lax.rsqrt(jnp.sum(q * q, axis=-1, keepdims=True) + EPS) * (B_DK ** -0.5)
            kn = k * lax.rsqrt(jnp.sum(k * k, axis=-1, keepdims=True) + EPS)
            kt = kn.T
            heads.append((qn, kn, kt, kt.astype(BF16)))
        kks = [_dot(kn.astype(BF16), kt16) for _, kn, _, kt16 in heads]
        qks = [_dot(qn.astype(BF16), kt16) for qn, _, _, kt16 in heads]

        for d, h in [(d, h) for d in range(N_DIR) for h in range(B_HEADS)]:
            chains.append((rows, d, h))
            ci = d * B_HEADS + h
            lanes = slice(h * LANES, (h + 1) * LANES)
            qn, kn, kt, _ = heads[h]
            v = act_ref[rows, 2 * D_QK + h * B_DV:2 * D_QK + (h + 1) * B_DV]
            gc = gcol[d][:, ci:ci + 1]
            gr = grow[d][ci:ci + 1, :]
            beta = gbp[:, na + ci:na + ci + 1]
            decay = jnp.where(incl[d], jnp.exp(gc - gr), 0.0)
            a_negs.append(jnp.where(strict[d], -(kks[h] * beta * decay), 0.0))
            eg = jnp.exp(gc)
            rhss.append(jnp.concatenate([v * beta, kn * (beta * eg)], axis=1).astype(BF16))
            qk_ref[d, rows, lanes] = (qks[h] * decay).astype(BF16)
            qg_ref[d, rows, lanes] = (qn * eg).astype(BF16)
            kgt_ref[d, rows, lanes] = (kt * jnp.exp(tot_row[ci:ci + 1, :] - gr)).astype(BF16)

    tinvs = _unit_tri_inverses(a_negs)
    uws = [_dot(t.astype(BF16), rhs) for t, rhs in zip(tinvs, rhss)]
    for (rows, d, h), uw in zip(chains, uws):
        lanes = slice(h * LANES, (h + 1) * LANES)
        u_ref[d, rows, lanes] = uw[:, :B_DV]
        w_ref[d, rows, lanes] = uw[:, B_DV:].astype(BF16)


def _stage3_kernel(wf, qgf, uf, qkf, kgf, wb, qgb, ub, qkb, kgb, glf, glb, of_ref, ob_ref, s_ref):
    i = pl.program_id(1)

    @pl.when(i == 0)
    def _():
        s_ref[...] = jnp.zeros_like(s_ref)

    nseq, rows_blk = wf.shape[1], wf.shape[2]
    npairs = rows_blk // PAIR
    zpad = jnp.zeros((DN_CHUNK, B_DV), BF16)
    per_dir = ((wf, qgf, uf, qkf, kgf, glf, of_ref), (wb, qgb, ub, qkb, kgb, glb, ob_ref))
    chains = [(q, d, h) for q in range(nseq) for d in range(N_DIR) for h in range(B_HEADS)]
    for step in range(2 * npairs):
        def where(d):
            chunk = step if d == 0 else 2 * npairs - 1 - step
            pair = chunk // 2
            return (chunk, slice(chunk * DN_CHUNK, (chunk + 1) * DN_CHUNK),
                    slice(pair * PAIR, (pair + 1) * PAIR))

        states, m1s, m2s = [], [], []
        for q, d, h in chains:
            w_r, qg_r = per_dir[d][0], per_dir[d][1]
            _, rows, _ = where(d)
            lanes = slice(h * LANES, (h + 1) * LANES)
            s = s_ref[q, d, h]
            states.append(s)
            lhs1 = jnp.concatenate([w_r[0, q, rows, lanes], qg_r[0, q, rows, lanes]], axis=0)
            m1s.append(_dot(lhs1, s.astype(BF16)))
        for (q, d, h), m1 in zip(chains, m1s):
            u_r, qk_r, kg_r = per_dir[d][2], per_dir[d][3], per_dir[d][4]
            chunk, rows, prow = where(d)
            lanes = slice(h * LANES, (h + 1) * LANES)
            v_new = (u_r[0, q, rows, lanes] - m1[:DN_CHUNK]).astype(BF16)
            v_pad = jnp.concatenate([v_new, zpad] if chunk % 2 == 0 else [zpad, v_new], axis=0)
            lhs2 = jnp.concatenate([qk_r[0, q, rows, lanes], kg_r[0, q, prow, lanes]], axis=0)
            m2s.append(_dot(lhs2, v_pad))
        for (q, d, h), s, m1, m2 in zip(chains, states, m1s, m2s):
            gl_r, o_r = per_dir[d][5], per_dir[d][6]
            chunk, rows, _ = where(d)
            lanes = slice(h * LANES, (h + 1) * LANES)
            o_r[q, rows, lanes] = m1[DN_CHUNK:] + m2[:DN_CHUNK]
            ci = d * B_HEADS + h
            s_ref[q, d, h] = s * gl_r[q, chunk, ci:ci + 1, :] + m2[DN_CHUNK:]


def _stage3(w, qg, u, qk, kgt, gl, rows_blk):
    _, b, s, _ = w.shape
    n_i = s // rows_blk
    cpb = rows_blk // DN_CHUNK
    nseq = 2 if b % 2 == 0 else 1
    fwd = pl.BlockSpec((1, nseq, rows_blk, D_B), lambda bi, i: (0, bi, i, 0))
    bwd = pl.BlockSpec((1, nseq, rows_blk, D_B), lambda bi, i: (1, bi, n_i - 1 - i, 0))
    return pl.pallas_call(
        _stage3_kernel,
        grid=(b // nseq, n_i),
        in_specs=[fwd] * 5 + [bwd] * 5 + [
            pl.BlockSpec((nseq, cpb, N_GB, LANES), lambda bi, i: (bi, i, 0, 0)),
            pl.BlockSpec((nseq, cpb, N_GB, LANES), lambda bi, i: (bi, n_i - 1 - i, 0, 0)),
        ],
        out_specs=[pl.BlockSpec((nseq, rows_blk, D_B), lambda bi, i: (bi, i, 0)),
                   pl.BlockSpec((nseq, rows_blk, D_B), lambda bi, i: (bi, n_i - 1 - i, 0))],
        out_shape=[jax.ShapeDtypeStruct((b, s, D_B), F32), jax.ShapeDtypeStruct((b, s, D_B), F32)],
        scratch_shapes=[pltpu.VMEM((nseq, N_DIR, B_HEADS, B_DK, B_DV), F32)],
        compiler_params=pltpu.CompilerParams(
            dimension_semantics=("arbitrary", "arbitrary"), vmem_limit_bytes=VMEM_LIMIT),
        name="stage3_gdn_scan",
    )(w, qg, u, qk, kgt, w, qg, u, qk, kgt, gl, gl)


N_ROUTE = N_EXPERT_GROUPS + N_EXPERTS


def _stage4_kernel(x_ref, ya_ref, of_ref, ob_ref, z_ref, gnw_ref, woa_ref, wob_ref, fnw_ref,
                   wr_ref, h_ref, hn_ref, ids_ref, wts_ref, cnt_ref):
    tm = x_ref.shape[0]
    for k in range(tm // MOE_TM):
        cnt_ref[k] = _stage4_rows(slice(k * MOE_TM, (k + 1) * MOE_TM), x_ref, ya_ref, of_ref,
                                  ob_ref, z_ref, gnw_ref, woa_ref, wob_ref, fnw_ref, wr_ref,
                                  h_ref, hn_ref, ids_ref, wts_ref)


def _stage4_rows(sl, x_ref, ya_ref, of_ref, ob_ref, z_ref, gnw_ref, woa_ref, wob_ref, fnw_ref,
                 wr_ref, h_ref, hn_ref, ids_ref, wts_ref):
    tm = sl.stop - sl.start
    o = of_ref[sl, :] + ob_ref[sl, :]
    z = z_ref[sl, :]
    parts = []
    for hd in range(B_HEADS):
        lanes = slice(hd * B_DV, (hd + 1) * B_DV)
        oh = o[:, lanes]
        yh = oh * lax.rsqrt(jnp.mean(oh * oh, axis=-1, keepdims=True) + EPS) * gnw_ref[...]
        parts.append((yh * _silu(z[:, lanes])).astype(BF16))
    yb = jnp.concatenate(parts, axis=1)
    h = x_ref[sl, :] + (_dot(ya_ref[sl, :], woa_ref[...]) + _dot(yb, wob_ref[...]))
    h_ref[sl, :] = h
    hn = h * lax.rsqrt(jnp.mean(h * h, axis=-1, keepdims=True) + EPS) * fnw_ref[...]

    hi = hn.astype(BF16)
    hn_ref[sl, :] = hi
    lo = (hn - hi.astype(F32)).astype(BF16)
    prod = _dot(jnp.concatenate([hi, lo], axis=0), wr_ref[...])
    logits = ((prod[:tm, :LANES] + prod[tm:, :LANES])
              + (prod[:tm, LANES:] + prod[tm:, LANES:]))

    lane = lax.broadcasted_iota(jnp.int32, (tm, LANES), 1)
    neg = -jnp.inf
    is_g = lane < N_EXPERT_GROUPS
    gl = jnp.where(is_g, logits, neg)
    gmax = jnp.max(gl, axis=-1, keepdims=True)
    gidx = jnp.min(jnp.where(gl == gmax, lane, LANES), axis=-1, keepdims=True)
    g_w = 1.0 / jnp.sum(jnp.where(is_g, jnp.exp(gl - gmax), 0.0), axis=-1, keepdims=True)

    e0 = N_EXPERT_GROUPS + gidx * EXPERTS_PER_GROUP
    in_grp = (lane >= e0) & (lane < e0 + EXPERTS_PER_GROUP)
    el = jnp.where(in_grp, logits, neg)
    m1 = jnp.max(el, axis=-1, keepdims=True)
    i1 = jnp.min(jnp.where(el == m1, lane, LANES), axis=-1, keepdims=True)
    el2 = jnp.where(lane == i1, neg, el)
    m2 = jnp.max(el2, axis=-1, keepdims=True)
    i2 = jnp.min(jnp.where(el2 == m2, lane, LANES), axis=-1, keepdims=True)
    e2 = jnp.exp(m2 - m1)
    inv = 1.0 / (1.0 + e2)
    ids_ref[sl, 0:1] = i1 - N_EXPERT_GROUPS
    ids_ref[sl, 1:2] = i2 - N_EXPERT_GROUPS
    wts_ref[sl, 0:1] = g_w * inv
    wts_ref[sl, 1:2] = g_w * (e2 * inv)
    elane = lane + N_EXPERT_GROUPS
    chosen = (elane == i1) | (elane == i2)
    return jnp.sum(jnp.where(chosen, 1.0, 0.0), axis=0, keepdims=True)


def _stage4(x, ya, o_f, o_b, z, gnw, woa, wob, fnw, wr, tm):
    t = x.shape[0]
    full = lambda a: pl.BlockSpec(a.shape, lambda i: (0,) * a.ndim)
    tile = lambda n: pl.BlockSpec((tm, n), lambda i: (i, 0))
    return pl.pallas_call(
        _stage4_kernel,
        grid=(t // tm,),
        in_specs=[tile(D_MODEL), tile(D_A), tile(D_B), tile(D_B), tile(D_B),
                  full(gnw), full(woa), full(wob), full(fnw), full(wr)],
        out_specs=[tile(D_MODEL), tile(D_MODEL), tile(TOP_K), tile(TOP_K),
                   pl.BlockSpec((tm // MOE_TM, 1, LANES), lambda i: (i, 0, 0))],
        out_shape=[jax.ShapeDtypeStruct((t, D_MODEL), F32), jax.ShapeDtypeStruct((t, D_MODEL), BF16),
                   jax.ShapeDtypeStruct((t, TOP_K), jnp.int32),
                   jax.ShapeDtypeStruct((t, TOP_K), F32),
                   jax.ShapeDtypeStruct((t // MOE_TM, 1, LANES), F32)],
        compiler_params=pltpu.CompilerParams(
            dimension_semantics=("arbitrary",), vmem_limit_bytes=VMEM_LIMIT),
        name="stage4_outproj_router",
    )(x, ya, o_f, o_b, z, gnw, woa, wob, fnw, wr)


MOE_TM = 512
MOE_R = 16
MOE_BM = 512
MOE_L = TOP_K * MOE_TM + N_EXPERTS * MOE_R


def _moe_plan(cnt):
    n_tiles = cnt.shape[0]
    c = cnt[:, :N_EXPERTS].astype(jnp.int32)
    cpad = (c + MOE_R - 1) // MOE_R * MOE_R
    seg = jnp.sum(cpad, axis=0)
    segpad = (seg + MOE_BM - 1) // MOE_BM * MOE_BM
    pad_end = jnp.cumsum(segpad)
    pad_start = pad_end - segpad
    off = pad_start[None, :] + jnp.cumsum(cpad, axis=0) - cpad
    loc = jnp.cumsum(cpad, axis=1) - cpad
    nch = cpad // MOE_R
    n_blocks = -(-(TOP_K * MOE_TM + N_EXPERTS * (MOE_R - 1)) * n_tiles // MOE_BM) + N_EXPERTS
    n_used = pad_end[-1] // MOE_BM
    blk = jnp.arange(n_blocks, dtype=jnp.int32)
    first_row = jnp.minimum(blk, n_used - 1) * MOE_BM
    block_expert = jnp.sum((pad_end[None, :] <= first_row[:, None]).astype(jnp.int32), axis=1)
    block_expert = jnp.minimum(block_expert, N_EXPERTS - 1)
    loc_lanes = jnp.pad(loc, ((0, 0), (0, LANES - N_EXPERTS))).astype(F32)
    return dict(off=off.reshape(-1), loc=loc.reshape(-1), nch=nch.reshape(-1),
                tot=jnp.sum(nch, axis=1), tail_off=pad_start + seg,
                tail_n=(segpad - seg) // MOE_R, loc_lanes=loc_lanes.reshape(n_tiles, 1, LANES),
                block_expert=block_expert, n_used=n_used.reshape(1), n_blocks=n_blocks)


def _run_chunks(nch_ref, loc_ref, off_ref, tile, visit):
    def per_expert(e, carry):
        idx = tile * N_EXPERTS + e
        loc0 = loc_ref[idx]
        off0 = off_ref[idx]
        n = nch_ref[idx]

        def per_double(j, c):
            step = j * (2 * MOE_R)
            visit(pl.multiple_of(loc0 + step, MOE_R), pl.multiple_of(off0 + step, MOE_R),
                  2 * MOE_R)
            return c
        lax.fori_loop(0, n >> 1, per_double, 0)

        @pl.when((n & 1) == 1)
        def _():
            last = (n - 1) * MOE_R
            visit(pl.multiple_of(loc0 + last, MOE_R), pl.multiple_of(off0 + last, MOE_R), MOE_R)
        return carry
    lax.fori_loop(0, N_EXPERTS, per_expert, 0)


def _wait_chunks(count, chunk_wait):
    for bit in range((MOE_L // MOE_R).bit_length()):
        @pl.when(((count >> bit) & 1) == 1)
        def _():
            chunk_wait((1 << bit) * MOE_R)


def _dispatch_kernel(n, n_first, n_blocks, off_ref, loc_ref, nch_ref, tot_ref, toff_ref, tn_ref,
                     nu_ref, ids_ref, hna_ref, hnb_ref, locl_ref, xs_ref, pos_ref,
                     hn_ref, xl, zrows, sem, tsem):
    i = pl.program_id(0)
    g = i
    slot = lax.rem(i, 2)
    tm = ids_ref.shape[0]

    @pl.when(i < n_first)
    def _():
        hn_ref[...] = hna_ref[...]

    @pl.when(i >= n_first)
    def _():
        hn_ref[...] = hnb_ref[...]

    ids = ids_ref[...]
    lane = lax.broadcasted_iota(jnp.int32, (tm, LANES), 1)
    oh0 = jnp.where(lane == ids[:, 0:1], 1.0, 0.0)
    oh1 = jnp.where(lane == ids[:, 1:2], 1.0, 0.0)
    row = lax.broadcasted_iota(jnp.int32, (tm, tm), 0)
    col = lax.broadcasted_iota(jnp.int32, (tm, tm), 1)
    earlier = jnp.where(row > col, 1.0, 0.0).astype(BF16)
    base = _dot(earlier, (oh0 + oh1).astype(BF16)) + locl_ref[0]
    m0 = base * oh0
    m1 = base * oh1
    pos_ref[:, 0:1] = jnp.sum(m0, axis=-1, keepdims=True).astype(jnp.int32)
    pos_ref[:, 1:2] = jnp.sum(m1, axis=-1, keepdims=True).astype(jnp.int32)
    ones = jnp.ones((8, LANES), BF16)
    lane_form = lambda m: sum(_dot_nt(ones, part) for part in _split3(m))[0:1].astype(jnp.int32)
    p0 = lane_form(m0)
    p1 = lane_form(m1)
    srow = lax.broadcasted_iota(jnp.int32, (MOE_L, tm), 0)
    perm = jnp.where((srow == p0) | (srow == p1), 1.0, 0.0).astype(BF16)
    nb = 256
    for cb in range(D_MODEL // nb):
        xl[slot, :, cb * nb:(cb + 1) * nb] = _dot(perm, hn_ref[:, cb * nb:(cb + 1) * nb]).astype(BF16)

    def rows_copy(sl, lrow, grow, nrows):
        return pltpu.make_async_copy(xl.at[sl, pl.ds(lrow, nrows)],
                                     xs_ref.at[pl.ds(grow, nrows)], sem.at[sl])

    _run_chunks(nch_ref, loc_ref, off_ref, g,
                lambda lrow, grow, nrows: rows_copy(slot, lrow, grow, nrows).start())

    def wait_tile(tile, sl):
        _wait_chunks(tot_ref[tile], lambda nrows: rows_copy(sl, 0, 0, nrows).wait())

    @pl.when(i > 0)
    def _():
        wait_tile(g - 1, 1 - slot)

    @pl.when(i == n - 1)
    def _():
        wait_tile(g, slot)
        zrows[...] = jnp.zeros_like(zrows)

        def tail_copy(e, j):
            row0 = pl.multiple_of(toff_ref[e] + j * MOE_R, MOE_R)
            return pltpu.make_async_copy(zrows.at[pl.ds(0, MOE_R)],
                                         xs_ref.at[pl.ds(row0, MOE_R)], tsem.at[0])

        def block_copy(b):
            row0 = pl.multiple_of(b * MOE_BM, MOE_BM)
            return pltpu.make_async_copy(zrows, xs_ref.at[pl.ds(row0, MOE_BM)], tsem.at[0])

        def fill(act):
            def per_expert(e, carry):
                def per_chunk(j, c):
                    act(tail_copy(e, j))
                    return c
                lax.fori_loop(0, tn_ref[e], per_chunk, 0)
                return carry
            lax.fori_loop(0, N_EXPERTS, per_expert, 0)

            def per_block(b, c):
                act(block_copy(b))
                return c
            lax.fori_loop(nu_ref[0], n_blocks, per_block, 0)

        fill(lambda cp: cp.start())
        fill(lambda cp: cp.wait())


def _dispatch(plan, ids, hn_a, hn_b):
    n_a = hn_a.shape[0] // MOE_TM
    n = ids.shape[0] // MOE_TM
    n_blocks = plan['n_blocks']
    grid_spec = pltpu.PrefetchScalarGridSpec(
        num_scalar_prefetch=7,
        grid=(n,),
        in_specs=[
            pl.BlockSpec((MOE_TM, TOP_K), lambda i, *_: (i, 0)),
            pl.BlockSpec((MOE_TM, D_MODEL), lambda i, *_: (jnp.minimum(i, n_a - 1), 0)),
            pl.BlockSpec((MOE_TM, D_MODEL), lambda i, *_: (jnp.maximum(i - n_a, 0), 0)),
            pl.BlockSpec((1, 1, LANES), lambda i, *_: (i, 0, 0)),
        ],
        out_specs=[pl.BlockSpec(memory_space=pl.ANY),
                   pl.BlockSpec((MOE_TM, TOP_K), lambda i, *_: (i, 0))],
        scratch_shapes=[pltpu.VMEM((MOE_TM, D_MODEL), BF16),
                        pltpu.VMEM((2, MOE_L, D_MODEL), BF16), pltpu.VMEM((MOE_BM, D_MODEL), BF16),
                        pltpu.SemaphoreType.DMA((2,)), pltpu.SemaphoreType.DMA((1,))],
    )
    return pl.pallas_call(
        functools.partial(_dispatch_kernel, n, n_a, n_blocks),
        grid_spec=grid_spec,
        out_shape=[jax.ShapeDtypeStruct((n_blocks * MOE_BM, D_MODEL), BF16),
                   jax.ShapeDtypeStruct((ids.shape[0], TOP_K), jnp.int32)],
        compiler_params=pltpu.CompilerParams(
            dimension_semantics=("arbitrary",), vmem_limit_bytes=VMEM_LIMIT),
        name="stage5a_dispatch",
    )(plan['off'], plan['loc'], plan['nch'], plan['tot'], plan['tail_off'], plan['tail_n'],
      plan['n_used'], ids, hn_a, hn_b, plan['loc_lanes'])


def _expert_kernel(be_ref, nu_ref, xs_ref, wg_ref, wu_ref, wd_ref, y_ref, wg16, wu16, wd16):
    b = pl.program_id(0)

    @pl.when((b == 0) | (be_ref[b] != be_ref[jnp.maximum(b - 1, 0)]))
    def _():
        wg16[...] = wg_ref[0].astype(BF16)
        wu16[...] = wu_ref[0].astype(BF16)
        wd16[...] = wd_ref[0].astype(BF16)

    @pl.when(b < nu_ref[0])
    def _():
        x = xs_ref[...]
        g = _dot(x, wg16[...])
        u = _dot(x, wu16[...])
        y_ref[...] = _dot((_silu(g) * u).astype(BF16), wd16[...]).astype(BF16)

    @pl.when(b >= nu_ref[0])
    def _():
        y_ref[...] = jnp.zeros_like(y_ref)


def _experts(plan, xs, wg, wu, wd):
    n_blocks = plan['n_blocks']
    used = lambda b, nu: jnp.minimum(b, nu[0] - 1)
    grid_spec = pltpu.PrefetchScalarGridSpec(
        num_scalar_prefetch=2,
        grid=(n_blocks,),
        in_specs=[
            pl.BlockSpec((MOE_BM, D_MODEL), lambda b, be, nu: (used(b, nu), 0)),
            pl.BlockSpec((1, D_MODEL, D_EXPERT), lambda b, be, nu: (be[b], 0, 0)),
            pl.BlockSpec((1, D_MODEL, D_EXPERT), lambda b, be, nu: (be[b], 0, 0)),
            pl.BlockSpec((1, D_EXPERT, D_MODEL), lambda b, be, nu: (be[b], 0, 0)),
        ],
        out_specs=pl.BlockSpec((MOE_BM, D_MODEL), lambda b, be, nu: (b, 0)),
        scratch_shapes=[pltpu.VMEM((D_MODEL, D_EXPERT), BF16), pltpu.VMEM((D_MODEL, D_EXPERT), BF16),
                        pltpu.VMEM((D_EXPERT, D_MODEL), BF16)],
    )
    return pl.pallas_call(
        _expert_kernel,
        grid_spec=grid_spec,
        out_shape=jax.ShapeDtypeStruct((n_blocks * MOE_BM, D_MODEL), BF16),
        compiler_params=pltpu.CompilerParams(
            dimension_semantics=("arbitrary",), vmem_limit_bytes=VMEM_LIMIT),
        name="stage5b_experts",
    )(plan['block_expert'], plan['n_used'], xs, wg, wu, wd)


def _combine_kernel(n, tile_base, off_ref, loc_ref, nch_ref, tot_ref, pos_ref, wts_ref, h_ref,
                    fw_ref, y_hbm, out_ref, yl, sem):
    i = pl.program_id(0)
    g = tile_base + i
    slot = lax.rem(i, 2)
    tm = h_ref.shape[0]

    def rows_copy(sl, lrow, grow, nrows):
        return pltpu.make_async_copy(y_hbm.at[pl.ds(grow, nrows)],
                                     yl.at[sl, pl.ds(lrow, nrows)], sem.at[sl])

    def fetch(tile, sl):
        _run_chunks(nch_ref, loc_ref, off_ref, tile,
                    lambda lrow, grow, nrows: rows_copy(sl, lrow, grow, nrows).start())

    @pl.when(i == 0)
    def _():
        yl[...] = jnp.zeros_like(yl)
        fetch(g, 0)

    @pl.when(i + 1 < n)
    def _():
        fetch(g + 1, 1 - slot)

    _wait_chunks(tot_ref[g], lambda nrows: rows_copy(slot, 0, 0, nrows).wait())

    lane = lax.broadcasted_iota(jnp.int32, (tm, MOE_L), 1)
    pos = pos_ref[...]
    w = wts_ref[...]
    sel = (jnp.where(lane == pos[:, 0:1], w[:, 0:1], 0.0)
           + jnp.where(lane == pos[:, 1:2], w[:, 1:2], 0.0)).astype(BF16)
    h = h_ref[...] + _dot(sel, yl[slot])
    out_ref[...] = h * lax.rsqrt(jnp.mean(h * h, axis=-1, keepdims=True) + EPS) * fw_ref[...]


def _combine(plan, tile_base, pos, wts, h, fw, y_rows):
    t = h.shape[0]
    n = t // MOE_TM
    grid_spec = pltpu.PrefetchScalarGridSpec(
        num_scalar_prefetch=4,
        grid=(n,),
        in_specs=[
            pl.BlockSpec((MOE_TM, TOP_K), lambda i, *_: (i, 0)),
            pl.BlockSpec((MOE_TM, TOP_K), lambda i, *_: (i, 0)),
            pl.BlockSpec((MOE_TM, D_MODEL), lambda i, *_: (i, 0)),
            pl.BlockSpec((1, D_MODEL), lambda i, *_: (0, 0)),
            pl.BlockSpec(memory_space=pl.ANY),
        ],
        out_specs=pl.BlockSpec((MOE_TM, D_MODEL), lambda i, *_: (i, 0)),
        scratch_shapes=[pltpu.VMEM((2, MOE_L, D_MODEL), BF16), pltpu.SemaphoreType.DMA((2,))],
    )
    return pl.pallas_call(
        functools.partial(_combine_kernel, n, tile_base),
        grid_spec=grid_spec,
        out_shape=jax.ShapeDtypeStruct((t, D_MODEL), F32),
        compiler_params=pltpu.CompilerParams(
            dimension_semantics=("arbitrary",), vmem_limit_bytes=VMEM_LIMIT),
        name="stage6_combine_norm",
    )(plan['off'], plan['loc'], plan['nch'], plan['tot'], pos, wts, h, fw, y_rows)


def kernel(x_prompt, x_sample, norm_mix_w, w_in, a_ln_w, a_ln_b, a_spatial_w, a_spatial_b, a_out_norm_w, conv_w, a_log, dt_bias, gdn_norm_w, w_out, norm_ffn_w, w_router_group, w_router_expert, w_gate, w_up, w_down, norm_final_w):
    s1w = _prep_stage1_weights(norm_mix_w[0], w_in[0], a_ln_w[0], a_ln_b[0], a_spatial_w[0],
                               a_spatial_b[0], a_out_norm_w[0], a_log[0], dt_bias[0])
    woa = w_out[0, :D_A].astype(BF16)
    wob = w_out[0, D_A:].astype(BF16)
    w_r = jnp.concatenate([w_router_group[0], w_router_expert[0]], axis=1)
    w_r = jnp.pad(w_r, ((0, 0), (0, LANES - N_ROUTE)))
    wrh = w_r.astype(BF16)
    wr = jnp.concatenate([wrh, (w_r - wrh.astype(F32)).astype(BF16)], axis=1)
    gnw = gdn_norm_w[0].reshape(1, B_DV)
    fnw = norm_ffn_w[0].reshape(1, D_MODEL)

    per_run = []
    for x in (x_prompt, x_sample):
        b, s, d = x.shape
        x2 = x.reshape(b * s, d)
        ya, z, w, qg, u, qk, kgt, gl = _stage1(x2, s, *s1w, conv_w[0], tm=512)
        per_seq = lambda a: a.reshape(N_DIR, b, s, D_B)
        o_f, o_b = _stage3(per_seq(w), per_seq(qg), per_seq(u), per_seq(qk), per_seq(kgt),
                           gl.reshape(b, s // DN_CHUNK, N_GB, LANES), rows_blk=512)
        h, hn, ids, wts, cnt = _stage4(x2, ya, o_f.reshape(b * s, D_B), o_b.reshape(b * s, D_B),
                                       z, gnw, woa, wob, fnw, wr, tm=2 * MOE_TM)
        per_run.append(dict(shape=x.shape, h=h, hn=hn, ids=ids, wts=wts, cnt=cnt[:, 0, :]))

    plan = _moe_plan(jnp.concatenate([r['cnt'] for r in per_run], axis=0))
    ids_all = jnp.concatenate([r['ids'] for r in per_run], axis=0)
    xs, pos_all = _dispatch(plan, ids_all, per_run[0]['hn'], per_run[1]['hn'])
    y_rows = _experts(plan, xs, w_gate[0], w_up[0], w_down[0])
    fw = norm_final_w.reshape(1, D_MODEL)
    outs = []
    t0 = 0
    for r in per_run:
        t = r['h'].shape[0]
        outs.append(_combine(plan, t0 // MOE_TM, pos_all[t0:t0 + t], r['wts'], r['h'], fw,
                             y_rows).reshape(r['shape']))
        t0 += t
    return tuple(outs)
```

```python
import functools
import math

import jax
import jax.numpy as jnp
from jax import lax
from jax.experimental import pallas as pl
from jax.experimental.pallas import tpu as pltpu

D_MODEL = 1024
D_A = 512
A_GROUPS = 4
A_DG = 128
A_CHUNK = 128
D_B = 512
B_HEADS = 4
B_DK = 128
B_DV = 128
D_QK = 512
DN_CHUNK = 64
CONV_W = 5
N_DIR = 2
N_EXPERT_GROUPS = 4
EXPERTS_PER_GROUP = 8
N_EXPERTS = 32
TOP_K = 2
D_EXPERT = 512
EPS = 1e-6
D_CONV = 2 * D_QK + D_B
N_GB = 2 * N_DIR * B_HEADS

LANES = 128
VMEM_LIMIT = 48 * 1024 * 1024

BF16 = jnp.bfloat16
F32 = jnp.float32


def _dot(a, b):
    return jnp.dot(a, b, preferred_element_type=F32)


def _dot_nt(a, b):
    return lax.dot_general(a, b, (((1,), (1,)), ((), ())), preferred_element_type=F32)


def _gelu_tanh(x):
    c = math.sqrt(2.0 / math.pi)
    return x * (0.5 * (1.0 + jnp.tanh(c * (x + 0.044715 * (x * x * x)))))


def _sigmoid(x):
    return 0.5 * (1.0 + jnp.tanh(0.5 * x))


def _silu(x):
    return x * _sigmoid(x)


def _softplus(x):
    return jnp.maximum(x, 0.0) + jnp.log(1.0 + jnp.exp(-jnp.abs(x)))


MIXER_TM = 512
GDN_TC = 256


def _stage1_kernel(tiles_per_seq, x_ref, xp_ref, xn_ref, nw_ref, wuv_ref, wqkv_ref,
                   wz_ref, wgb_ref, wgbt_ref, lnw_ref, lnb_ref, spw_ref, spbt_ref, onw_ref,
                   alog_ref, dtb_ref, alogt_ref, dtbt_ref, cw_ref,
                   ya_ref, z_ref, w_ref, qg_ref, u_ref, qk_ref, kgt_ref, gl_ref,
                   xpad_ref, act_ref, gb_ref, gbt_ref):
    tm = x_ref.shape[0]
    na = N_DIR * B_HEADS
    seq_pos = lax.rem(pl.program_id(0), tiles_per_seq)

    def normed(x):
        xn = x * lax.rsqrt(jnp.mean(x * x, axis=-1, keepdims=True) + EPS) * nw_ref[...]
        return xn.astype(BF16)

    xb = normed(x_ref[...])
    halo = normed(jnp.concatenate([xp_ref[...], xn_ref[...]], axis=0))
    qkv = _dot(jnp.concatenate([xb, halo], axis=0), wqkv_ref[...])
    xpad_ref[0:HALO, :] = jnp.where(seq_pos > 0, qkv[tm:tm + HALO], 0.0)
    xpad_ref[HALO:HALO + tm, :] = qkv[:tm]
    xpad_ref[HALO + tm:, :] = jnp.where(seq_pos < tiles_per_seq - 1, qkv[tm + HALO:], 0.0)

    ab = _dot(xb, wgb_ref[...])[:, :N_GB]
    abt = _dot_nt(wgbt_ref[...], xb)
    gb_ref[:, :na] = -jnp.exp(alog_ref[...]) * _softplus(ab[:, :na] + dtb_ref[...])
    gb_ref[:, na:] = _sigmoid(ab[:, na:])
    gbt_ref[:na, :] = -jnp.exp(alogt_ref[...]) * _softplus(abt[:na, :] + dtbt_ref[...])
    gbt_ref[na:, :] = _sigmoid(abt[na:, :])

    def gmlp_group(grp):
        cols = slice(grp * A_DG, (grp + 1) * A_DG)
        uv_all = _dot(xb, wuv_ref[:, 2 * grp * A_DG:2 * (grp + 1) * A_DG])
        u_all = uv_all[:, :A_DG]
        v_all = uv_all[:, A_DG:]
        spw = spw_ref[grp]
        for c in range(tm // A_CHUNK):
            rows = slice(c * A_CHUNK, (c + 1) * A_CHUNK)
            u = _gelu_tanh(u_all[rows])
            v = _gelu_tanh(v_all[rows])
            mu = jnp.mean(v, axis=-1, keepdims=True)
            vc = v - mu
            var = jnp.mean(vc * vc, axis=-1, keepdims=True)
            vn = vc * lax.rsqrt(var + EPS) * lnw_ref[:, cols] + lnb_ref[:, cols]
            mixed = _dot(spw, vn.astype(BF16)) + spbt_ref[:, grp:grp + 1]
            gated = u * mixed
            out = gated * lax.rsqrt(jnp.mean(gated * gated, axis=-1, keepdims=True) + EPS)
            ya_ref[rows, cols] = (out * onw_ref[:, cols]).astype(BF16)

    n_sub = tm // GDN_TC
    for k in range(n_sub):
        _short_conv_silu(k * GDN_TC, GDN_TC, xpad_ref, act_ref, cw_ref)
        if k == 0:
            z_ref[...] = _dot(xb, wz_ref[...])
        _gdn_prep(k * GDN_TC, GDN_TC, act_ref, gb_ref, gbt_ref,
                  w_ref, qg_ref, u_ref, qk_ref, kgt_ref, gl_ref)
        for grp in range(k * A_GROUPS // n_sub, (k + 1) * A_GROUPS // n_sub):
            gmlp_group(grp)


def _stage1(x, seq_len, nw, wuv, wqkv, wz, wgb, wgbt, lnw, lnb, spw, spbt, onw, alog, dtb,
            alogt, dtbt, conv_w, tm):
    t = x.shape[0]
    hb = tm // HALO
    consts = (nw, wuv, wqkv, wz, wgb, wgbt, lnw, lnb, spw, spbt, onw, alog, dtb, alogt, dtbt,
              conv_w)
    full = lambda a: pl.BlockSpec(a.shape, lambda i: (0,) * a.ndim)
    dirs = pl.BlockSpec((N_DIR, tm, D_B), lambda i: (0, i, 0))
    dir_shape = lambda dtype: jax.ShapeDtypeStruct((N_DIR, t, D_B), dtype)
    return pl.pallas_call(
        functools.partial(_stage1_kernel, seq_len // tm),
        grid=(t // tm,),
        in_specs=[
            pl.BlockSpec((tm, D_MODEL), lambda i: (i, 0)),
            pl.BlockSpec((HALO, D_MODEL), lambda i: (jnp.maximum(i * hb - 1, 0), 0)),
            pl.BlockSpec((HALO, D_MODEL), lambda i: (jnp.minimum((i + 1) * hb, t // HALO - 1), 0)),
        ] + [full(a) for a in consts],
        out_specs=[
            pl.BlockSpec((tm, D_A), lambda i: (i, 0)),
            pl.BlockSpec((tm, D_B), lambda i: (i, 0)),
            dirs, dirs, dirs, dirs, dirs,
            pl.BlockSpec((tm // DN_CHUNK, N_GB, LANES), lambda i: (i, 0, 0)),
        ],
        out_shape=[
            jax.ShapeDtypeStruct((t, D_A), BF16),
            jax.ShapeDtypeStruct((t, D_B), F32),
            dir_shape(BF16), dir_shape(BF16), dir_shape(F32), dir_shape(BF16), dir_shape(BF16),
            jax.ShapeDtypeStruct((t // DN_CHUNK, N_GB, LANES), F32),
        ],
        scratch_shapes=[pltpu.VMEM((tm + 2 * HALO, D_CONV), F32), pltpu.VMEM((tm, D_CONV), F32),
                        pltpu.VMEM((tm, N_GB), F32), pltpu.VMEM((N_GB, tm), F32)],
        compiler_params=pltpu.CompilerParams(
            dimension_semantics=("arbitrary",), vmem_limit_bytes=VMEM_LIMIT),
        name="stage1_inproj_gmlp_gdnprep",
    )(x, x, x, *consts)


def _prep_stage1_weights(norm_mix_w, w_in, a_ln_w, a_ln_b, a_spatial_w, a_spatial_b,
                         a_out_norm_w, a_log, dt_bias):
    c0 = 2 * D_A
    c1 = c0 + D_CONV
    c2 = c1 + D_B
    wuv = w_in[:, :c0].reshape(D_MODEL, 2, A_GROUPS, A_DG).transpose(0, 2, 1, 3)
    wuv = wuv.reshape(D_MODEL, c0).astype(BF16)
    wqkv = w_in[:, c0:c1].astype(BF16)
    wz = w_in[:, c1:c2].astype(BF16)
    wgb_raw = w_in[:, c2:]
    wgb = jnp.pad(wgb_raw, ((0, 0), (0, LANES - N_GB))).astype(BF16)
    wgbt = wgb_raw.T.astype(BF16)
    return (norm_mix_w.reshape(1, D_MODEL), wuv, wqkv, wz, wgb, wgbt,
            a_ln_w.reshape(1, D_A), a_ln_b.reshape(1, D_A), a_spatial_w.astype(BF16),
            a_spatial_b.T, a_out_norm_w.reshape(1, D_A),
            a_log.reshape(1, N_DIR * B_HEADS), dt_bias.reshape(1, N_DIR * B_HEADS),
            a_log.reshape(N_DIR * B_HEADS, 1), dt_bias.reshape(N_DIR * B_HEADS, 1))


PAIR = 2 * DN_CHUNK
HALO = 8
assert CONV_W == 5 and CONV_W // 2 <= HALO


def _split3(x):
    hi = x.astype(BF16)
    r1 = x - hi.astype(F32)
    mid = r1.astype(BF16)
    lo = (r1 - mid.astype(F32)).astype(BF16)
    return hi, mid, lo


def _dot_exact_rhs01(x, m01):
    hi, mid, lo = _split3(x)
    return _dot(hi, m01) + _dot(mid, m01) + _dot(lo, m01)


def _dot_exact_lhs01(m01, x):
    hi, mid, lo = _split3(x)
    return _dot(m01, hi) + _dot(m01, mid) + _dot(m01, lo)


INV_BASE = 8


def _unit_tri_inverses(a_negs):
    n = a_negs[0].shape[0]
    row = lax.broadcasted_iota(jnp.int32, (n, n), 0)
    col = lax.broadcasted_iota(jnp.int32, (n, n), 1)
    same_block = lambda size: ((row >> int(math.log2(size))) == (col >> int(math.log2(size))))
    eye = jnp.where(row == col, 1.0, 0.0)

    diag = [jnp.where(same_block(INV_BASE), a, 0.0) for a in a_negs]
    t = [eye + d for d in diag]
    d16 = [d.astype(BF16) for d in diag]
    p16 = [_dot(x, x).astype(BF16) for x in d16]
    both = [_dot(jnp.concatenate([ti.astype(BF16), pi], axis=0), pi) for ti, pi in zip(t, p16)]
    t = [ti + bi[:n] for ti, bi in zip(t, both)]
    t = [ti + _dot(ti.astype(BF16), bi[n:].astype(BF16)) for ti, bi in zip(t, both)]

    size = INV_BASE
    while size < DN_CHUNK:
        off = same_block(2 * size) & jnp.logical_not(same_block(size))
        m16 = [_dot(jnp.where(off, a, 0.0).astype(BF16), ti.astype(BF16)).astype(BF16)
               for a, ti in zip(a_negs, t)]
        t = [ti + _dot(ti.astype(BF16), mi) for ti, mi in zip(t, m16)]
        size *= 2
    return t


def _short_conv_silu(t0, tc, xpad_ref, act_ref, cw_ref):
    win = PAIR + 2 * HALO
    for cb in range(D_CONV // LANES):
        cols = slice(cb * LANES, (cb + 1) * LANES)
        wj = [cw_ref[j:j + 1, cols] for j in range(CONV_W)]
        for r0 in range(t0, t0 + tc, PAIR):
            xw = xpad_ref[r0:r0 + win, cols]
            up = pltpu.roll(wj[3] * xw + pltpu.roll(wj[4] * xw, win - 1, axis=0), win - 1, axis=0)
            dn = pltpu.roll(wj[1] * xw + pltpu.roll(wj[0] * xw, 1, axis=0), 1, axis=0)
            acc = wj[2] * xw + up + dn
            act_ref[r0:r0 + PAIR, cols] = _silu(acc[HALO:HALO + PAIR])


def _gdn_prep(t0, tc, act_ref, gb_ref, gbt_ref, w_ref, qg_ref, u_ref, qk_ref, kgt_ref, gl_ref):
    row = lax.broadcasted_iota(jnp.int32, (PAIR, PAIR), 0)
    col = lax.broadcasted_iota(jnp.int32, (PAIR, PAIR), 1)
    same = (row >= DN_CHUNK) == (col >= DN_CHUNK)
    incl = (same & (row >= col), same & (row <= col))
    strict = (same & (row > col), same & (row < col))
    as01 = lambda m: jnp.where(m, 1.0, 0.0).astype(BF16)
    m_incl = tuple(as01(m) for m in incl)
    m_same = as01(same)
    e_chunk = (as01(row < DN_CHUNK), as01(row >= DN_CHUNK))
    na = N_DIR * B_HEADS

    chains, a_negs, rhss = [], [], []
    for p in range(t0 // PAIR, (t0 + tc) // PAIR):
        rows = slice(p * PAIR, (p + 1) * PAIR)
        gbp = gb_ref[rows, :]
        gbtp = gbt_ref[:, rows]
        gcol = tuple(_dot_exact_lhs01(m_incl[d], gbp) for d in range(N_DIR))
        grow = tuple(_dot_exact_rhs01(gbtp, m_incl[1 - d]) for d in range(N_DIR))
        tot_row = _dot_exact_rhs01(gbtp, m_same)
        for c in range(2):
            gl_ref[2 * p + c] = jnp.exp(_dot_exact_rhs01(gbtp, e_chunk[c]))

        heads = []
        for h in range(B_HEADS):
            q = act_ref[rows, h * B_DK:(h + 1) * B_DK]
            k = act_ref[rows, D_QK + h * B_DK:D_QK + (h + 1) * B_DK]
            qn = q * lax.rsqrt(jnp.sum(q * q, axis=-1, keepdims=True) + EPS) * (B_DK ** -0.5)
            kn = k * lax.rsqrt(jnp.sum(k * k, axis=-1, keepdims=True) + EPS)
            kt = kn.T
            heads.append((qn, kn, kt, kt.astype(BF16)))
        kks = [_dot(kn.astype(BF16), kt16) for _, kn, _, kt16 in heads]
        qks = [_dot(qn.astype(BF16), kt16) for qn, _, _, kt16 in heads]

        for d, h in [(d, h) for d in range(N_DIR) for h in range(B_HEADS)]:
            chains.append((rows, d, h))
            ci = d * B_HEADS + h
            lanes = slice(h * LANES, (h + 1) * LANES)
            qn, kn, kt, _ = heads[h]
            v = act_ref[rows, 2 * D_QK + h * B_DV:2 * D_QK + (h + 1) * B_DV]
            gc = gcol[d][:, ci:ci + 1]
            gr = grow[d][ci:ci + 1, :]
            beta = gbp[:, na + ci:na + ci + 1]
            decay = jnp.where(incl[d], jnp.exp(gc - gr), 0.0)
            a_negs.append(jnp.where(strict[d], -(kks[h] * beta * decay), 0.0))
            eg = jnp.exp(gc)
            rhss.append(jnp.concatenate([v * beta, kn * (beta * eg)], axis=1).astype(BF16))
            qk_ref[d, rows, lanes] = (qks[h] * decay).astype(BF16)
            qg_ref[d, rows, lanes] = (qn * eg).astype(BF16)
            kgt_ref[d, rows, lanes] = (kt * jnp.exp(tot_row[ci:ci + 1, :] - gr)).astype(BF16)

    tinvs = _unit_tri_inverses(a_negs)
    uws = [_dot(t.astype(BF16), rhs) for t, rhs in zip(tinvs, rhss)]
    for (rows, d, h), uw in zip(chains, uws):
        lanes = slice(h * LANES, (h + 1) * LANES)
        u_ref[d, rows, lanes] = uw[:, :B_DV]
        w_ref[d, rows, lanes] = uw[:, B_DV:].astype(BF16)


def _stage3_kernel(wf, qgf, uf, qkf, kgf, wb, qgb, ub, qkb, kgb, glf, glb, of_ref, ob_ref, s_ref):
    i = pl.program_id(1)

    @pl.when(i == 0)
    def _():
        s_ref[...] = jnp.zeros_like(s_ref)

    nseq, rows_blk = wf.shape[1], wf.shape[2]
    npairs = rows_blk // PAIR
    zpad = jnp.zeros((DN_CHUNK, B_DV), BF16)
    per_dir = ((wf, qgf, uf, qkf, kgf, glf, of_ref), (wb, qgb, ub, qkb, kgb, glb, ob_ref))
    chains = [(q, d, h) for q in range(nseq) for d in range(N_DIR) for h in range(B_HEADS)]
    for step in range(2 * npairs):
        def where(d):
            chunk = step if d == 0 else 2 * npairs - 1 - step
            pair = chunk // 2
            return (chunk, slice(chunk * DN_CHUNK, (chunk + 1) * DN_CHUNK),
                    slice(pair * PAIR, (pair + 1) * PAIR))

        states, m1s, m2s = [], [], []
        for q, d, h in chains:
            w_r, qg_r = per_dir[d][0], per_dir[d][1]
            _, rows, _ = where(d)
            lanes = slice(h * LANES, (h + 1) * LANES)
            s = s_ref[q, d, h]
            states.append(s)
            lhs1 = jnp.concatenate([w_r[0, q, rows, lanes], qg_r[0, q, rows, lanes]], axis=0)
            m1s.append(_dot(lhs1, s.astype(BF16)))
        for (q, d, h), m1 in zip(chains, m1s):
            u_r, qk_r, kg_r = per_dir[d][2], per_dir[d][3], per_dir[d][4]
            chunk, rows, prow = where(d)
            lanes = slice(h * LANES, (h + 1) * LANES)
            v_new = (u_r[0, q, rows, lanes] - m1[:DN_CHUNK]).astype(BF16)
            v_pad = jnp.concatenate([v_new, zpad] if chunk % 2 == 0 else [zpad, v_new], axis=0)
            lhs2 = jnp.concatenate([qk_r[0, q, rows, lanes], kg_r[0, q, prow, lanes]], axis=0)
            m2s.append(_dot(lhs2, v_pad))
        for (q, d, h), s, m1, m2 in zip(chains, states, m1s, m2s):
            gl_r, o_r = per_dir[d][5], per_dir[d][6]
            chunk, rows, _ = where(d)
            lanes = slice(h * LANES, (h + 1) * LANES)
            o_r[q, rows, lanes] = m1[DN_CHUNK:] + m2[:DN_CHUNK]
            ci = d * B_HEADS + h
            s_ref[q, d, h] = s * gl_r[q, chunk, ci:ci + 1, :] + m2[DN_CHUNK:]


def _stage3(w, qg, u, qk, kgt, gl, rows_blk):
    _, b, s, _ = w.shape
    n_i = s // rows_blk
    cpb = rows_blk // DN_CHUNK
    nseq = 2 if b % 2 == 0 else 1
    fwd = pl.BlockSpec((1, nseq, rows_blk, D_B), lambda bi, i: (0, bi, i, 0))
    bwd = pl.BlockSpec((1, nseq, rows_blk, D_B), lambda bi, i: (1, bi, n_i - 1 - i, 0))
    return pl.pallas_call(
        _stage3_kernel,
        grid=(b // nseq, n_i),
        in_specs=[fwd] * 5 + [bwd] * 5 + [
            pl.BlockSpec((nseq, cpb, N_GB, LANES), lambda bi, i: (bi, i, 0, 0)),
            pl.BlockSpec((nseq, cpb, N_GB, LANES), lambda bi, i: (bi, n_i - 1 - i, 0, 0)),
        ],
        out_specs=[pl.BlockSpec((nseq, rows_blk, D_B), lambda bi, i: (bi, i, 0)),
                   pl.BlockSpec((nseq, rows_blk, D_B), lambda bi, i: (bi, n_i - 1 - i, 0))],
        out_shape=[jax.ShapeDtypeStruct((b, s, D_B), F32), jax.ShapeDtypeStruct((b, s, D_B), F32)],
        scratch_shapes=[pltpu.VMEM((nseq, N_DIR, B_HEADS, B_DK, B_DV), F32)],
        compiler_params=pltpu.CompilerParams(
            dimension_semantics=("arbitrary", "arbitrary"), vmem_limit_bytes=VMEM_LIMIT),
        name="stage3_gdn_scan",
    )(w, qg, u, qk, kgt, w, qg, u, qk, kgt, gl, gl)


N_ROUTE = N_EXPERT_GROUPS + N_EXPERTS


def _stage4_kernel(x_ref, ya_ref, of_ref, ob_ref, z_ref, gnw_ref, woa_ref, wob_ref, fnw_ref,
                   wr_ref, h_ref, hn_ref, ids_ref, wts_ref, cnt_ref):
    tm = x_ref.shape[0]
    for k in range(tm // MOE_TM):
        cnt_ref[k] = _stage4_rows(slice(k * MOE_TM, (k + 1) * MOE_TM), x_ref, ya_ref, of_ref,
                                  ob_ref, z_ref, gnw_ref, woa_ref, wob_ref, fnw_ref, wr_ref,
                                  h_ref, hn_ref, ids_ref, wts_ref)


def _stage4_rows(sl, x_ref, ya_ref, of_ref, ob_ref, z_ref, gnw_ref, woa_ref, wob_ref, fnw_ref,
                 wr_ref, h_ref, hn_ref, ids_ref, wts_ref):
    tm = sl.stop - sl.start
    o = of_ref[sl, :] + ob_ref[sl, :]
    z = z_ref[sl, :]
    parts = []
    for hd in range(B_HEADS):
        lanes = slice(hd * B_DV, (hd + 1) * B_DV)
        oh = o[:, lanes]
        yh = oh * lax.rsqrt(jnp.mean(oh * oh, axis=-1, keepdims=True) + EPS) * gnw_ref[...]
        parts.append((yh * _silu(z[:, lanes])).astype(BF16))
    yb = jnp.concatenate(parts, axis=1)
    h = x_ref[sl, :] + (_dot(ya_ref[sl, :], woa_ref[...]) + _dot(yb, wob_ref[...]))
    h_ref[sl, :] = h
    hn = h * lax.rsqrt(jnp.mean(h * h, axis=-1, keepdims=True) + EPS) * fnw_ref[...]

    hi = hn.astype(BF16)
    hn_ref[sl, :] = hi
    lo = (hn - hi.astype(F32)).astype(BF16)
    prod = _dot(jnp.concatenate([hi, lo], axis=0), wr_ref[...])
    logits = ((prod[:tm, :LANES] + prod[tm:, :LANES])
              + (prod[:tm, LANES:] + prod[tm:, LANES:]))

    lane = lax.broadcasted_iota(jnp.int32, (tm, LANES), 1)
    neg = -jnp.inf
    is_g = lane < N_EXPERT_GROUPS
    gl = jnp.where(is_g, logits, neg)
    gmax = jnp.max(gl, axis=-1, keepdims=True)
    gidx = jnp.min(jnp.where(gl == gmax, lane, LANES), axis=-1, keepdims=True)
    g_w = 1.0 / jnp.sum(jnp.where(is_g, jnp.exp(gl - gmax), 0.0), axis=-1, keepdims=True)

    e0 = N_EXPERT_GROUPS + gidx * EXPERTS_PER_GROUP
    in_grp = (lane >= e0) & (lane < e0 + EXPERTS_PER_GROUP)
    el = jnp.where(in_grp, logits, neg)
    m1 = jnp.max(el, axis=-1, keepdims=True)
    i1 = jnp.min(jnp.where(el == m1, lane, LANES), axis=-1, keepdims=True)
    el2 = jnp.where(lane == i1, neg, el)
    m2 = jnp.max(el2, axis=-1, keepdims=True)
    i2 = jnp.min(jnp.where(el2 == m2, lane, LANES), axis=-1, keepdims=True)
    e2 = jnp.exp(m2 - m1)
    inv = 1.0 / (1.0 + e2)
    ids_ref[sl, 0:1] = i1 - N_EXPERT_GROUPS
    ids_ref[sl, 1:2] = i2 - N_EXPERT_GROUPS
    wts_ref[sl, 0:1] = g_w * inv
    wts_ref[sl, 1:2] = g_w * (e2 * inv)
    elane = lane + N_EXPERT_GROUPS
    chosen = (elane == i1) | (elane == i2)
    return jnp.sum(jnp.where(chosen, 1.0, 0.0), axis=0, keepdims=True)


def _stage4(x, ya, o_f, o_b, z, gnw, woa, wob, fnw, wr, tm):
    t = x.shape[0]
    full = lambda a: pl.BlockSpec(a.shape, lambda i: (0,) * a.ndim)
    tile = lambda n: pl.BlockSpec((tm, n), lambda i: (i, 0))
    return pl.pallas_call(
        _stage4_kernel,
        grid=(t // tm,),
        in_specs=[tile(D_MODEL), tile(D_A), tile(D_B), tile(D_B), tile(D_B),
                  full(gnw), full(woa), full(wob), full(fnw), full(wr)],
        out_specs=[tile(D_MODEL), tile(D_MODEL), tile(TOP_K), tile(TOP_K),
                   pl.BlockSpec((tm // MOE_TM, 1, LANES), lambda i: (i, 0, 0))],
        out_shape=[jax.ShapeDtypeStruct((t, D_MODEL), F32), jax.ShapeDtypeStruct((t, D_MODEL), BF16),
                   jax.ShapeDtypeStruct((t, TOP_K), jnp.int32),
                   jax.ShapeDtypeStruct((t, TOP_K), F32),
                   jax.ShapeDtypeStruct((t // MOE_TM, 1, LANES), F32)],
        compiler_params=pltpu.CompilerParams(
            dimension_semantics=("arbitrary",), vmem_limit_bytes=VMEM_LIMIT),
        name="stage4_outproj_router",
    )(x, ya, o_f, o_b, z, gnw, woa, wob, fnw, wr)


MOE_TM = 512
MOE_R = 16
MOE_BM = 512
MOE_L = TOP_K * MOE_TM + N_EXPERTS * MOE_R
SCAN_ROWS = 512
OUTPROJ_TM = 2 * MOE_TM


def _moe_plan(cnt):
    n_tiles = cnt.shape[0]
    c = cnt[:, :N_EXPERTS].astype(jnp.int32)
    cpad = (c + MOE_R - 1) // MOE_R * MOE_R
    seg = jnp.sum(cpad, axis=0)
    segpad = (seg + MOE_BM - 1) // MOE_BM * MOE_BM
    pad_end = jnp.cumsum(segpad)
    pad_start = pad_end - segpad
    off = pad_start[None, :] + jnp.cumsum(cpad, axis=0) - cpad
    loc = jnp.cumsum(cpad, axis=1) - cpad
    nch = cpad // MOE_R
    n_blocks = -(-(TOP_K * MOE_TM + N_EXPERTS * (MOE_R - 1)) * n_tiles // MOE_BM) + N_EXPERTS
    n_used = pad_end[-1] // MOE_BM
    blk = jnp.arange(n_blocks, dtype=jnp.int32)
    first_row = jnp.minimum(blk, n_used - 1) * MOE_BM
    block_expert = jnp.sum((pad_end[None, :] <= first_row[:, None]).astype(jnp.int32), axis=1)
    block_expert = jnp.minimum(block_expert, N_EXPERTS - 1)
    loc_lanes = jnp.pad(loc, ((0, 0), (0, LANES - N_EXPERTS))).astype(F32)
    return dict(off=off.reshape(-1), loc=loc.reshape(-1), nch=nch.reshape(-1),
                tot=jnp.sum(nch, axis=1), tail_off=pad_start + seg,
                tail_n=(segpad - seg) // MOE_R, loc_lanes=loc_lanes.reshape(n_tiles, 1, LANES),
                block_expert=block_expert, n_used=n_used.reshape(1), n_blocks=n_blocks)


def _run_chunks(nch_ref, loc_ref, off_ref, tile, visit):
    def per_expert(e, carry):
        idx = tile * N_EXPERTS + e
        loc0 = loc_ref[idx]
        off0 = off_ref[idx]
        n = nch_ref[idx]

        def per_double(j, c):
            step = j * (2 * MOE_R)
            visit(pl.multiple_of(loc0 + step, MOE_R), pl.multiple_of(off0 + step, MOE_R),
                  2 * MOE_R)
            return c
        lax.fori_loop(0, n >> 1, per_double, 0)

        @pl.when((n & 1) == 1)
        def _():
            last = (n - 1) * MOE_R
            visit(pl.multiple_of(loc0 + last, MOE_R), pl.multiple_of(off0 + last, MOE_R), MOE_R)
        return carry
    lax.fori_loop(0, N_EXPERTS, per_expert, 0)


def _wait_chunks(count, chunk_wait):
    for bit in range((MOE_L // MOE_R).bit_length()):
        @pl.when(((count >> bit) & 1) == 1)
        def _():
            chunk_wait((1 << bit) * MOE_R)


def _dispatch_kernel(n, n_first, n_blocks, off_ref, loc_ref, nch_ref, tot_ref, toff_ref, tn_ref,
                     nu_ref, ids_ref, hna_ref, hnb_ref, locl_ref, xs_ref, pos_ref,
                     hn_ref, xl, zrows, sem, tsem):
    i = pl.program_id(0)
    g = i
    slot = lax.rem(i, 2)
    tm = ids_ref.shape[0]

    @pl.when(i < n_first)
    def _():
        hn_ref[...] = hna_ref[...]

    @pl.when(i >= n_first)
    def _():
        hn_ref[...] = hnb_ref[...]

    ids = ids_ref[...]
    lane = lax.broadcasted_iota(jnp.int32, (tm, LANES), 1)
    oh0 = jnp.where(lane == ids[:, 0:1], 1.0, 0.0)
    oh1 = jnp.where(lane == ids[:, 1:2], 1.0, 0.0)
    row = lax.broadcasted_iota(jnp.int32, (tm, tm), 0)
    col = lax.broadcasted_iota(jnp.int32, (tm, tm), 1)
    earlier = jnp.where(row > col, 1.0, 0.0).astype(BF16)
    base = _dot(earlier, (oh0 + oh1).astype(BF16)) + locl_ref[0]
    m0 = base * oh0
    m1 = base * oh1
    pos_ref[:, 0:1] = jnp.sum(m0, axis=-1, keepdims=True).astype(jnp.int32)
    pos_ref[:, 1:2] = jnp.sum(m1, axis=-1, keepdims=True).astype(jnp.int32)
    ones = jnp.ones((8, LANES), BF16)
    lane_form = lambda m: sum(_dot_nt(ones, part) for part in _split3(m))[0:1].astype(jnp.int32)
    p0 = lane_form(m0)
    p1 = lane_form(m1)
    srow = lax.broadcasted_iota(jnp.int32, (MOE_L, tm), 0)
    perm = jnp.where((srow == p0) | (srow == p1), 1.0, 0.0).astype(BF16)
    nb = 256
    for cb in range(D_MODEL // nb):
        xl[slot, :, cb * nb:(cb + 1) * nb] = _dot(perm, hn_ref[:, cb * nb:(cb + 1) * nb]).astype(BF16)

    def rows_copy(sl, lrow, grow, nrows):
        return pltpu.make_async_copy(xl.at[sl, pl.ds(lrow, nrows)],
                                     xs_ref.at[pl.ds(grow, nrows)], sem.at[sl])

    _run_chunks(nch_ref, loc_ref, off_ref, g,
                lambda lrow, grow, nrows: rows_copy(slot, lrow, grow, nrows).start())

    def wait_tile(tile, sl):
        _wait_chunks(tot_ref[tile], lambda nrows: rows_copy(sl, 0, 0, nrows).wait())

    @pl.when(i > 0)
    def _():
        wait_tile(g - 1, 1 - slot)

    @pl.when(i == n - 1)
    def _():
        wait_tile(g, slot)
        zrows[...] = jnp.zeros_like(zrows)

        def tail_copy(e, j):
            row0 = pl.multiple_of(toff_ref[e] + j * MOE_R, MOE_R)
            return pltpu.make_async_copy(zrows.at[pl.ds(0, MOE_R)],
                                         xs_ref.at[pl.ds(row0, MOE_R)], tsem.at[0])

        def block_copy(b):
            row0 = pl.multiple_of(b * MOE_BM, MOE_BM)
            return pltpu.make_async_copy(zrows, xs_ref.at[pl.ds(row0, MOE_BM)], tsem.at[0])

        def fill(act):
            def per_expert(e, carry):
                def per_chunk(j, c):
                    act(tail_copy(e, j))
                    return c
                lax.fori_loop(0, tn_ref[e], per_chunk, 0)
                return carry
            lax.fori_loop(0, N_EXPERTS, per_expert, 0)

            def per_block(b, c):
                act(block_copy(b))
                return c
            lax.fori_loop(nu_ref[0], n_blocks, per_block, 0)

        fill(lambda cp: cp.start())
        fill(lambda cp: cp.wait())


def _dispatch(plan, ids, hn_a, hn_b):
    n_a = hn_a.shape[0] // MOE_TM
    n = ids.shape[0] // MOE_TM
    n_blocks = plan['n_blocks']
    grid_spec = pltpu.PrefetchScalarGridSpec(
        num_scalar_prefetch=7,
        grid=(n,),
        in_specs=[
            pl.BlockSpec((MOE_TM, TOP_K), lambda i, *_: (i, 0)),
            pl.BlockSpec((MOE_TM, D_MODEL), lambda i, *_: (jnp.minimum(i, n_a - 1), 0)),
            pl.BlockSpec((MOE_TM, D_MODEL), lambda i, *_: (jnp.maximum(i - n_a, 0), 0)),
            pl.BlockSpec((1, 1, LANES), lambda i, *_: (i, 0, 0)),
        ],
        out_specs=[pl.BlockSpec(memory_space=pl.ANY),
                   pl.BlockSpec((MOE_TM, TOP_K), lambda i, *_: (i, 0))],
        scratch_shapes=[pltpu.VMEM((MOE_TM, D_MODEL), BF16),
                        pltpu.VMEM((2, MOE_L, D_MODEL), BF16), pltpu.VMEM((MOE_BM, D_MODEL), BF16),
                        pltpu.SemaphoreType.DMA((2,)), pltpu.SemaphoreType.DMA((1,))],
    )
    return pl.pallas_call(
        functools.partial(_dispatch_kernel, n, n_a, n_blocks),
        grid_spec=grid_spec,
        out_shape=[jax.ShapeDtypeStruct((n_blocks * MOE_BM, D_MODEL), BF16),
                   jax.ShapeDtypeStruct((ids.shape[0], TOP_K), jnp.int32)],
        compiler_params=pltpu.CompilerParams(
            dimension_semantics=("arbitrary",), vmem_limit_bytes=VMEM_LIMIT),
        name="stage5a_dispatch",
    )(plan['off'], plan['loc'], plan['nch'], plan['tot'], plan['tail_off'], plan['tail_n'],
      plan['n_used'], ids, hn_a, hn_b, plan['loc_lanes'])


def _expert_kernel(be_ref, nu_ref, xs_ref, wg_ref, wu_ref, wd_ref, y_ref, wg16, wu16, wd16):
    b = pl.program_id(0)

    @pl.when((b == 0) | (be_ref[b] != be_ref[jnp.maximum(b - 1, 0)]))
    def _():
        wg16[...] = wg_ref[0].astype(BF16)
        wu16[...] = wu_ref[0].astype(BF16)
        wd16[...] = wd_ref[0].astype(BF16)

    @pl.when(b < nu_ref[0])
    def _():
        x = xs_ref[...]
        g = _dot(x, wg16[...])
        u = _dot(x, wu16[...])
        y_ref[...] = _dot((_silu(g) * u).astype(BF16), wd16[...]).astype(BF16)

    @pl.when(b >= nu_ref[0])
    def _():
        y_ref[...] = jnp.zeros_like(y_ref)


def _experts(plan, xs, wg, wu, wd):
    n_blocks = plan['n_blocks']
    used = lambda b, nu: jnp.minimum(b, nu[0] - 1)
    grid_spec = pltpu.PrefetchScalarGridSpec(
        num_scalar_prefetch=2,
        grid=(n_blocks,),
        in_specs=[
            pl.BlockSpec((MOE_BM, D_MODEL), lambda b, be, nu: (used(b, nu), 0)),
            pl.BlockSpec((1, D_MODEL, D_EXPERT), lambda b, be, nu: (be[b], 0, 0)),
            pl.BlockSpec((1, D_MODEL, D_EXPERT), lambda b, be, nu: (be[b], 0, 0)),
            pl.BlockSpec((1, D_EXPERT, D_MODEL), lambda b, be, nu: (be[b], 0, 0)),
        ],
        out_specs=pl.BlockSpec((MOE_BM, D_MODEL), lambda b, be, nu: (b, 0)),
        scratch_shapes=[pltpu.VMEM((D_MODEL, D_EXPERT), BF16), pltpu.VMEM((D_MODEL, D_EXPERT), BF16),
                        pltpu.VMEM((D_EXPERT, D_MODEL), BF16)],
    )
    return pl.pallas_call(
        _expert_kernel,
        grid_spec=grid_spec,
        out_shape=jax.ShapeDtypeStruct((n_blocks * MOE_BM, D_MODEL), BF16),
        compiler_params=pltpu.CompilerParams(
            dimension_semantics=("arbitrary",), vmem_limit_bytes=VMEM_LIMIT),
        name="stage5b_experts",
    )(plan['block_expert'], plan['n_used'], xs, wg, wu, wd)


def _combine_kernel(n, tile_base, off_ref, loc_ref, nch_ref, tot_ref, pos_ref, wts_ref, h_ref,
                    fw_ref, y_hbm, out_ref, yl, sem):
    i = pl.program_id(0)
    g = tile_base + i
    slot = lax.rem(i, 2)
    tm = h_ref.shape[0]

    def rows_copy(sl, lrow, grow, nrows):
        return pltpu.make_async_copy(y_hbm.at[pl.ds(grow, nrows)],
                                     yl.at[sl, pl.ds(lrow, nrows)], sem.at[sl])

    def fetch(tile, sl):
        _run_chunks(nch_ref, loc_ref, off_ref, tile,
                    lambda lrow, grow, nrows: rows_copy(sl, lrow, grow, nrows).start())

    @pl.when(i == 0)
    def _():
        yl[...] = jnp.zeros_like(yl)
        fetch(g, 0)

    @pl.when(i + 1 < n)
    def _():
        fetch(g + 1, 1 - slot)

    _wait_chunks(tot_ref[g], lambda nrows: rows_copy(slot, 0, 0, nrows).wait())

    lane = lax.broadcasted_iota(jnp.int32, (tm, MOE_L), 1)
    pos = pos_ref[...]
    w = wts_ref[...]
    sel = (jnp.where(lane == pos[:, 0:1], w[:, 0:1], 0.0)
           + jnp.where(lane == pos[:, 1:2], w[:, 1:2], 0.0)).astype(BF16)
    h = h_ref[...] + _dot(sel, yl[slot])
    out_ref[...] = h * lax.rsqrt(jnp.mean(h * h, axis=-1, keepdims=True) + EPS) * fw_ref[...]


def _combine(plan, tile_base, pos, wts, h, fw, y_rows):
    t = h.shape[0]
    n = t // MOE_TM
    grid_spec = pltpu.PrefetchScalarGridSpec(
        num_scalar_prefetch=4,
        grid=(n,),
        in_specs=[
            pl.BlockSpec((MOE_TM, TOP_K), lambda i, *_: (i, 0)),
            pl.BlockSpec((MOE_TM, TOP_K), lambda i, *_: (i, 0)),
            pl.BlockSpec((MOE_TM, D_MODEL), lambda i, *_: (i, 0)),
            pl.BlockSpec((1, D_MODEL), lambda i, *_: (0, 0)),
            pl.BlockSpec(memory_space=pl.ANY),
        ],
        out_specs=pl.BlockSpec((MOE_TM, D_MODEL), lambda i, *_: (i, 0)),
        scratch_shapes=[pltpu.VMEM((2, MOE_L, D_MODEL), BF16), pltpu.SemaphoreType.DMA((2,))],
    )
    return pl.pallas_call(
        functools.partial(_combine_kernel, n, tile_base),
        grid_spec=grid_spec,
        out_shape=jax.ShapeDtypeStruct((t, D_MODEL), F32),
        compiler_params=pltpu.CompilerParams(
            dimension_semantics=("arbitrary",), vmem_limit_bytes=VMEM_LIMIT),
        name="stage6_combine_norm",
    )(plan['off'], plan['loc'], plan['nch'], plan['tot'], pos, wts, h, fw, y_rows)


def kernel(x_prompt, x_sample, norm_mix_w, w_in, a_ln_w, a_ln_b, a_spatial_w, a_spatial_b, a_out_norm_w, conv_w, a_log, dt_bias, gdn_norm_w, w_out, norm_ffn_w, w_router_group, w_router_expert, w_gate, w_up, w_down, norm_final_w):
    s1w = _prep_stage1_weights(norm_mix_w[0], w_in[0], a_ln_w[0], a_ln_b[0], a_spatial_w[0],
                               a_spatial_b[0], a_out_norm_w[0], a_log[0], dt_bias[0])
    woa = w_out[0, :D_A].astype(BF16)
    wob = w_out[0, D_A:].astype(BF16)
    w_r = jnp.concatenate([w_router_group[0], w_router_expert[0]], axis=1)
    w_r = jnp.pad(w_r, ((0, 0), (0, LANES - N_ROUTE)))
    wrh = w_r.astype(BF16)
    wr = jnp.concatenate([wrh, (w_r - wrh.astype(F32)).astype(BF16)], axis=1)
    gnw = gdn_norm_w[0].reshape(1, B_DV)
    fnw = norm_ffn_w[0].reshape(1, D_MODEL)

    per_run = []
    for x in (x_prompt, x_sample):
        b, s, d = x.shape
        assert d == D_MODEL and s % MIXER_TM == 0 and s % SCAN_ROWS == 0
        assert (b * s) % OUTPROJ_TM == 0
        x2 = x.reshape(b * s, d)
        ya, z, w, qg, u, qk, kgt, gl = _stage1(x2, s, *s1w, conv_w[0], tm=MIXER_TM)
        per_seq = lambda a: a.reshape(N_DIR, b, s, D_B)
        o_f, o_b = _stage3(per_seq(w), per_seq(qg), per_seq(u), per_seq(qk), per_seq(kgt),
                           gl.reshape(b, s // DN_CHUNK, N_GB, LANES), rows_blk=SCAN_ROWS)
        h, hn, ids, wts, cnt = _stage4(x2, ya, o_f.reshape(b * s, D_B), o_b.reshape(b * s, D_B),
                                       z, gnw, woa, wob, fnw, wr, tm=OUTPROJ_TM)
        per_run.append(dict(shape=x.shape, h=h, hn=hn, ids=ids, wts=wts, cnt=cnt[:, 0, :]))

    plan = _moe_plan(jnp.concatenate([r['cnt'] for r in per_run], axis=0))
    ids_all = jnp.concatenate([r['ids'] for r in per_run], axis=0)
    xs, pos_all = _dispatch(plan, ids_all, per_run[0]['hn'], per_run[1]['hn'])
    y_rows = _experts(plan, xs, w_gate[0], w_up[0], w_down[0])
    fw = norm_final_w.reshape(1, D_MODEL)
    outs = []
    t0 = 0
    for r in per_run:
        t = r['h'].shape[0]
        outs.append(_combine(plan, t0 // MOE_TM, pos_all[t0:t0 + t], r['wts'], r['h'], fw,
                             y_rows).reshape(r['shape']))
        t0 += t
    return tuple(outs)
```

```python
import functools
import math

import jax
import jax.numpy as jnp
from jax import lax
from jax.experimental import pallas as pl
from jax.experimental.pallas import tpu as pltpu

D_MODEL = 1024
D_A = 512
A_GROUPS = 4
A_DG = 128
A_CHUNK = 128
D_B = 512
B_HEADS = 4
B_DK = 128
B_DV = 128
D_QK = 512
DN_CHUNK = 128
CONV_W = 5
N_DIR = 2
N_EXPERT_GROUPS = 4
EXPERTS_PER_GROUP = 8
N_EXPERTS = 32
TOP_K = 2
D_EXPERT = 512
EPS = 1e-6
D_CONV = 2 * D_QK + D_B
N_GB = 2 * N_DIR * B_HEADS

LANES = 128
VMEM_LIMIT = 48 * 1024 * 1024

BF16 = jnp.bfloat16
F32 = jnp.float32


def _dot(a, b):
    return jnp.dot(a, b, preferred_element_type=F32)


def _dot_nt(a, b):
    return lax.dot_general(a, b, (((1,), (1,)), ((), ())), preferred_element_type=F32)


def _gelu_tanh(x):
    c = math.sqrt(2.0 / math.pi)
    return x * (0.5 * (1.0 + jnp.tanh(c * (x + 0.044715 * (x * x * x)))))


def _sigmoid(x):
    return 0.5 * (1.0 + jnp.tanh(0.5 * x))


def _silu(x):
    return x * _sigmoid(x)


def _softplus(x):
    return jnp.maximum(x, 0.0) + jnp.log(1.0 + jnp.exp(-jnp.abs(x)))


MIXER_TM = 512
GDN_TC = 256


def _stage1_kernel(tiles_per_seq, x_ref, xp_ref, xn_ref, nw_ref, wuv_ref, wqkv_ref,
                   wz_ref, wgb_ref, wgbt_ref, lnw_ref, lnb_ref, spw_ref, spbt_ref, onw_ref,
                   alog_ref, dtb_ref, alogt_ref, dtbt_ref, cw_ref,
                   ya_ref, z_ref, w_ref, qg_ref, u_ref, qk_ref, kgt_ref, gl_ref,
                   xpad_ref, act_ref, gb_ref, gbt_ref):
    tm = x_ref.shape[0]
    na = N_DIR * B_HEADS
    seq_pos = lax.rem(pl.program_id(0), tiles_per_seq)

    def normed(x):
        xn = x * lax.rsqrt(jnp.mean(x * x, axis=-1, keepdims=True) + EPS) * nw_ref[...]
        return xn.astype(BF16)

    xb = normed(x_ref[...])
    halo = normed(jnp.concatenate([xp_ref[...], xn_ref[...]], axis=0))
    qkv = _dot(jnp.concatenate([xb, halo], axis=0), wqkv_ref[...])
    xpad_ref[0:HALO, :] = jnp.where(seq_pos > 0, qkv[tm:tm + HALO], 0.0)
    xpad_ref[HALO:HALO + tm, :] = qkv[:tm]
    xpad_ref[HALO + tm:, :] = jnp.where(seq_pos < tiles_per_seq - 1, qkv[tm + HALO:], 0.0)

    ab = _dot(xb, wgb_ref[...])[:, :N_GB]
    abt = _dot_nt(wgbt_ref[...], xb)
    gb_ref[:, :na] = -jnp.exp(alog_ref[...]) * _softplus(ab[:, :na] + dtb_ref[...])
    gb_ref[:, na:] = _sigmoid(ab[:, na:])
    gbt_ref[:na, :] = -jnp.exp(alogt_ref[...]) * _softplus(abt[:na, :] + dtbt_ref[...])
    gbt_ref[na:, :] = _sigmoid(abt[na:, :])

    def gmlp_group(grp):
        cols = slice(grp * A_DG, (grp + 1) * A_DG)
        uv_all = _dot(xb, wuv_ref[:, 2 * grp * A_DG:2 * (grp + 1) * A_DG])
        u_all = uv_all[:, :A_DG]
        v_all = uv_all[:, A_DG:]
        spw = spw_ref[grp]
        for c in range(tm // A_CHUNK):
            rows = slice(c * A_CHUNK, (c + 1) * A_CHUNK)
            u = _gelu_tanh(u_all[rows])
            v = _gelu_tanh(v_all[rows])
            mu = jnp.mean(v, axis=-1, keepdims=True)
            vc = v - mu
            var = jnp.mean(vc * vc, axis=-1, keepdims=True)
            vn = vc * lax.rsqrt(var + EPS) * lnw_ref[:, cols] + lnb_ref[:, cols]
            mixed = _dot(spw, vn.astype(BF16)) + spbt_ref[:, grp:grp + 1]
            gated = u * mixed
            out = gated * lax.rsqrt(jnp.mean(gated * gated, axis=-1, keepdims=True) + EPS)
            ya_ref[rows, cols] = (out * onw_ref[:, cols]).astype(BF16)

    n_sub = tm // GDN_TC
    for k in range(n_sub):
        _short_conv_silu(k * GDN_TC, GDN_TC, xpad_ref, act_ref, cw_ref)
        if k == 0:
            z_ref[...] = _dot(xb, wz_ref[...])
        _gdn_prep(k * GDN_TC, GDN_TC, act_ref, gb_ref, gbt_ref,
                  w_ref, qg_ref, u_ref, qk_ref, kgt_ref, gl_ref)
        for grp in range(k * A_GROUPS // n_sub, (k + 1) * A_GROUPS // n_sub):
            gmlp_group(grp)


def _stage1(x, seq_len, nw, wuv, wqkv, wz, wgb, wgbt, lnw, lnb, spw, spbt, onw, alog, dtb,
            alogt, dtbt, conv_w, tm):
    t = x.shape[0]
    hb = tm // HALO
    consts = (nw, wuv, wqkv, wz, wgb, wgbt, lnw, lnb, spw, spbt, onw, alog, dtb, alogt, dtbt,
              conv_w)
    full = lambda a: pl.BlockSpec(a.shape, lambda i: (0,) * a.ndim)
    dirs = pl.BlockSpec((N_DIR, tm, D_B), lambda i: (0, i, 0))
    dir_shape = lambda dtype: jax.ShapeDtypeStruct((N_DIR, t, D_B), dtype)
    return pl.pallas_call(
        functools.partial(_stage1_kernel, seq_len // tm),
        grid=(t // tm,),
        in_specs=[
            pl.BlockSpec((tm, D_MODEL), lambda i: (i, 0)),
            pl.BlockSpec((HALO, D_MODEL), lambda i: (jnp.maximum(i * hb - 1, 0), 0)),
            pl.BlockSpec((HALO, D_MODEL), lambda i: (jnp.minimum((i + 1) * hb, t // HALO - 1), 0)),
        ] + [full(a) for a in consts],
        out_specs=[
            pl.BlockSpec((tm, D_A), lambda i: (i, 0)),
            pl.BlockSpec((tm, D_B), lambda i: (i, 0)),
            dirs, dirs, dirs, dirs, dirs,
            pl.BlockSpec((tm // DN_CHUNK, N_GB, LANES), lambda i: (i, 0, 0)),
        ],
        out_shape=[
            jax.ShapeDtypeStruct((t, D_A), BF16),
            jax.ShapeDtypeStruct((t, D_B), F32),
            dir_shape(BF16), dir_shape(BF16), dir_shape(F32), dir_shape(BF16), dir_shape(BF16),
            jax.ShapeDtypeStruct((t // DN_CHUNK, N_GB, LANES), F32),
        ],
        scratch_shapes=[pltpu.VMEM((tm + 2 * HALO, D_CONV), F32), pltpu.VMEM((tm, D_CONV), F32),
                        pltpu.VMEM((tm, N_GB), F32), pltpu.VMEM((N_GB, tm), F32)],
        compiler_params=pltpu.CompilerParams(
            dimension_semantics=("arbitrary",), vmem_limit_bytes=VMEM_LIMIT),
        name="stage1_inproj_gmlp_gdnprep",
    )(x, x, x, *consts)


def _prep_stage1_weights(norm_mix_w, w_in, a_ln_w, a_ln_b, a_spatial_w, a_spatial_b,
                         a_out_norm_w, a_log, dt_bias):
    c0 = 2 * D_A
    c1 = c0 + D_CONV
    c2 = c1 + D_B
    wuv = w_in[:, :c0].reshape(D_MODEL, 2, A_GROUPS, A_DG).transpose(0, 2, 1, 3)
    wuv = wuv.reshape(D_MODEL, c0).astype(BF16)
    wqkv = w_in[:, c0:c1].astype(BF16)
    wz = w_in[:, c1:c2].astype(BF16)
    wgb_raw = w_in[:, c2:]
    wgb = jnp.pad(wgb_raw, ((0, 0), (0, LANES - N_GB))).astype(BF16)
    wgbt = wgb_raw.T.astype(BF16)
    return (norm_mix_w.reshape(1, D_MODEL), wuv, wqkv, wz, wgb, wgbt,
            a_ln_w.reshape(1, D_A), a_ln_b.reshape(1, D_A), a_spatial_w.astype(BF16),
            a_spatial_b.T, a_out_norm_w.reshape(1, D_A),
            a_log.reshape(1, N_DIR * B_HEADS), dt_bias.reshape(1, N_DIR * B_HEADS),
            a_log.reshape(N_DIR * B_HEADS, 1), dt_bias.reshape(N_DIR * B_HEADS, 1))


PAIR = 128
assert PAIR % DN_CHUNK == 0
HALO = 8
assert CONV_W == 5 and CONV_W // 2 <= HALO


def _split3(x):
    hi = x.astype(BF16)
    r1 = x - hi.astype(F32)
    mid = r1.astype(BF16)
    lo = (r1 - mid.astype(F32)).astype(BF16)
    return hi, mid, lo


def _dot_exact_rhs01(x, m01):
    hi, mid, lo = _split3(x)
    return _dot(hi, m01) + _dot(mid, m01) + _dot(lo, m01)


def _dot_exact_lhs01(m01, x):
    hi, mid, lo = _split3(x)
    return _dot(m01, hi) + _dot(m01, mid) + _dot(m01, lo)


INV_BASE = 8


def _unit_tri_inverses(a_negs):
    n = a_negs[0].shape[0]
    row = lax.broadcasted_iota(jnp.int32, (n, n), 0)
    col = lax.broadcasted_iota(jnp.int32, (n, n), 1)
    same_block = lambda size: ((row >> int(math.log2(size))) == (col >> int(math.log2(size))))
    eye = jnp.where(row == col, 1.0, 0.0)

    diag = [jnp.where(same_block(INV_BASE), a, 0.0) for a in a_negs]
    t = [eye + d for d in diag]
    d16 = [d.astype(BF16) for d in diag]
    p16 = [_dot(x, x).astype(BF16) for x in d16]
    both = [_dot(jnp.concatenate([ti.astype(BF16), pi], axis=0), pi) for ti, pi in zip(t, p16)]
    t = [ti + bi[:n] for ti, bi in zip(t, both)]
    t = [ti + _dot(ti.astype(BF16), bi[n:].astype(BF16)) for ti, bi in zip(t, both)]

    size = INV_BASE
    while size < DN_CHUNK:
        off = same_block(2 * size) & jnp.logical_not(same_block(size))
        m16 = [_dot(jnp.where(off, a, 0.0).astype(BF16), ti.astype(BF16)).astype(BF16)
               for a, ti in zip(a_negs, t)]
        t = [ti + _dot(ti.astype(BF16), mi) for ti, mi in zip(t, m16)]
        size *= 2
    return t


def _short_conv_silu(t0, tc, xpad_ref, act_ref, cw_ref):
    win = PAIR + 2 * HALO
    for cb in range(D_CONV // LANES):
        cols = slice(cb * LANES, (cb + 1) * LANES)
        wj = [cw_ref[j:j + 1, cols] for j in range(CONV_W)]
        for r0 in range(t0, t0 + tc, PAIR):
            xw = xpad_ref[r0:r0 + win, cols]
            up = pltpu.roll(wj[3] * xw + pltpu.roll(wj[4] * xw, win - 1, axis=0), win - 1, axis=0)
            dn = pltpu.roll(wj[1] * xw + pltpu.roll(wj[0] * xw, 1, axis=0), 1, axis=0)
            acc = wj[2] * xw + up + dn
            act_ref[r0:r0 + PAIR, cols] = _silu(acc[HALO:HALO + PAIR])


def _gdn_prep(t0, tc, act_ref, gb_ref, gbt_ref, w_ref, qg_ref, u_ref, qk_ref, kgt_ref, gl_ref):
    row = lax.broadcasted_iota(jnp.int32, (PAIR, PAIR), 0)
    col = lax.broadcasted_iota(jnp.int32, (PAIR, PAIR), 1)
    chunk_of = lambda idx: idx >> int(math.log2(DN_CHUNK))
    same = chunk_of(row) == chunk_of(col)
    incl = (same & (row >= col), same & (row <= col))
    strict = (same & (row > col), same & (row < col))
    as01 = lambda m: jnp.where(m, 1.0, 0.0).astype(BF16)
    m_incl = tuple(as01(m) for m in incl)
    m_same = as01(same)
    cpp = PAIR // DN_CHUNK
    e_chunk = tuple(as01(chunk_of(row) == c) for c in range(cpp))
    na = N_DIR * B_HEADS

    chains, a_negs, rhss = [], [], []
    for p in range(t0 // PAIR, (t0 + tc) // PAIR):
        rows = slice(p * PAIR, (p + 1) * PAIR)
        gbp = gb_ref[rows, :]
        gbtp = gbt_ref[:, rows]
        gcol = tuple(_dot_exact_lhs01(m_incl[d], gbp) for d in range(N_DIR))
        grow = tuple(_dot_exact_rhs01(gbtp, m_incl[1 - d]) for d in range(N_DIR))
        tot_row = _dot_exact_rhs01(gbtp, m_same)
        for c in range(cpp):
            gl_ref[cpp * p + c] = jnp.exp(_dot_exact_rhs01(gbtp, e_chunk[c]))

        heads = []
        for h in range(B_HEADS):
            q = act_ref[rows, h * B_DK:(h + 1) * B_DK]
            k = act_ref[rows, D_QK + h * B_DK:D_QK + (h + 1) * B_DK]
            qn = q * lax.rsqrt(jnp.sum(q * q, axis=-1, keepdims=True) + EPS) * (B_DK ** -0.5)
            kn = k * lax.rsqrt(jnp.sum(k * k, axis=-1, keepdims=True) + EPS)
            kt = kn.T
            heads.append((qn, kn, kt, kt.astype(BF16)))
        kks = [_dot(kn.astype(BF16), kt16) for _, kn, _, kt16 in heads]
        qks = [_dot(qn.astype(BF16), kt16) for qn, _, _, kt16 in heads]

        for d, h in [(d, h) for d in range(N_DIR) for h in range(B_HEADS)]:
            chains.append((rows, d, h))
            ci = d * B_HEADS + h
            lanes = slice(h * LANES, (h + 1) * LANES)
            qn, kn, kt, _ = heads[h]
            v = act_ref[rows, 2 * D_QK + h * B_DV:2 * D_QK + (h + 1) * B_DV]
            gc = gcol[d][:, ci:ci + 1]
            gr = grow[d][ci:ci + 1, :]
            beta = gbp[:, na + ci:na + ci + 1]
            decay = jnp.where(incl[d], jnp.exp(gc - gr), 0.0)
            a_negs.append(jnp.where(strict[d], -(kks[h] * beta * decay), 0.0))
            eg = jnp.exp(gc)
            rhss.append(jnp.concatenate([v * beta, kn * (beta * eg)], axis=1).astype(BF16))
            qk_ref[d, rows, lanes] = (qks[h] * decay).astype(BF16)
            qg_ref[d, rows, lanes] = (qn * eg).astype(BF16)
            kgt_ref[d, rows, lanes] = (kt * jnp.exp(tot_row[ci:ci + 1, :] - gr)).astype(BF16)

    tinvs = _unit_tri_inverses(a_negs)
    uws = [_dot(t.astype(BF16), rhs) for t, rhs in zip(tinvs, rhss)]
    for (rows, d, h), uw in zip(chains, uws):
        lanes = slice(h * LANES, (h + 1) * LANES)
        u_ref[d, rows, lanes] = uw[:, :B_DV]
        w_ref[d, rows, lanes] = uw[:, B_DV:].astype(BF16)


def _stage3_kernel(wf, qgf, uf, qkf, kgf, wb, qgb, ub, qkb, kgb, glf, glb, of_ref, ob_ref, s_ref):
    i = pl.program_id(1)

    @pl.when(i == 0)
    def _():
        s_ref[...] = jnp.zeros_like(s_ref)

    nseq, rows_blk = wf.shape[1], wf.shape[2]
    nchunks = rows_blk // DN_CHUNK
    cpp = PAIR // DN_CHUNK
    per_dir = ((wf, qgf, uf, qkf, kgf, glf, of_ref), (wb, qgb, ub, qkb, kgb, glb, ob_ref))
    chains = [(q, d, h) for q in range(nseq) for d in range(N_DIR) for h in range(B_HEADS)]
    for step in range(nchunks):
        def where(d):
            chunk = step if d == 0 else nchunks - 1 - step
            pair = chunk // cpp
            return (chunk, slice(chunk * DN_CHUNK, (chunk + 1) * DN_CHUNK),
                    slice(pair * PAIR, (pair + 1) * PAIR))

        states, m1s, m2s = [], [], []
        for q, d, h in chains:
            w_r, qg_r = per_dir[d][0], per_dir[d][1]
            _, rows, _ = where(d)
            lanes = slice(h * LANES, (h + 1) * LANES)
            s = s_ref[q, d, h]
            states.append(s)
            lhs1 = jnp.concatenate([w_r[0, q, rows, lanes], qg_r[0, q, rows, lanes]], axis=0)
            m1s.append(_dot(lhs1, s.astype(BF16)))
        for (q, d, h), m1 in zip(chains, m1s):
            u_r, qk_r, kg_r = per_dir[d][2], per_dir[d][3], per_dir[d][4]
            chunk, rows, prow = where(d)
            lanes = slice(h * LANES, (h + 1) * LANES)
            v_new = (u_r[0, q, rows, lanes] - m1[:DN_CHUNK]).astype(BF16)
            zpad = jnp.zeros((DN_CHUNK, B_DV), BF16)
            v_pad = jnp.concatenate(
                [v_new if c == chunk % cpp else zpad for c in range(cpp)], axis=0)
            lhs2 = jnp.concatenate([qk_r[0, q, rows, lanes], kg_r[0, q, prow, lanes]], axis=0)
            m2s.append(_dot(lhs2, v_pad))
        for (q, d, h), s, m1, m2 in zip(chains, states, m1s, m2s):
            gl_r, o_r = per_dir[d][5], per_dir[d][6]
            chunk, rows, _ = where(d)
            lanes = slice(h * LANES, (h + 1) * LANES)
            o_r[q, rows, lanes] = m1[DN_CHUNK:] + m2[:DN_CHUNK]
            ci = d * B_HEADS + h
            s_ref[q, d, h] = s * gl_r[q, chunk, ci:ci + 1, :] + m2[DN_CHUNK:]


def _stage3(w, qg, u, qk, kgt, gl, rows_blk):
    _, b, s, _ = w.shape
    n_i = s // rows_blk
    cpb = rows_blk // DN_CHUNK
    nseq = 2 if b % 2 == 0 else 1
    fwd = pl.BlockSpec((1, nseq, rows_blk, D_B), lambda bi, i: (0, bi, i, 0))
    bwd = pl.BlockSpec((1, nseq, rows_blk, D_B), lambda bi, i: (1, bi, n_i - 1 - i, 0))
    return pl.pallas_call(
        _stage3_kernel,
        grid=(b // nseq, n_i),
        in_specs=[fwd] * 5 + [bwd] * 5 + [
            pl.BlockSpec((nseq, cpb, N_GB, LANES), lambda bi, i: (bi, i, 0, 0)),
            pl.BlockSpec((nseq, cpb, N_GB, LANES), lambda bi, i: (bi, n_i - 1 - i, 0, 0)),
        ],
        out_specs=[pl.BlockSpec((nseq, rows_blk, D_B), lambda bi, i: (bi, i, 0)),
                   pl.BlockSpec((nseq, rows_blk, D_B), lambda bi, i: (bi, n_i - 1 - i, 0))],
        out_shape=[jax.ShapeDtypeStruct((b, s, D_B), F32), jax.ShapeDtypeStruct((b, s, D_B), F32)],
        scratch_shapes=[pltpu.VMEM((nseq, N_DIR, B_HEADS, B_DK, B_DV), F32)],
        compiler_params=pltpu.CompilerParams(
            dimension_semantics=("arbitrary", "arbitrary"), vmem_limit_bytes=VMEM_LIMIT),
        name="stage3_gdn_scan",
    )(w, qg, u, qk, kgt, w, qg, u, qk, kgt, gl, gl)


N_ROUTE = N_EXPERT_GROUPS + N_EXPERTS


def _stage4_kernel(x_ref, ya_ref, of_ref, ob_ref, z_ref, gnw_ref, woa_ref, wob_ref, fnw_ref,
                   wr_ref, h_ref, hn_ref, ids_ref, wts_ref, cnt_ref):
    tm = x_ref.shape[0]
    for k in range(tm // MOE_TM):
        cnt_ref[k] = _stage4_rows(slice(k * MOE_TM, (k + 1) * MOE_TM), x_ref, ya_ref, of_ref,
                                  ob_ref, z_ref, gnw_ref, woa_ref, wob_ref, fnw_ref, wr_ref,
                                  h_ref, hn_ref, ids_ref, wts_ref)


def _stage4_rows(sl, x_ref, ya_ref, of_ref, ob_ref, z_ref, gnw_ref, woa_ref, wob_ref, fnw_ref,
                 wr_ref, h_ref, hn_ref, ids_ref, wts_ref):
    tm = sl.stop - sl.start
    o = of_ref[sl, :] + ob_ref[sl, :]
    z = z_ref[sl, :]
    parts = []
    for hd in range(B_HEADS):
        lanes = slice(hd * B_DV, (hd + 1) * B_DV)
        oh = o[:, lanes]
        yh = oh * lax.rsqrt(jnp.mean(oh * oh, axis=-1, keepdims=True) + EPS) * gnw_ref[...]
        parts.append((yh * _silu(z[:, lanes])).astype(BF16))
    yb = jnp.concatenate(parts, axis=1)
    h = x_ref[sl, :] + (_dot(ya_ref[sl, :], woa_ref[...]) + _dot(yb, wob_ref[...]))
    h_ref[sl, :] = h
    hn = h * lax.rsqrt(jnp.mean(h * h, axis=-1, keepdims=True) + EPS) * fnw_ref[...]

    hi = hn.astype(BF16)
    hn_ref[sl, :] = hi
    lo = (hn - hi.astype(F32)).astype(BF16)
    prod = _dot(jnp.concatenate([hi, lo], axis=0), wr_ref[...])
    logits = ((prod[:tm, :LANES] + prod[tm:, :LANES])
              + (prod[:tm, LANES:] + prod[tm:, LANES:]))

    lane = lax.broadcasted_iota(jnp.int32, (tm, LANES), 1)
    neg = -jnp.inf
    is_g = lane < N_EXPERT_GROUPS
    gl = jnp.where(is_g, logits, neg)
    gmax = jnp.max(gl, axis=-1, keepdims=True)
    gidx = jnp.min(jnp.where(gl == gmax, lane, LANES), axis=-1, keepdims=True)
    g_w = 1.0 / jnp.sum(jnp.where(is_g, jnp.exp(gl - gmax), 0.0), axis=-1, keepdims=True)

    e0 = N_EXPERT_GROUPS + gidx * EXPERTS_PER_GROUP
    in_grp = (lane >= e0) & (lane < e0 + EXPERTS_PER_GROUP)
    el = jnp.where(in_grp, logits, neg)
    m1 = jnp.max(el, axis=-1, keepdims=True)
    i1 = jnp.min(jnp.where(el == m1, lane, LANES), axis=-1, keepdims=True)
    el2 = jnp.where(lane == i1, neg, el)
    m2 = jnp.max(el2, axis=-1, keepdims=True)
    i2 = jnp.min(jnp.where(el2 == m2, lane, LANES), axis=-1, keepdims=True)
    e2 = jnp.exp(m2 - m1)
    inv = 1.0 / (1.0 + e2)
    ids_ref[sl, 0:1] = i1 - N_EXPERT_GROUPS
    ids_ref[sl, 1:2] = i2 - N_EXPERT_GROUPS
    wts_ref[sl, 0:1] = g_w * inv
    wts_ref[sl, 1:2] = g_w * (e2 * inv)
    elane = lane + N_EXPERT_GROUPS
    chosen = (elane == i1) | (elane == i2)
    return jnp.sum(jnp.where(chosen, 1.0, 0.0), axis=0, keepdims=True)


def _stage4(x, ya, o_f, o_b, z, gnw, woa, wob, fnw, wr, tm):
    t = x.shape[0]
    full = lambda a: pl.BlockSpec(a.shape, lambda i: (0,) * a.ndim)
    tile = lambda n: pl.BlockSpec((tm, n), lambda i: (i, 0))
    return pl.pallas_call(
        _stage4_kernel,
        grid=(t // tm,),
        in_specs=[tile(D_MODEL), tile(D_A), tile(D_B), tile(D_B), tile(D_B),
                  full(gnw), full(woa), full(wob), full(fnw), full(wr)],
        out_specs=[tile(D_MODEL), tile(D_MODEL), tile(TOP_K), tile(TOP_K),
                   pl.BlockSpec((tm // MOE_TM, 1, LANES), lambda i: (i, 0, 0))],
        out_shape=[jax.ShapeDtypeStruct((t, D_MODEL), F32), jax.ShapeDtypeStruct((t, D_MODEL), BF16),
                   jax.ShapeDtypeStruct((t, TOP_K), jnp.int32),
                   jax.ShapeDtypeStruct((t, TOP_K), F32),
                   jax.ShapeDtypeStruct((t // MOE_TM, 1, LANES), F32)],
        compiler_params=pltpu.CompilerParams(
            dimension_semantics=("arbitrary",), vmem_limit_bytes=VMEM_LIMIT),
        name="stage4_outproj_router",
    )(x, ya, o_f, o_b, z, gnw, woa, wob, fnw, wr)


MOE_TM = 512
MOE_R = 16
MOE_BM = 512
MOE_L = TOP_K * MOE_TM + N_EXPERTS * MOE_R
SCAN_ROWS = 512
OUTPROJ_TM = 2 * MOE_TM


def _moe_plan(cnt):
    n_tiles = cnt.shape[0]
    c = cnt[:, :N_EXPERTS].astype(jnp.int32)
    cpad = (c + MOE_R - 1) // MOE_R * MOE_R
    seg = jnp.sum(cpad, axis=0)
    segpad = (seg + MOE_BM - 1) // MOE_BM * MOE_BM
    pad_end = jnp.cumsum(segpad)
    pad_start = pad_end - segpad
    off = pad_start[None, :] + jnp.cumsum(cpad, axis=0) - cpad
    loc = jnp.cumsum(cpad, axis=1) - cpad
    nch = cpad // MOE_R
    n_blocks = -(-(TOP_K * MOE_TM + N_EXPERTS * (MOE_R - 1)) * n_tiles // MOE_BM) + N_EXPERTS
    n_used = pad_end[-1] // MOE_BM
    blk = jnp.arange(n_blocks, dtype=jnp.int32)
    first_row = jnp.minimum(blk, n_used - 1) * MOE_BM
    block_expert = jnp.sum((pad_end[None, :] <= first_row[:, None]).astype(jnp.int32), axis=1)
    block_expert = jnp.minimum(block_expert, N_EXPERTS - 1)
    loc_lanes = jnp.pad(loc, ((0, 0), (0, LANES - N_EXPERTS))).astype(F32)
    return dict(off=off.reshape(-1), loc=loc.reshape(-1), nch=nch.reshape(-1),
                tot=jnp.sum(nch, axis=1), tail_off=pad_start + seg,
                tail_n=(segpad - seg) // MOE_R, loc_lanes=loc_lanes.reshape(n_tiles, 1, LANES),
                block_expert=block_expert, n_used=n_used.reshape(1), n_blocks=n_blocks)


def _run_chunks(nch_ref, loc_ref, off_ref, tile, visit):
    def per_expert(e, carry):
        idx = tile * N_EXPERTS + e
        loc0 = loc_ref[idx]
        off0 = off_ref[idx]
        n = nch_ref[idx]

        def per_double(j, c):
            step = j * (2 * MOE_R)
            visit(pl.multiple_of(loc0 + step, MOE_R), pl.multiple_of(off0 + step, MOE_R),
                  2 * MOE_R)
            return c
        lax.fori_loop(0, n >> 1, per_double, 0)

        @pl.when((n & 1) == 1)
        def _():
            last = (n - 1) * MOE_R
            visit(pl.multiple_of(loc0 + last, MOE_R), pl.multiple_of(off0 + last, MOE_R), MOE_R)
        return carry
    lax.fori_loop(0, N_EXPERTS, per_expert, 0)


def _wait_chunks(count, chunk_wait):
    for bit in range((MOE_L // MOE_R).bit_length()):
        @pl.when(((count >> bit) & 1) == 1)
        def _():
            chunk_wait((1 << bit) * MOE_R)


def _dispatch_kernel(n, n_first, n_blocks, off_ref, loc_ref, nch_ref, tot_ref, toff_ref, tn_ref,
                     nu_ref, ids_ref, hna_ref, hnb_ref, locl_ref, xs_ref, pos_ref,
                     hn_ref, xl, zrows, sem, tsem):
    i = pl.program_id(0)
    g = i
    slot = lax.rem(i, 2)
    tm = ids_ref.shape[0]

    @pl.when(i < n_first)
    def _():
        hn_ref[...] = hna_ref[...]

    @pl.when(i >= n_first)
    def _():
        hn_ref[...] = hnb_ref[...]

    ids = ids_ref[...]
    lane = lax.broadcasted_iota(jnp.int32, (tm, LANES), 1)
    oh0 = jnp.where(lane == ids[:, 0:1], 1.0, 0.0)
    oh1 = jnp.where(lane == ids[:, 1:2], 1.0, 0.0)
    row = lax.broadcasted_iota(jnp.int32, (tm, tm), 0)
    col = lax.broadcasted_iota(jnp.int32, (tm, tm), 1)
    earlier = jnp.where(row > col, 1.0, 0.0).astype(BF16)
    base = _dot(earlier, (oh0 + oh1).astype(BF16)) + locl_ref[0]
    m0 = base * oh0
    m1 = base * oh1
    pos_ref[:, 0:1] = jnp.sum(m0, axis=-1, keepdims=True).astype(jnp.int32)
    pos_ref[:, 1:2] = jnp.sum(m1, axis=-1, keepdims=True).astype(jnp.int32)
    ones = jnp.ones((8, LANES), BF16)
    lane_form = lambda m: sum(_dot_nt(ones, part) for part in _split3(m))[0:1].astype(jnp.int32)
    p0 = lane_form(m0)
    p1 = lane_form(m1)
    srow = lax.broadcasted_iota(jnp.int32, (MOE_L, tm), 0)
    perm = jnp.where((srow == p0) | (srow == p1), 1.0, 0.0).astype(BF16)
    nb = 256
    for cb in range(D_MODEL // nb):
        xl[slot, :, cb * nb:(cb + 1) * nb] = _dot(perm, hn_ref[:, cb * nb:(cb + 1) * nb]).astype(BF16)

    def rows_copy(sl, lrow, grow, nrows):
        return pltpu.make_async_copy(xl.at[sl, pl.ds(lrow, nrows)],
                                     xs_ref.at[pl.ds(grow, nrows)], sem.at[sl])

    _run_chunks(nch_ref, loc_ref, off_ref, g,
                lambda lrow, grow, nrows: rows_copy(slot, lrow, grow, nrows).start())

    def wait_tile(tile, sl):
        _wait_chunks(tot_ref[tile], lambda nrows: rows_copy(sl, 0, 0, nrows).wait())

    @pl.when(i > 0)
    def _():
        wait_tile(g - 1, 1 - slot)

    @pl.when(i == n - 1)
    def _():
        wait_tile(g, slot)
        zrows[...] = jnp.zeros_like(zrows)

        def tail_copy(e, j):
            row0 = pl.multiple_of(toff_ref[e] + j * MOE_R, MOE_R)
            return pltpu.make_async_copy(zrows.at[pl.ds(0, MOE_R)],
                                         xs_ref.at[pl.ds(row0, MOE_R)], tsem.at[0])

        def block_copy(b):
            row0 = pl.multiple_of(b * MOE_BM, MOE_BM)
            return pltpu.make_async_copy(zrows, xs_ref.at[pl.ds(row0, MOE_BM)], tsem.at[0])

        def fill(act):
            def per_expert(e, carry):
                def per_chunk(j, c):
                    act(tail_copy(e, j))
                    return c
                lax.fori_loop(0, tn_ref[e], per_chunk, 0)
                return carry
            lax.fori_loop(0, N_EXPERTS, per_expert, 0)

            def per_block(b, c):
                act(block_copy(b))
                return c
            lax.fori_loop(nu_ref[0], n_blocks, per_block, 0)

        fill(lambda cp: cp.start())
        fill(lambda cp: cp.wait())


def _dispatch(plan, ids, hn_a, hn_b):
    n_a = hn_a.shape[0] // MOE_TM
    n = ids.shape[0] // MOE_TM
    n_blocks = plan['n_blocks']
    grid_spec = pltpu.PrefetchScalarGridSpec(
        num_scalar_prefetch=7,
        grid=(n,),
        in_specs=[
            pl.BlockSpec((MOE_TM, TOP_K), lambda i, *_: (i, 0)),
            pl.BlockSpec((MOE_TM, D_MODEL), lambda i, *_: (jnp.minimum(i, n_a - 1), 0)),
            pl.BlockSpec((MOE_TM, D_MODEL), lambda i, *_: (jnp.maximum(i - n_a, 0), 0)),
            pl.BlockSpec((1, 1, LANES), lambda i, *_: (i, 0, 0)),
        ],
        out_specs=[pl.BlockSpec(memory_space=pl.ANY),
                   pl.BlockSpec((MOE_TM, TOP_K), lambda i, *_: (i, 0))],
        scratch_shapes=[pltpu.VMEM((MOE_TM, D_MODEL), BF16),
                        pltpu.VMEM((2, MOE_L, D_MODEL), BF16), pltpu.VMEM((MOE_BM, D_MODEL), BF16),
                        pltpu.SemaphoreType.DMA((2,)), pltpu.SemaphoreType.DMA((1,))],
    )
    return pl.pallas_call(
        functools.partial(_dispatch_kernel, n, n_a, n_blocks),
        grid_spec=grid_spec,
        out_shape=[jax.ShapeDtypeStruct((n_blocks * MOE_BM, D_MODEL), BF16),
                   jax.ShapeDtypeStruct((ids.shape[0], TOP_K), jnp.int32)],
        compiler_params=pltpu.CompilerParams(
            dimension_semantics=("arbitrary",), vmem_limit_bytes=VMEM_LIMIT),
        name="stage5a_dispatch",
    )(plan['off'], plan['loc'], plan['nch'], plan['tot'], plan['tail_off'], plan['tail_n'],
      plan['n_used'], ids, hn_a, hn_b, plan['loc_lanes'])


def _expert_kernel(be_ref, nu_ref, xs_ref, wg_ref, wu_ref, wd_ref, y_ref, wg16, wu16, wd16):
    b = pl.program_id(0)

    @pl.when((b == 0) | (be_ref[b] != be_ref[jnp.maximum(b - 1, 0)]))
    def _():
        wg16[...] = wg_ref[0].astype(BF16)
        wu16[...] = wu_ref[0].astype(BF16)
        wd16[...] = wd_ref[0].astype(BF16)

    @pl.when(b < nu_ref[0])
    def _():
        x = xs_ref[...]
        g = _dot(x, wg16[...])
        u = _dot(x, wu16[...])
        y_ref[...] = _dot((_silu(g) * u).astype(BF16), wd16[...]).astype(BF16)

    @pl.when(b >= nu_ref[0])
    def _():
        y_ref[...] = jnp.zeros_like(y_ref)


def _experts(plan, xs, wg, wu, wd):
    n_blocks = plan['n_blocks']
    used = lambda b, nu: jnp.minimum(b, nu[0] - 1)
    grid_spec = pltpu.PrefetchScalarGridSpec(
        num_scalar_prefetch=2,
        grid=(n_blocks,),
        in_specs=[
            pl.BlockSpec((MOE_BM, D_MODEL), lambda b, be, nu: (used(b, nu), 0)),
            pl.BlockSpec((1, D_MODEL, D_EXPERT), lambda b, be, nu: (be[b], 0, 0)),
            pl.BlockSpec((1, D_MODEL, D_EXPERT), lambda b, be, nu: (be[b], 0, 0)),
            pl.BlockSpec((1, D_EXPERT, D_MODEL), lambda b, be, nu: (be[b], 0, 0)),
        ],
        out_specs=pl.BlockSpec((MOE_BM, D_MODEL), lambda b, be, nu: (b, 0)),
        scratch_shapes=[pltpu.VMEM((D_MODEL, D_EXPERT), BF16), pltpu.VMEM((D_MODEL, D_EXPERT), BF16),
                        pltpu.VMEM((D_EXPERT, D_MODEL), BF16)],
    )
    return pl.pallas_call(
        _expert_kernel,
        grid_spec=grid_spec,
        out_shape=jax.ShapeDtypeStruct((n_blocks * MOE_BM, D_MODEL), BF16),
        compiler_params=pltpu.CompilerParams(
            dimension_semantics=("arbitrary",), vmem_limit_bytes=VMEM_LIMIT),
        name="stage5b_experts",
    )(plan['block_expert'], plan['n_used'], xs, wg, wu, wd)


def _combine_kernel(n, tile_base, off_ref, loc_ref, nch_ref, tot_ref, pos_ref, wts_ref, h_ref,
                    fw_ref, y_hbm, out_ref, yl, sem):
    i = pl.program_id(0)
    g = tile_base + i
    slot = lax.rem(i, 2)
    tm = h_ref.shape[0]

    def rows_copy(sl, lrow, grow, nrows):
        return pltpu.make_async_copy(y_hbm.at[pl.ds(grow, nrows)],
                                     yl.at[sl, pl.ds(lrow, nrows)], sem.at[sl])

    def fetch(tile, sl):
        _run_chunks(nch_ref, loc_ref, off_ref, tile,
                    lambda lrow, grow, nrows: rows_copy(sl, lrow, grow, nrows).start())

    @pl.when(i == 0)
    def _():
        yl[...] = jnp.zeros_like(yl)
        fetch(g, 0)

    @pl.when(i + 1 < n)
    def _():
        fetch(g + 1, 1 - slot)

    _wait_chunks(tot_ref[g], lambda nrows: rows_copy(slot, 0, 0, nrows).wait())

    lane = lax.broadcasted_iota(jnp.int32, (tm, MOE_L), 1)
    pos = pos_ref[...]
    w = wts_ref[...]
    sel = (jnp.where(lane == pos[:, 0:1], w[:, 0:1], 0.0)
           + jnp.where(lane == pos[:, 1:2], w[:, 1:2], 0.0)).astype(BF16)
    h = h_ref[...] + _dot(sel, yl[slot])
    out_ref[...] = h * lax.rsqrt(jnp.mean(h * h, axis=-1, keepdims=True) + EPS) * fw_ref[...]


def _combine(plan, tile_base, pos, wts, h, fw, y_rows):
    t = h.shape[0]
    n = t // MOE_TM
    grid_spec = pltpu.PrefetchScalarGridSpec(
        num_scalar_prefetch=4,
        grid=(n,),
        in_specs=[
            pl.BlockSpec((MOE_TM, TOP_K), lambda i, *_: (i, 0)),
            pl.BlockSpec((MOE_TM, TOP_K), lambda i, *_: (i, 0)),
            pl.BlockSpec((MOE_TM, D_MODEL), lambda i, *_: (i, 0)),
            pl.BlockSpec((1, D_MODEL), lambda i, *_: (0, 0)),
            pl.BlockSpec(memory_space=pl.ANY),
        ],
        out_specs=pl.BlockSpec((MOE_TM, D_MODEL), lambda i, *_: (i, 0)),
        scratch_shapes=[pltpu.VMEM((2, MOE_L, D_MODEL), BF16), pltpu.SemaphoreType.DMA((2,))],
    )
    return pl.pallas_call(
        functools.partial(_combine_kernel, n, tile_base),
        grid_spec=grid_spec,
        out_shape=jax.ShapeDtypeStruct((t, D_MODEL), F32),
        compiler_params=pltpu.CompilerParams(
            dimension_semantics=("arbitrary",), vmem_limit_bytes=VMEM_LIMIT),
        name="stage6_combine_norm",
    )(plan['off'], plan['loc'], plan['nch'], plan['tot'], pos, wts, h, fw, y_rows)


def kernel(x_prompt, x_sample, norm_mix_w, w_in, a_ln_w, a_ln_b, a_spatial_w, a_spatial_b, a_out_norm_w, conv_w, a_log, dt_bias, gdn_norm_w, w_out, norm_ffn_w, w_router_group, w_router_expert, w_gate, w_up, w_down, norm_final_w):
    s1w = _prep_stage1_weights(norm_mix_w[0], w_in[0], a_ln_w[0], a_ln_b[0], a_spatial_w[0],
                               a_spatial_b[0], a_out_norm_w[0], a_log[0], dt_bias[0])
    woa = w_out[0, :D_A].astype(BF16)
    wob = w_out[0, D_A:].astype(BF16)
    w_r = jnp.concatenate([w_router_group[0], w_router_expert[0]], axis=1)
    w_r = jnp.pad(w_r, ((0, 0), (0, LANES - N_ROUTE)))
    wrh = w_r.astype(BF16)
    wr = jnp.concatenate([wrh, (w_r - wrh.astype(F32)).astype(BF16)], axis=1)
    gnw = gdn_norm_w[0].reshape(1, B_DV)
    fnw = norm_ffn_w[0].reshape(1, D_MODEL)

    per_run = []
    for x in (x_prompt, x_sample):
        b, s, d = x.shape
        assert d == D_MODEL and s % MIXER_TM == 0 and s % SCAN_ROWS == 0
        assert (b * s) % OUTPROJ_TM == 0
        x2 = x.reshape(b * s, d)
        ya, z, w, qg, u, qk, kgt, gl = _stage1(x2, s, *s1w, conv_w[0], tm=MIXER_TM)
        per_seq = lambda a: a.reshape(N_DIR, b, s, D_B)
        o_f, o_b = _stage3(per_seq(w), per_seq(qg), per_seq(u), per_seq(qk), per_seq(kgt),
                           gl.reshape(b, s // DN_CHUNK, N_GB, LANES), rows_blk=SCAN_ROWS)
        h, hn, ids, wts, cnt = _stage4(x2, ya, o_f.reshape(b * s, D_B), o_b.reshape(b * s, D_B),
                                       z, gnw, woa, wob, fnw, wr, tm=OUTPROJ_TM)
        per_run.append(dict(shape=x.shape, h=h, hn=hn, ids=ids, wts=wts, cnt=cnt[:, 0, :]))

    plan = _moe_plan(jnp.concatenate([r['cnt'] for r in per_run], axis=0))
    ids_all = jnp.concatenate([r['ids'] for r in per_run], axis=0)
    xs, pos_all = _dispatch(plan, ids_all, per_run[0]['hn'], per_run[1]['hn'])
    y_rows = _experts(plan, xs, w_gate[0], w_up[0], w_down[0])
    fw = norm_final_w.reshape(1, D_MODEL)
    outs = []
    t0 = 0
    for r in per_run:
        t = r['h'].shape[0]
        outs.append(_combine(plan, t0 // MOE_TM, pos_all[t0:t0 + t], r['wts'], r['h'], fw,
                             y_rows).reshape(r['shape']))
        t0 += t
    return tuple(outs)
```
